```python
import math
import jax, jax.numpy as jnp
from jax import lax
import numpy as np

D_MODEL = 4096
BATCH = 8
SEQ = 2048
DEPTH = 1

CHUNK = 64
Q_BLOCK = 128
MLA_HEADS = 16
MLA_NOPE = 128
MLA_ROPE = 64
MLA_V = 128
Q_LORA = 1024
KV_LORA = 512
ROPE_THETA = 10000.0
SWA_HEADS = 32
SWA_KV_HEADS = 4
SWA_HEAD_DIM = 64
WINDOW = 128
SWA_BLOCK = 128
NUM_BUCKETS = 32
MAX_DISTANCE = 128
D_FF = 4 * D_MODEL
EPS = 1e-6
NEG = -1e30

IN_SIZES = (Q_LORA, KV_LORA, MLA_ROPE, SWA_HEADS * SWA_HEAD_DIM,
            SWA_KV_HEADS * SWA_HEAD_DIM, SWA_KV_HEADS * SWA_HEAD_DIM)
D_IN = sum(IN_SIZES)
MLA_WIDTH = MLA_HEADS * MLA_V
SWA_WIDTH = SWA_HEADS * SWA_HEAD_DIM

kernel_name = "hybrid_mla_swa_sink_gated_block"


def rmsnorm(x, g):
    xf = x.astype(jnp.float32)
    y = xf * lax.rsqrt(jnp.mean(xf * xf, axis=-1, keepdims=True) + EPS)
    return (y * g.astype(jnp.float32)).astype(x.dtype)


def rope_tables(positions, dim):
    half = dim // 2
    inv = ROPE_THETA ** (-jnp.arange(half, dtype=jnp.float32) * (2.0 / dim))
    ang = positions.astype(jnp.float32)[..., None] * inv
    return jnp.cos(ang), jnp.sin(ang)


def apply_rope(x, cos, sin):
    half = x.shape[-1] // 2
    xf = x.astype(jnp.float32)
    x1, x2 = xf[..., :half], xf[..., half:]
    return jnp.concatenate([x1 * cos - x2 * sin, x1 * sin + x2 * cos], axis=-1).astype(x.dtype)


def t5_bucket(rel):
    nb = NUM_BUCKETS // 2
    max_exact = nb // 2
    ret = jnp.where(rel > 0, nb, 0)
    n = jnp.abs(rel)
    nf = jnp.maximum(n, 1).astype(jnp.float32)
    large = max_exact + (jnp.log(nf / max_exact) / math.log(MAX_DISTANCE / max_exact)
                         * (nb - max_exact)).astype(jnp.int32)
    large = jnp.minimum(large, nb - 1)
    return ret + jnp.where(n < max_exact, n, large)


def mla_attention(q_lat, kv_lat, k_rope, positions, q_norm_g, kv_norm_g, w_uq, w_ukv):
    B, S, _ = q_lat.shape
    q = (rmsnorm(q_lat, q_norm_g) @ w_uq).reshape(B, S, MLA_HEADS, MLA_NOPE + MLA_ROPE)
    q_nope, q_pe = q[..., :MLA_NOPE], q[..., MLA_NOPE:]
    kv = (rmsnorm(kv_lat, kv_norm_g) @ w_ukv).reshape(B, S, MLA_HEADS, MLA_NOPE + MLA_V)
    k_nope, v = kv[..., :MLA_NOPE], kv[..., MLA_NOPE:]
    cos, sin = rope_tables(positions, MLA_ROPE)
    q_pe = apply_rope(q_pe, cos[:, :, None, :], sin[:, :, None, :])
    k_pe = apply_rope(k_rope, cos, sin)
    scale = (MLA_NOPE + MLA_ROPE) ** -0.5
    outs = []
    for i in range(S // Q_BLOCK):
        q0, kend = i * Q_BLOCK, (i + 1) * Q_BLOCK
        s = (jnp.einsum('bqhd,bkhd->bhqk', q_nope[:, q0:kend], k_nope[:, :kend])
             + jnp.einsum('bqhd,bkd->bhqk', q_pe[:, q0:kend], k_pe[:, :kend]))
        s = s.astype(jnp.float32) * scale
        qc = (q0 + jnp.arange(Q_BLOCK)) // CHUNK
        kc = jnp.arange(kend) // CHUNK
        s = jnp.where((kc[None, :] <= qc[:, None])[None, None], s, NEG)
        p = jax.nn.softmax(s, axis=-1).astype(v.dtype)
        outs.append(jnp.einsum('bhqk,bkhd->bqhd', p, v[:, :kend]))
    return jnp.concatenate(outs, axis=1).reshape(B, S, MLA_WIDTH)


def swa_attention(q, k, v, sinks, rel_bias):
    B, S, _ = q.shape
    NB = S // SWA_BLOCK
    G = SWA_HEADS // SWA_KV_HEADS
    qb = q.reshape(B, NB, SWA_BLOCK, SWA_KV_HEADS, G, SWA_HEAD_DIM)

    def band(t):
        t = t.reshape(B, S, SWA_KV_HEADS, SWA_HEAD_DIM)
        tp = jnp.pad(t, ((0, 0), (SWA_BLOCK, 0), (0, 0), (0, 0)))
        tp = tp.reshape(B, NB + 1, SWA_BLOCK, SWA_KV_HEADS, SWA_HEAD_DIM)
        return jnp.concatenate([tp[:, :-1], tp[:, 1:]], axis=2)

    kb, vb = band(k), band(v)
    s = jnp.einsum('bnqgrd,bnkgd->bngrqk', qb, kb).astype(jnp.float32) * SWA_HEAD_DIM ** -0.5
    iq = jnp.arange(SWA_BLOCK)
    ik = jnp.arange(2 * SWA_BLOCK) - SWA_BLOCK
    rel = ik[None, :] - iq[:, None]
    bias = rel_bias.astype(jnp.float32)[t5_bucket(rel)]
    bias = jnp.transpose(bias, (2, 0, 1)).reshape(SWA_KV_HEADS, G, SWA_BLOCK, 2 * SWA_BLOCK)
    blk = jnp.arange(NB)[:, None] * SWA_BLOCK
    qpos = blk + iq
    kpos = blk + ik
    qc = qpos[:, :, None] // CHUNK
    kc = kpos[:, None, :] // CHUNK
    valid = (kc <= qc) & (kc >= qc - WINDOW // CHUNK) & (kpos[:, None, :] >= 0)
    s = jnp.where(valid[None, :, None, None], s + bias, NEG)
    sk = sinks.astype(jnp.float32).reshape(1, 1, SWA_KV_HEADS, G, 1, 1)
    m = jnp.maximum(jnp.max(s, axis=-1, keepdims=True), sk)
    e = jnp.exp(s - m)
    p = e / (jnp.sum(e, axis=-1, keepdims=True) + jnp.exp(sk - m))
    o = jnp.einsum('bngrqk,bnkgd->bnqgrd', p.astype(v.dtype), vb)
    return o.reshape(B, S, SWA_WIDTH)


def setup_inputs(seed: int = 0) -> dict:
    key = jax.random.key(seed)
    ks = jax.random.split(key, 24)
    f32 = jnp.float32

    def nrm(k, shape, scale):
        return jax.random.normal(k, shape, f32) * scale

    x = nrm(ks[0], (BATCH, SEQ, D_MODEL), 1.0)
    c = nrm(ks[1], (BATCH, D_MODEL), 1.0)
    offset = jax.random.randint(ks[2], (BATCH, 1), 0, 64, dtype=jnp.int32) * CHUNK
    positions = (offset + jnp.arange(SEQ, dtype=jnp.int32)[None, :]).astype(jnp.int32)
    return {
        "x": x,
        "c": c,
        "positions": positions,
        "w_ada": nrm(ks[3], (DEPTH, D_MODEL, 6 * D_MODEL), 0.1 * D_MODEL ** -0.5),
        "b_ada": nrm(ks[4], (DEPTH, 6 * D_MODEL), 0.02),
        "pre_norm_g": 1.0 + nrm(ks[5], (DEPTH, 2, D_MODEL), 0.05),
        "post_norm_g": 1.0 + nrm(ks[6], (DEPTH, 2, D_MODEL), 0.05),
        "w_in": nrm(ks[7], (DEPTH, D_MODEL, D_IN), D_MODEL ** -0.5),
        "q_norm_g": 1.0 + nrm(ks[8], (DEPTH, Q_LORA), 0.05),
        "kv_norm_g": 1.0 + nrm(ks[9], (DEPTH, KV_LORA), 0.05),
        "w_uq": nrm(ks[10], (DEPTH, Q_LORA, MLA_HEADS * (MLA_NOPE + MLA_ROPE)), Q_LORA ** -0.5),
        "w_ukv": nrm(ks[11], (DEPTH, KV_LORA, MLA_HEADS * (MLA_NOPE + MLA_V)), KV_LORA ** -0.5),
        "swa_sinks": nrm(ks[12], (DEPTH, SWA_HEADS), 0.5),
        "rel_bias": nrm(ks[13], (NUM_BUCKETS, SWA_HEADS), 0.5),
        "w_gate": nrm(ks[14], (DEPTH, D_MODEL, 2 * D_MODEL), D_MODEL ** -0.5),
        "w_proj_a": nrm(ks[15], (DEPTH, MLA_WIDTH, D_MODEL), MLA_WIDTH ** -0.5),
        "w_proj_b": nrm(ks[16], (DEPTH, SWA_WIDTH, D_MODEL), SWA_WIDTH ** -0.5),
        "w_out": nrm(ks[17], (DEPTH, D_MODEL, D_MODEL), D_MODEL ** -0.5),
        "w_ff_up": nrm(ks[18], (DEPTH, D_MODEL, D_FF), D_MODEL ** -0.5),
        "w_ff_down": nrm(ks[19], (DEPTH, D_FF, D_MODEL), D_FF ** -0.5),
    }


def reference(x, c, positions, w_ada, b_ada, pre_norm_g, post_norm_g, w_in, q_norm_g,
              kv_norm_g, w_uq, w_ukv, swa_sinks, rel_bias, w_gate, w_proj_a, w_proj_b,
              w_out, w_ff_up, w_ff_down):
    offs = np.cumsum((0,) + IN_SIZES)
    c_act = jax.nn.silu(c)
    for l in range(DEPTH):
        mod = (c_act @ w_ada[l] + b_ada[l])[:, None, :]
        sh1, sc1, g1, sh2, sc2, g2 = jnp.split(mod, 6, axis=-1)

        h = rmsnorm(x, pre_norm_g[l, 0]) * (1.0 + sc1) + sh1
        proj = h @ w_in[l]
        q_lat, kv_lat, k_rope, q_s, k_s, v_s = [proj[..., offs[i]:offs[i + 1]] for i in range(6)]
        y_a = mla_attention(q_lat, kv_lat, k_rope, positions, q_norm_g[l], kv_norm_g[l],
                            w_uq[l], w_ukv[l]) @ w_proj_a[l]
        y_b = swa_attention(q_s, k_s, v_s, swa_sinks[l], rel_bias) @ w_proj_b[l]
        gates = jax.nn.sigmoid(h @ w_gate[l])
        g_a, g_b = gates[..., :D_MODEL], gates[..., D_MODEL:]
        y = (g_a * y_a + g_b * y_b) @ w_out[l]
        x = x + g1 * rmsnorm(y, post_norm_g[l, 0])

        h = rmsnorm(x, pre_norm_g[l, 1]) * (1.0 + sc2) + sh2
        y = jnp.square(jax.nn.relu(h @ w_ff_up[l])) @ w_ff_down[l]
        x = x + g2 * rmsnorm(y, post_norm_g[l, 1])
    return x
```

```python
import functools
import math

import numpy as np
import jax
import jax.numpy as jnp
from jax import lax
from jax.experimental import pallas as pl
from jax.experimental.pallas import tpu as pltpu

F32 = jnp.float32
BF16 = jnp.bfloat16

D_MODEL = 4096
CHUNK = 64
MLA_HEADS = 16
MLA_NOPE = 128
MLA_ROPE = 64
MLA_V = 128
Q_LORA = 1024
KV_LORA = 512
ROPE_THETA = 10000.0
SWA_HEADS = 32
SWA_KV_HEADS = 4
SWA_HEAD_DIM = 64
SWA_GROUP = SWA_HEADS // SWA_KV_HEADS
WINDOW = 128
SWA_BLOCK = 128
NUM_BUCKETS = 32
MAX_DISTANCE = 128
EPS = 1e-6
NEG = -1e30

LANES = 128
MLA_QK_PAD = 256
SWA_Q_WIDTH = SWA_HEADS * SWA_HEAD_DIM
SWA_KV_WIDTH = SWA_KV_HEADS * SWA_HEAD_DIM
VMEM_CAP_BYTES = 56 * 1024 * 1024


def _vmem_limit(block_bytes, scratch_bytes=0, temp_bytes=0):
    need = 2 * block_bytes + scratch_bytes + temp_bytes + (4 << 20)
    return int(min(max(need, 16 << 20), VMEM_CAP_BYTES))


def _params(sem, vmem):
    return pltpu.CompilerParams(dimension_semantics=sem, vmem_limit_bytes=vmem)


def _rms(x):
    return x * lax.rsqrt(jnp.mean(x * x, axis=-1, keepdims=True) + EPS)


def _rope_half_swap(x, cos_t, sin_t):
    lane = lax.broadcasted_iota(jnp.int32, x.shape, 1)
    swapped = jnp.where(lane < MLA_ROPE // 2,
                        pltpu.roll(x, LANES - MLA_ROPE // 2, 1),
                        pltpu.roll(x, MLA_ROPE // 2, 1))
    return x * cos_t + swapped * sin_t


def _mod_body(c_ref, w_ref, b_ref, o_ref):
    c = c_ref[...]
    ca = (c * jax.nn.sigmoid(c)).astype(BF16)
    o_ref[...] = jnp.dot(ca, w_ref[...].astype(BF16), preferred_element_type=F32) + b_ref[...]


def _modulation(c_pad, w_ada, b_ada):
    m, k = c_pad.shape
    n = w_ada.shape[1]
    tn = 512
    blocks = m * k * 4 + k * tn * 4 + tn * 4 + m * tn * 4
    return pl.pallas_call(
        _mod_body,
        out_shape=jax.ShapeDtypeStruct((m, n), F32),
        grid=(n // tn,),
        in_specs=[pl.BlockSpec((m, k), lambda j: (0, 0)),
                  pl.BlockSpec((k, tn), lambda j: (0, j)),
                  pl.BlockSpec((1, tn), lambda j: (0, j))],
        out_specs=pl.BlockSpec((m, tn), lambda j: (0, j)),
        compiler_params=_params(("arbitrary",), _vmem_limit(blocks, temp_bytes=k * tn * 2)),
        name="mod",
    )(c_pad, w_ada, b_ada)


def _prenorm_body(x_ref, mod_ref, g_ref, o_ref):
    y = _rms(x_ref[...]) * g_ref[...]
    o_ref[...] = (y * (1.0 + mod_ref[1:2, :]) + mod_ref[0:1, :]).astype(o_ref.dtype)


def _prenorm(x2, mod3, g, seq, tr=256):
    rows, d = x2.shape
    per_b = seq // tr
    blocks = tr * d * 4 + 6 * d * 4 + d * 4 + tr * d * 2
    return pl.pallas_call(
        _prenorm_body,
        out_shape=jax.ShapeDtypeStruct((rows, d), BF16),
        grid=(rows // tr,),
        in_specs=[pl.BlockSpec((tr, d), lambda i: (i, 0)),
                  pl.BlockSpec((None, 6, d), lambda i: (i // per_b, 0, 0)),
                  pl.BlockSpec((1, d), lambda i: (0, 0))],
        out_specs=pl.BlockSpec((tr, d), lambda i: (i, 0)),
        compiler_params=_params(("arbitrary",), _vmem_limit(blocks, temp_bytes=3 * tr * d * 4)),
        name="prenorm",
    )(x2, mod3, g)


def _post1_body(x_ref, y_ref, mod_ref, pg_ref, g2_ref, x1_ref, h2_ref):
    yn = _rms(y_ref[...].astype(F32)) * pg_ref[...]
    x1 = x_ref[...] + mod_ref[2:3, :] * yn
    x1_ref[...] = x1
    h = _rms(x1) * g2_ref[...]
    h2_ref[...] = (h * (1.0 + mod_ref[4:5, :]) + mod_ref[3:4, :]).astype(h2_ref.dtype)


def _post1(x2, y, mod3, post_g, pre_g2, seq, tr=256):
    rows, d = x2.shape
    per_b = seq // tr
    blocks = tr * d * (4 + 2 + 4 + 2) + 8 * d * 4
    row = pl.BlockSpec((tr, d), lambda i: (i, 0))
    vec = pl.BlockSpec((1, d), lambda i: (0, 0))
    return pl.pallas_call(
        _post1_body,
        out_shape=(jax.ShapeDtypeStruct((rows, d), F32), jax.ShapeDtypeStruct((rows, d), BF16)),
        grid=(rows // tr,),
        in_specs=[row, row, pl.BlockSpec((None, 6, d), lambda i: (i // per_b, 0, 0)), vec, vec],
        out_specs=(row, row),
        compiler_params=_params(("arbitrary",), _vmem_limit(blocks, temp_bytes=4 * tr * d * 4)),
        name="post1",
    )(x2, y, mod3, post_g, pre_g2)


def _post2_body(x_ref, y_ref, mod_ref, pg_ref, o_ref):
    yn = _rms(y_ref[...].astype(F32)) * pg_ref[...]
    o_ref[...] = x_ref[...] + mod_ref[5:6, :] * yn


def _post2(x1, y, mod3, post_g, seq, tr=256):
    rows, d = x1.shape
    per_b = seq // tr
    blocks = tr * d * (4 + 2 + 4) + 7 * d * 4
    row = pl.BlockSpec((tr, d), lambda i: (i, 0))
    return pl.pallas_call(
        _post2_body,
        out_shape=jax.ShapeDtypeStruct((rows, d), F32),
        grid=(rows // tr,),
        in_specs=[row, row, pl.BlockSpec((None, 6, d), lambda i: (i // per_b, 0, 0)),
                  pl.BlockSpec((1, d), lambda i: (0, 0))],
        out_specs=row,
        compiler_params=_params(("arbitrary",), _vmem_limit(blocks, temp_bytes=3 * tr * d * 4)),
        name="post2",
    )(x1, y, mod3, post_g)


def _mm_body(x_ref, w_ref, o_ref, *, act):
    acc = jnp.dot(x_ref[...], w_ref[...], preferred_element_type=F32)
    if act == "relu2":
        r = jnp.maximum(acc, 0.0)
        acc = r * r
    o_ref[...] = acc.astype(o_ref.dtype)


def _matmul(x, w, *, act=None, tm=1024, tn=1024, name):
    m, k = x.shape
    n = w.shape[1]
    tn = min(tn, n)
    blocks = tm * k * 2 + k * tn * 2 + tm * tn * 2
    return pl.pallas_call(
        functools.partial(_mm_body, act=act),
        out_shape=jax.ShapeDtypeStruct((m, n), BF16),
        grid=(m // tm, n // tn),
        in_specs=[pl.BlockSpec((tm, k), lambda i, j: (i, 0)),
                  pl.BlockSpec((k, tn), lambda i, j: (0, j))],
        out_specs=pl.BlockSpec((tm, tn), lambda i, j: (i, j)),
        compiler_params=_params(("arbitrary", "arbitrary"), _vmem_limit(blocks, temp_bytes=2 * tm * tn * 4)),
        name=name,
    )(x, w)


def _mm_rope_body(x_ref, w_ref, cos_ref, sin_ref, o_ref):
    acc = jnp.dot(x_ref[...], w_ref[...], preferred_element_type=F32)
    o_ref[...] = _rope_half_swap(acc, cos_ref[...], sin_ref[...]).astype(o_ref.dtype)


def _matmul_rope(x, w, cos_t, sin_t, *, tm=1024):
    m, k = x.shape
    n = w.shape[1]
    blocks = tm * k * 2 + k * n * 2 + 2 * tm * n * 4 + tm * n * 2
    return pl.pallas_call(
        _mm_rope_body,
        out_shape=jax.ShapeDtypeStruct((m, n), BF16),
        grid=(m // tm,),
        in_specs=[pl.BlockSpec((tm, k), lambda i: (i, 0)),
                  pl.BlockSpec((k, n), lambda i: (0, 0)),
                  pl.BlockSpec((tm, n), lambda i: (i, 0)),
                  pl.BlockSpec((tm, n), lambda i: (i, 0))],
        out_specs=pl.BlockSpec((tm, n), lambda i: (i, 0)),
        compiler_params=_params(("arbitrary",), _vmem_limit(blocks, temp_bytes=4 * tm * n * 4)),
        name="mm_krope",
    )(x, w, cos_t, sin_t)


def _mm_norm_body(x_ref, g_ref, w_ref, *rest, rope):
    xn = (_rms(x_ref[...].astype(F32)) * g_ref[...]).astype(BF16)
    acc = jnp.dot(xn, w_ref[...], preferred_element_type=F32)
    if not rope:
        (o_ref,) = rest
        o_ref[...] = acc.astype(o_ref.dtype)
        return
    cos_ref, sin_ref, o_ref = rest
    cos_t = cos_ref[...]
    sin_t = sin_ref[...]
    for hd in range(acc.shape[1] // MLA_QK_PAD):
        c0 = hd * MLA_QK_PAD
        o_ref[:, c0:c0 + MLA_NOPE] = acc[:, c0:c0 + MLA_NOPE].astype(o_ref.dtype)
        pe = _rope_half_swap(acc[:, c0 + MLA_NOPE:c0 + MLA_QK_PAD], cos_t, sin_t)
        o_ref[:, c0 + MLA_NOPE:c0 + MLA_QK_PAD] = pe.astype(o_ref.dtype)


def _matmul_norm(x, xcol, kdim, g, w, *, rope_tables=None, tm=1024, tn=1024, name):
    m = x.shape[0]
    n = w.shape[1]
    blocks = tm * kdim * 2 + kdim * 4 + kdim * tn * 2 + tm * tn * 2
    in_specs = [pl.BlockSpec((tm, kdim), lambda i, j: (i, xcol)),
                pl.BlockSpec((1, kdim), lambda i, j: (0, 0)),
                pl.BlockSpec((kdim, tn), lambda i, j: (0, j))]
    args = [x, g, w]
    if rope_tables is not None:
        in_specs += [pl.BlockSpec((tm, LANES), lambda i, j: (i, 0))] * 2
        args += list(rope_tables)
        blocks += 2 * tm * LANES * 4
    return pl.pallas_call(
        functools.partial(_mm_norm_body, rope=rope_tables is not None),
        out_shape=jax.ShapeDtypeStruct((m, n), BF16),
        grid=(m // tm, n // tn),
        in_specs=in_specs,
        out_specs=pl.BlockSpec((tm, tn), lambda i, j: (i, j)),
        compiler_params=_params(("arbitrary", "arbitrary"),
                                _vmem_limit(blocks, temp_bytes=2 * tm * tn * 4 + 3 * tm * kdim * 4)),
        name=name,
    )(*args)


def _mm_acc_body(x_ref, w_ref, o_ref, acc_ref):
    kk = pl.program_id(2)

    @pl.when(kk == 0)
    def _():
        acc_ref[...] = jnp.zeros_like(acc_ref)

    acc_ref[...] += jnp.dot(x_ref[...], w_ref[...], preferred_element_type=F32)

    @pl.when(kk == pl.num_programs(2) - 1)
    def _():
        o_ref[...] = acc_ref[...].astype(o_ref.dtype)


def _matmul_ksplit(x, w, *, tm=1024, tn=1024, tk=2048, name):
    m, k = x.shape
    n = w.shape[1]
    blocks = tm * tk * 2 + tk * tn * 2 + tm * tn * 2
    return pl.pallas_call(
        _mm_acc_body,
        out_shape=jax.ShapeDtypeStruct((m, n), BF16),
        grid=(m // tm, n // tn, k // tk),
        in_specs=[pl.BlockSpec((tm, tk), lambda i, j, kk: (i, kk)),
                  pl.BlockSpec((tk, tn), lambda i, j, kk: (kk, j))],
        out_specs=pl.BlockSpec((tm, tn), lambda i, j, kk: (i, j)),
        scratch_shapes=[pltpu.VMEM((tm, tn), F32)],
        compiler_params=_params(("arbitrary", "arbitrary", "arbitrary"),
                                _vmem_limit(blocks, scratch_bytes=tm * tn * 4, temp_bytes=2 * tm * tn * 4)),
        name=name,
    )(x, w)


def _mix_body(h_ref, a_ref, b_ref, wga_ref, wgb_ref, wa_ref, wb_ref, o_ref):
    h = h_ref[...]
    ga = jax.nn.sigmoid(jnp.dot(h, wga_ref[...], preferred_element_type=F32))
    t = ga * jnp.dot(a_ref[...], wa_ref[...], preferred_element_type=F32)
    gb = jax.nn.sigmoid(jnp.dot(h, wgb_ref[...], preferred_element_type=F32))
    t = t + gb * jnp.dot(b_ref[...], wb_ref[...], preferred_element_type=F32)
    o_ref[...] = t.astype(o_ref.dtype)


def _gated_mix(h, a, b, w_gate, w_pa, w_pb, *, tm=512, tn=512):
    m, d = h.shape
    ka = a.shape[1]
    kb = b.shape[1]
    n = w_pa.shape[1]
    nb = n // tn
    blocks = tm * (d + ka + kb) * 2 + (2 * d + ka + kb) * tn * 2 + tm * tn * 2
    return pl.pallas_call(
        _mix_body,
        out_shape=jax.ShapeDtypeStruct((m, n), BF16),
        grid=(m // tm, nb),
        in_specs=[pl.BlockSpec((tm, d), lambda i, j: (i, 0)),
                  pl.BlockSpec((tm, ka), lambda i, j: (i, 0)),
                  pl.BlockSpec((tm, kb), lambda i, j: (i, 0)),
                  pl.BlockSpec((d, tn), lambda i, j: (0, j)),
                  pl.BlockSpec((d, tn), lambda i, j: (0, j + nb)),
                  pl.BlockSpec((ka, tn), lambda i, j: (0, j)),
                  pl.BlockSpec((kb, tn), lambda i, j: (0, j))],
        out_specs=pl.BlockSpec((tm, tn), lambda i, j: (i, j)),
        compiler_params=_params(("arbitrary", "arbitrary"), _vmem_limit(blocks, temp_bytes=4 * tm * tn * 4)),
        name="mix",
    )(h, a, b, w_gate, w_gate, w_pa, w_pb)


def _mla_body(q_ref, kn_ref, kp_ref, v_ref, o_ref, *, tq):
    i = pl.program_id(2)
    q = q_ref[...]
    scale = (MLA_NOPE + MLA_ROPE) ** -0.5
    row = lax.broadcasted_iota(jnp.int32, (tq, tq), 0)
    col = lax.broadcasted_iota(jnp.int32, (tq, tq), 1)
    diag_visible = (col // CHUNK) <= (row // CHUNK)

    def step(j, carry, masked):
        m, l, acc = carry
        start = pl.multiple_of(j * tq, tq)
        kf = jnp.concatenate([kn_ref[pl.ds(start, tq), :], kp_ref[pl.ds(start, tq), :]], axis=1)
        s = lax.dot_general(q, kf, (((1,), (1,)), ((), ())), preferred_element_type=F32) * scale
        if masked:
            s = jnp.where(diag_visible, s, NEG)
        m_new = jnp.maximum(m, jnp.max(s, axis=-1, keepdims=True))
        alpha = jnp.exp(m - m_new)
        p = jnp.exp(s - m_new)
        l = alpha * l + jnp.sum(p, axis=-1, keepdims=True)
        pv = jnp.dot(p.astype(BF16), v_ref[pl.ds(start, tq), :], preferred_element_type=F32)
        return m_new, l, alpha * acc + pv

    init = (jnp.full((tq, 1), NEG, F32), jnp.zeros((tq, 1), F32), jnp.zeros((tq, MLA_V), F32))
    carry = lax.fori_loop(0, i, lambda j, c: step(j, c, False), init)
    _, l, acc = step(i, carry, True)
    o_ref[...] = (acc / l).astype(o_ref.dtype)


def _mla_attention(q, kv, kpe, batch, seq, tq=128):
    rows = q.shape[0]
    nq = seq // tq
    blocks = tq * MLA_QK_PAD * 2 + 3 * seq * LANES * 2 + tq * MLA_V * 2
    return pl.pallas_call(
        functools.partial(_mla_body, tq=tq),
        out_shape=jax.ShapeDtypeStruct((rows, MLA_HEADS * MLA_V), BF16),
        grid=(batch, MLA_HEADS, nq),
        in_specs=[pl.BlockSpec((tq, MLA_QK_PAD), lambda b, h, i: (b * nq + i, h)),
                  pl.BlockSpec((seq, MLA_NOPE), lambda b, h, i: (b, h)),
                  pl.BlockSpec((seq, LANES), lambda b, h, i: (b, 0)),
                  pl.BlockSpec((seq, MLA_V), lambda b, h, i: (b, MLA_HEADS + h))],
        out_specs=pl.BlockSpec((tq, MLA_V), lambda b, h, i: (b * nq + i, h)),
        compiler_params=_params(("arbitrary", "arbitrary", "arbitrary"), _vmem_limit(blocks, temp_bytes=8 << 20)),
        name="mla",
    )(q, kv, kpe, kv)


def _t5_bucket_table():
    iq = np.arange(SWA_BLOCK)
    ik = np.arange(2 * SWA_BLOCK) - SWA_BLOCK
    rel = ik[None, :] - iq[:, None]
    nb = NUM_BUCKETS // 2
    max_exact = nb // 2
    ret = np.where(rel > 0, nb, 0)
    n = np.abs(rel)
    nf = np.maximum(n, 1).astype(np.float64)
    large = max_exact + (np.log(nf / max_exact) / math.log(MAX_DISTANCE / max_exact)
                         * (nb - max_exact)).astype(np.int32)
    large = np.minimum(large, nb - 1)
    return (ret + np.where(n < max_exact, n, large)).astype(np.int32)


def _bias_body(rb_ref, bucket_ref, o_ref):
    hd = pl.program_id(0)
    bk = bucket_ref[...]
    tab = jnp.zeros(bk.shape, F32)
    for b in range(NUM_BUCKETS):
        tab = jnp.where(bk == b, rb_ref[b, hd], tab)
    qc = lax.broadcasted_iota(jnp.int32, bk.shape, 0) // CHUNK
    kc = lax.broadcasted_iota(jnp.int32, bk.shape, 1) // CHUNK - SWA_BLOCK // CHUNK
    valid = jnp.logical_and(kc <= qc, kc >= qc - WINDOW // CHUNK)
    o_ref[0] = jnp.where(valid, tab, NEG)


def _bias_table(rel_bias):
    bucket = jnp.asarray(_t5_bucket_table())
    return pl.pallas_call(
        _bias_body,
        out_shape=jax.ShapeDtypeStruct((SWA_HEADS, SWA_BLOCK, 2 * SWA_BLOCK), F32),
        grid=(SWA_HEADS,),
        in_specs=[pl.BlockSpec(memory_space=pltpu.SMEM),
                  pl.BlockSpec((SWA_BLOCK, 2 * SWA_BLOCK), lambda h: (0, 0))],
        out_specs=pl.BlockSpec((1, SWA_BLOCK, 2 * SWA_BLOCK), lambda h: (h, 0, 0)),
        compiler_params=_params(("arbitrary",), 16 << 20),
        name="swa_bias",
    )(rel_bias, bucket)


def _swa_body(sink_ref, q_ref, kp_ref, kc_ref, vp_ref, vc_ref, bias_ref, o_ref):
    n = pl.program_id(1)
    scale = SWA_HEAD_DIM ** -0.5
    col = lax.broadcasted_iota(jnp.int32, (SWA_BLOCK, 2 * SWA_BLOCK), 1)
    no_prev = jnp.logical_and(n == 0, col < SWA_BLOCK)
    kcat = jnp.concatenate([kp_ref[...], kc_ref[...]], axis=0)
    vcat = jnp.concatenate([vp_ref[...], vc_ref[...]], axis=0)
    for g in range(SWA_KV_HEADS):
        kg = kcat[:, g * SWA_HEAD_DIM:(g + 1) * SWA_HEAD_DIM]
        vg = vcat[:, g * SWA_HEAD_DIM:(g + 1) * SWA_HEAD_DIM]
        for r in range(SWA_GROUP):
            hd = g * SWA_GROUP + r
            c0 = hd * SWA_HEAD_DIM
            qh = q_ref[:, c0:c0 + SWA_HEAD_DIM]
            s = lax.dot_general(qh, kg, (((1,), (1,)), ((), ())), preferred_element_type=F32)
            s = s * scale + bias_ref[hd]
            s = jnp.where(no_prev, NEG, s)
            sk = sink_ref[0, hd]
            m = jnp.maximum(jnp.max(s, axis=-1, keepdims=True), sk)
            e = jnp.exp(s - m)
            den = jnp.sum(e, axis=-1, keepdims=True) + jnp.exp(sk - m)
            o = jnp.dot(e.astype(BF16), vg, preferred_element_type=F32) / den
            o_ref[:, c0:c0 + SWA_HEAD_DIM] = o.astype(o_ref.dtype)


def _swa_attention(proj, sinks, bias_tab, batch, seq, qcol, kcol, vcol):
    rows = proj.shape[0]
    nb = seq // SWA_BLOCK
    blk = SWA_BLOCK
    blocks = (blk * SWA_Q_WIDTH * 2 + 4 * blk * SWA_KV_WIDTH * 2
              + SWA_HEADS * blk * 2 * blk * 4 + blk * SWA_Q_WIDTH * 2)
    cur = lambda b, n: b * nb + n
    prev = lambda b, n: b * nb + jnp.maximum(n - 1, 0)
    return pl.pallas_call(
        _swa_body,
        out_shape=jax.ShapeDtypeStruct((rows, SWA_Q_WIDTH), BF16),
        grid=(batch, nb),
        in_specs=[pl.BlockSpec(memory_space=pltpu.SMEM),
                  pl.BlockSpec((blk, SWA_Q_WIDTH), lambda b, n: (cur(b, n), qcol)),
                  pl.BlockSpec((blk, SWA_KV_WIDTH), lambda b, n: (prev(b, n), kcol)),
                  pl.BlockSpec((blk, SWA_KV_WIDTH), lambda b, n: (cur(b, n), kcol)),
                  pl.BlockSpec((blk, SWA_KV_WIDTH), lambda b, n: (prev(b, n), vcol)),
                  pl.BlockSpec((blk, SWA_KV_WIDTH), lambda b, n: (cur(b, n), vcol)),
                  pl.BlockSpec((SWA_HEADS, blk, 2 * blk), lambda b, n: (0, 0, 0))],
        out_specs=pl.BlockSpec((blk, SWA_Q_WIDTH), lambda b, n: (cur(b, n), 0)),
        compiler_params=_params(("arbitrary", "arbitrary"), _vmem_limit(blocks, temp_bytes=4 << 20)),
        name="swa",
    )(sinks, proj, proj, proj, proj, proj, bias_tab)


def _rope_tables(positions):
    half = MLA_ROPE // 2
    inv = ROPE_THETA ** (-jnp.arange(half, dtype=F32) * (2.0 / MLA_ROPE))
    ang = positions.astype(F32).reshape(-1, 1) * inv
    cos, sin = jnp.cos(ang), jnp.sin(ang)
    zero = jnp.zeros((ang.shape[0], LANES - MLA_ROPE), F32)
    return (jnp.concatenate([cos, cos, zero], axis=1), jnp.concatenate([-sin, sin, zero], axis=1))


def kernel(x, c, positions, w_ada, b_ada, pre_norm_g, post_norm_g, w_in, q_norm_g, kv_norm_g, w_uq, w_ukv,
           swa_sinks, rel_bias, w_gate, w_proj_a, w_proj_b, w_out, w_ff_up, w_ff_down):
    batch, seq, d = x.shape
    depth = w_ada.shape[0]
    rows = batch * seq
    x2 = x.reshape(rows, d)

    cos_t, sin_t = _rope_tables(positions)
    bias_tab = _bias_table(rel_bias.astype(F32))
    c_pad = jnp.concatenate([c, jnp.zeros((16 - batch, d), c.dtype)], axis=0)

    o_qlat, o_kvlat, o_rope = 0, Q_LORA, Q_LORA + KV_LORA
    o_qs = o_rope + MLA_ROPE
    o_ks = o_qs + SWA_Q_WIDTH
    o_vs = o_ks + SWA_KV_WIDTH

    for l in range(depth):
        wi = w_in[l]
        w_main = jnp.concatenate([wi[:, o_qs:o_ks], wi[:, o_qlat:o_kvlat], wi[:, o_kvlat:o_rope],
                                  wi[:, o_ks:o_vs], wi[:, o_vs:o_vs + SWA_KV_WIDTH]], axis=1).astype(BF16)
        w_rope = jnp.pad(wi[:, o_rope:o_qs], ((0, 0), (0, LANES - MLA_ROPE))).astype(BF16)
        w_q = jnp.pad(w_uq[l].reshape(Q_LORA, MLA_HEADS, MLA_NOPE + MLA_ROPE),
                      ((0, 0), (0, 0), (0, MLA_QK_PAD - MLA_NOPE - MLA_ROPE))
                      ).reshape(Q_LORA, MLA_HEADS * MLA_QK_PAD).astype(BF16)
        w_kv3 = w_ukv[l].reshape(KV_LORA, MLA_HEADS, MLA_NOPE + MLA_V)
        w_kv = jnp.concatenate([w_kv3[:, :, :MLA_NOPE].reshape(KV_LORA, -1),
                                w_kv3[:, :, MLA_NOPE:].reshape(KV_LORA, -1)], axis=1).astype(BF16)
        qcol_s, col_qlat, col_kvlat = 0, SWA_Q_WIDTH // Q_LORA, (SWA_Q_WIDTH + Q_LORA) // KV_LORA
        col_ks = (SWA_Q_WIDTH + Q_LORA + KV_LORA) // SWA_KV_WIDTH
        col_vs = col_ks + 1

        mod = _modulation(c_pad, w_ada[l], b_ada[l].reshape(1, -1))[:batch]
        mod3 = mod.reshape(batch, 6, d)

        h = _prenorm(x2, mod3, pre_norm_g[l, 0].reshape(1, d), seq)
        proj = _matmul(h, w_main, name="mm_in")
        kpe = _matmul_rope(h, w_rope, cos_t, sin_t)
        q = _matmul_norm(proj, col_qlat, Q_LORA, q_norm_g[l].reshape(1, -1), w_q,
                         rope_tables=(cos_t, sin_t), name="mm_uq")
        kv = _matmul_norm(proj, col_kvlat, KV_LORA, kv_norm_g[l].reshape(1, -1), w_kv, name="mm_ukv")
        att_a = _mla_attention(q, kv, kpe, batch, seq)
        att_b = _swa_attention(proj, swa_sinks[l].reshape(1, -1).astype(F32), bias_tab, batch, seq,
                               qcol_s, col_ks, col_vs)
        mix = _gated_mix(h, att_a, att_b, w_gate[l].astype(BF16), w_proj_a[l].astype(BF16),
                         w_proj_b[l].astype(BF16))
        y = _matmul(mix, w_out[l].astype(BF16), name="mm_out")
        x2, h2 = _post1(x2, y, mod3, post_norm_g[l, 0].reshape(1, d), pre_norm_g[l, 1].reshape(1, d), seq)

        u = _matmul(h2, w_ff_up[l].astype(BF16), act="relu2", name="mm_up")
        y2 = _matmul_ksplit(u, w_ff_down[l].astype(BF16), name="mm_down")
        x2 = _post2(x2, y2, mod3, post_norm_g[l, 1].reshape(1, d), seq)

    return x2.reshape(batch, seq, d)
```

```python
import functools
import math

import numpy as np
import jax
import jax.numpy as jnp
from jax import lax
from jax.experimental import pallas as pl
from jax.experimental.pallas import tpu as pltpu

F32 = jnp.float32
BF16 = jnp.bfloat16

D_MODEL = 4096
CHUNK = 64
MLA_HEADS = 16
MLA_NOPE = 128
MLA_ROPE = 64
MLA_V = 128
Q_LORA = 1024
KV_LORA = 512
ROPE_THETA = 10000.0
SWA_HEADS = 32
SWA_KV_HEADS = 4
SWA_HEAD_DIM = 64
SWA_GROUP = SWA_HEADS // SWA_KV_HEADS
WINDOW = 128
SWA_BLOCK = 128
NUM_BUCKETS = 32
MAX_DISTANCE = 128
EPS = 1e-6
NEG = -1e30

LANES = 128
MLA_QK_PAD = 256
SWA_Q_WIDTH = SWA_HEADS * SWA_HEAD_DIM
SWA_KV_WIDTH = SWA_KV_HEADS * SWA_HEAD_DIM
VMEM_CAP_BYTES = 56 * 1024 * 1024


def _vmem_limit(block_bytes, scratch_bytes=0, temp_bytes=0):
    need = 2 * block_bytes + scratch_bytes + temp_bytes + (4 << 20)
    return int(min(max(need, 16 << 20), VMEM_CAP_BYTES))


def _params(sem, vmem):
    return pltpu.CompilerParams(dimension_semantics=sem, vmem_limit_bytes=vmem)


def _rms(x):
    return x * lax.rsqrt(jnp.mean(x * x, axis=-1, keepdims=True) + EPS)


def _rope_half_swap(x, cos_t, sin_t):
    lane = lax.broadcasted_iota(jnp.int32, x.shape, 1)
    swapped = jnp.where(lane < MLA_ROPE // 2,
                        pltpu.roll(x, LANES - MLA_ROPE // 2, 1),
                        pltpu.roll(x, MLA_ROPE // 2, 1))
    return x * cos_t + swapped * sin_t


def _mod_body(c_ref, w_ref, b_ref, o_ref):
    c = c_ref[...]
    ca = (c * jax.nn.sigmoid(c)).astype(BF16)
    o_ref[...] = jnp.dot(ca, w_ref[...].astype(BF16), preferred_element_type=F32) + b_ref[...]


def _modulation(c_pad, w_ada, b_ada):
    m, k = c_pad.shape
    n = w_ada.shape[1]
    tn = 512
    blocks = m * k * 4 + k * tn * 4 + tn * 4 + m * tn * 4
    return pl.pallas_call(
        _mod_body,
        out_shape=jax.ShapeDtypeStruct((m, n), F32),
        grid=(n // tn,),
        in_specs=[pl.BlockSpec((m, k), lambda j: (0, 0)),
                  pl.BlockSpec((k, tn), lambda j: (0, j)),
                  pl.BlockSpec((1, tn), lambda j: (0, j))],
        out_specs=pl.BlockSpec((m, tn), lambda j: (0, j)),
        compiler_params=_params(("arbitrary",), _vmem_limit(blocks, temp_bytes=k * tn * 2)),
        name="mod",
    )(c_pad, w_ada, b_ada)


def _prenorm_body(x_ref, mod_ref, g_ref, o_ref):
    y = _rms(x_ref[...]) * g_ref[...]
    o_ref[...] = (y * (1.0 + mod_ref[1:2, :]) + mod_ref[0:1, :]).astype(o_ref.dtype)


def _prenorm(x2, mod3, g, seq, tr=256):
    rows, d = x2.shape
    per_b = seq // tr
    blocks = tr * d * 4 + 6 * d * 4 + d * 4 + tr * d * 2
    return pl.pallas_call(
        _prenorm_body,
        out_shape=jax.ShapeDtypeStruct((rows, d), BF16),
        grid=(rows // tr,),
        in_specs=[pl.BlockSpec((tr, d), lambda i: (i, 0)),
                  pl.BlockSpec((None, 6, d), lambda i: (i // per_b, 0, 0)),
                  pl.BlockSpec((1, d), lambda i: (0, 0))],
        out_specs=pl.BlockSpec((tr, d), lambda i: (i, 0)),
        compiler_params=_params(("arbitrary",), _vmem_limit(blocks, temp_bytes=3 * tr * d * 4)),
        name="prenorm",
    )(x2, mod3, g)


def _post1_body(x_ref, y_ref, mod_ref, pg_ref, g2_ref, x1_ref, h2_ref):
    yn = _rms(y_ref[...].astype(F32)) * pg_ref[...]
    x1 = x_ref[...] + mod_ref[2:3, :] * yn
    x1_ref[...] = x1
    h = _rms(x1) * g2_ref[...]
    h2_ref[...] = (h * (1.0 + mod_ref[4:5, :]) + mod_ref[3:4, :]).astype(h2_ref.dtype)


def _post1(x2, y, mod3, post_g, pre_g2, seq, tr=256):
    rows, d = x2.shape
    per_b = seq // tr
    blocks = tr * d * (4 + 2 + 4 + 2) + 8 * d * 4
    row = pl.BlockSpec((tr, d), lambda i: (i, 0))
    vec = pl.BlockSpec((1, d), lambda i: (0, 0))
    return pl.pallas_call(
        _post1_body,
        out_shape=(jax.ShapeDtypeStruct((rows, d), F32), jax.ShapeDtypeStruct((rows, d), BF16)),
        grid=(rows // tr,),
        in_specs=[row, row, pl.BlockSpec((None, 6, d), lambda i: (i // per_b, 0, 0)), vec, vec],
        out_specs=(row, row),
        compiler_params=_params(("arbitrary",), _vmem_limit(blocks, temp_bytes=4 * tr * d * 4)),
        name="post1",
    )(x2, y, mod3, post_g, pre_g2)


def _post2_body(x_ref, y_ref, mod_ref, pg_ref, o_ref):
    yn = _rms(y_ref[...].astype(F32)) * pg_ref[...]
    o_ref[...] = x_ref[...] + mod_ref[5:6, :] * yn


def _post2(x1, y, mod3, post_g, seq, tr=256):
    rows, d = x1.shape
    per_b = seq // tr
    blocks = tr * d * (4 + 2 + 4) + 7 * d * 4
    row = pl.BlockSpec((tr, d), lambda i: (i, 0))
    return pl.pallas_call(
        _post2_body,
        out_shape=jax.ShapeDtypeStruct((rows, d), F32),
        grid=(rows // tr,),
        in_specs=[row, row, pl.BlockSpec((None, 6, d), lambda i: (i // per_b, 0, 0)),
                  pl.BlockSpec((1, d), lambda i: (0, 0))],
        out_specs=row,
        compiler_params=_params(("arbitrary",), _vmem_limit(blocks, temp_bytes=3 * tr * d * 4)),
        name="post2",
    )(x1, y, mod3, post_g)


def _mm_body(x_ref, w_ref, o_ref, *, act):
    acc = jnp.dot(x_ref[...], w_ref[...], preferred_element_type=F32)
    if act == "relu2":
        r = jnp.maximum(acc, 0.0)
        acc = r * r
    o_ref[...] = acc.astype(o_ref.dtype)


def _matmul(x, w, *, act=None, tm=1024, tn=1024, name):
    m, k = x.shape
    n = w.shape[1]
    tn = min(tn, n)
    blocks = tm * k * 2 + k * tn * 2 + tm * tn * 2
    return pl.pallas_call(
        functools.partial(_mm_body, act=act),
        out_shape=jax.ShapeDtypeStruct((m, n), BF16),
        grid=(m // tm, n // tn),
        in_specs=[pl.BlockSpec((tm, k), lambda i, j: (i, 0)),
                  pl.BlockSpec((k, tn), lambda i, j: (0, j))],
        out_specs=pl.BlockSpec((tm, tn), lambda i, j: (i, j)),
        compiler_params=_params(("arbitrary", "arbitrary"), _vmem_limit(blocks, temp_bytes=2 * tm * tn * 4)),
        name=name,
    )(x, w)


def _mm_rope_body(x_ref, w_ref, cos_ref, sin_ref, o_ref):
    acc = jnp.dot(x_ref[...], w_ref[...], preferred_element_type=F32)
    o_ref[...] = _rope_half_swap(acc, cos_ref[...], sin_ref[...]).astype(o_ref.dtype)


def _matmul_rope(x, w, cos_t, sin_t, *, tm=1024):
    m, k = x.shape
    n = w.shape[1]
    blocks = tm * k * 2 + k * n * 2 + 2 * tm * n * 4 + tm * n * 2
    return pl.pallas_call(
        _mm_rope_body,
        out_shape=jax.ShapeDtypeStruct((m, n), BF16),
        grid=(m // tm,),
        in_specs=[pl.BlockSpec((tm, k), lambda i: (i, 0)),
                  pl.BlockSpec((k, n), lambda i: (0, 0)),
                  pl.BlockSpec((tm, n), lambda i: (i, 0)),
                  pl.BlockSpec((tm, n), lambda i: (i, 0))],
        out_specs=pl.BlockSpec((tm, n), lambda i: (i, 0)),
        compiler_params=_params(("arbitrary",), _vmem_limit(blocks, temp_bytes=4 * tm * n * 4)),
        name="mm_krope",
    )(x, w, cos_t, sin_t)


def _mm_norm_body(x_ref, g_ref, w_ref, *rest, rope):
    xn = (_rms(x_ref[...].astype(F32)) * g_ref[...]).astype(BF16)
    acc = jnp.dot(xn, w_ref[...], preferred_element_type=F32)
    if not rope:
        (o_ref,) = rest
        o_ref[...] = acc.astype(o_ref.dtype)
        return
    cos_ref, sin_ref, o_ref = rest
    cos_t = cos_ref[...]
    sin_t = sin_ref[...]
    for hd in range(acc.shape[1] // MLA_QK_PAD):
        c0 = hd * MLA_QK_PAD
        o_ref[:, c0:c0 + MLA_NOPE] = acc[:, c0:c0 + MLA_NOPE].astype(o_ref.dtype)
        pe = _rope_half_swap(acc[:, c0 + MLA_NOPE:c0 + MLA_QK_PAD], cos_t, sin_t)
        o_ref[:, c0 + MLA_NOPE:c0 + MLA_QK_PAD] = pe.astype(o_ref.dtype)


def _matmul_norm(x, xcol, kdim, g, w, *, rope_tables=None, tm=1024, tn=1024, name):
    m = x.shape[0]
    n = w.shape[1]
    blocks = tm * kdim * 2 + kdim * 4 + kdim * tn * 2 + tm * tn * 2
    in_specs = [pl.BlockSpec((tm, kdim), lambda i, j: (i, xcol)),
                pl.BlockSpec((1, kdim), lambda i, j: (0, 0)),
                pl.BlockSpec((kdim, tn), lambda i, j: (0, j))]
    args = [x, g, w]
    if rope_tables is not None:
        in_specs += [pl.BlockSpec((tm, LANES), lambda i, j: (i, 0))] * 2
        args += list(rope_tables)
        blocks += 2 * tm * LANES * 4
    return pl.pallas_call(
        functools.partial(_mm_norm_body, rope=rope_tables is not None),
        out_shape=jax.ShapeDtypeStruct((m, n), BF16),
        grid=(m // tm, n // tn),
        in_specs=in_specs,
        out_specs=pl.BlockSpec((tm, tn), lambda i, j: (i, j)),
        compiler_params=_params(("arbitrary", "arbitrary"),
                                _vmem_limit(blocks, temp_bytes=2 * tm * tn * 4 + 3 * tm * kdim * 4)),
        name=name,
    )(*args)


def _mm_acc_body(x_ref, w_ref, o_ref, acc_ref):
    kk = pl.program_id(2)

    @pl.when(kk == 0)
    def _():
        acc_ref[...] = jnp.zeros_like(acc_ref)

    acc_ref[...] += jnp.dot(x_ref[...], w_ref[...], preferred_element_type=F32)

    @pl.when(kk == pl.num_programs(2) - 1)
    def _():
        o_ref[...] = acc_ref[...].astype(o_ref.dtype)


def _matmul_ksplit(x, w, *, tm=1024, tn=1024, tk=2048, name):
    m, k = x.shape
    n = w.shape[1]
    blocks = tm * tk * 2 + tk * tn * 2 + tm * tn * 2
    return pl.pallas_call(
        _mm_acc_body,
        out_shape=jax.ShapeDtypeStruct((m, n), BF16),
        grid=(m // tm, n // tn, k // tk),
        in_specs=[pl.BlockSpec((tm, tk), lambda i, j, kk: (i, kk)),
                  pl.BlockSpec((tk, tn), lambda i, j, kk: (kk, j))],
        out_specs=pl.BlockSpec((tm, tn), lambda i, j, kk: (i, j)),
        scratch_shapes=[pltpu.VMEM((tm, tn), F32)],
        compiler_params=_params(("arbitrary", "arbitrary", "arbitrary"),
                                _vmem_limit(blocks, scratch_bytes=tm * tn * 4, temp_bytes=2 * tm * tn * 4)),
        name=name,
    )(x, w)


def _mix_body(h_ref, a_ref, b_ref, wga_ref, wgb_ref, wa_ref, wb_ref, o_ref):
    h = h_ref[...]
    ga = jax.nn.sigmoid(jnp.dot(h, wga_ref[...], preferred_element_type=F32))
    t = ga * jnp.dot(a_ref[...], wa_ref[...], preferred_element_type=F32)
    gb = jax.nn.sigmoid(jnp.dot(h, wgb_ref[...], preferred_element_type=F32))
    t = t + gb * jnp.dot(b_ref[...], wb_ref[...], preferred_element_type=F32)
    o_ref[...] = t.astype(o_ref.dtype)


def _gated_mix(h, a, b, w_gate, w_pa, w_pb, *, tm=512, tn=512):
    m, d = h.shape
    ka = a.shape[1]
    kb = b.shape[1]
    n = w_pa.shape[1]
    nb = n // tn
    blocks = tm * (d + ka + kb) * 2 + (2 * d + ka + kb) * tn * 2 + tm * tn * 2
    return pl.pallas_call(
        _mix_body,
        out_shape=jax.ShapeDtypeStruct((m, n), BF16),
        grid=(m // tm, nb),
        in_specs=[pl.BlockSpec((tm, d), lambda i, j: (i, 0)),
                  pl.BlockSpec((tm, ka), lambda i, j: (i, 0)),
                  pl.BlockSpec((tm, kb), lambda i, j: (i, 0)),
                  pl.BlockSpec((d, tn), lambda i, j: (0, j)),
                  pl.BlockSpec((d, tn), lambda i, j: (0, j + nb)),
                  pl.BlockSpec((ka, tn), lambda i, j: (0, j)),
                  pl.BlockSpec((kb, tn), lambda i, j: (0, j))],
        out_specs=pl.BlockSpec((tm, tn), lambda i, j: (i, j)),
        compiler_params=_params(("arbitrary", "arbitrary"), _vmem_limit(blocks, temp_bytes=4 * tm * tn * 4)),
        name="mix",
    )(h, a, b, w_gate, w_gate, w_pa, w_pb)


def _mla_body(q_ref, kn_ref, kp_ref, v_ref, o_ref, *, seq, tq, tk):
    c = (MLA_NOPE + MLA_ROPE) ** -0.5 * math.log2(math.e)
    ones = jnp.ones((tk, LANES), BF16)
    row = lax.broadcasted_iota(jnp.int32, (tq, tk), 0)
    col = lax.broadcasted_iota(jnp.int32, (tq, tk), 1)
    for qi in range(seq // tq):
        q = q_ref[qi * tq:(qi + 1) * tq, :]
        n_kb = -(-((qi + 1) * tq) // tk)
        m = None
        acc = None
        for kj in range(n_kb):
            ks = slice(kj * tk, (kj + 1) * tk)
            kf = jnp.concatenate([kn_ref[ks, :], kp_ref[ks, :]], axis=1)
            s = lax.dot_general(q, kf, (((1,), (1,)), ((), ())), preferred_element_type=F32)
            if (kj + 1) * tk > qi * tq + CHUNK:
                visible = (col + kj * tk) // CHUNK <= (row + qi * tq) // CHUNK
                s = jnp.where(visible, s, NEG)
            bm = jnp.max(s, axis=-1, keepdims=True)
            v_ext = jnp.concatenate([v_ref[ks, :], ones], axis=1)
            if m is None:
                m = bm
                p = jnp.exp2((s - m) * c)
                acc = jnp.dot(p.astype(BF16), v_ext, preferred_element_type=F32)
            else:
                m_new = jnp.maximum(m, bm)
                alpha = jnp.exp2((m - m_new) * c)
                p = jnp.exp2((s - m_new) * c)
                acc = alpha * acc + jnp.dot(p.astype(BF16), v_ext, preferred_element_type=F32)
                m = m_new
        o_ref[qi * tq:(qi + 1) * tq, :] = (acc[:, :MLA_V] / acc[:, MLA_V:]).astype(o_ref.dtype)


def _mla_attention(q, kv, kpe, batch, seq, tq=256, tk=256):
    rows = q.shape[0]
    blocks = seq * MLA_QK_PAD * 2 + 3 * seq * LANES * 2 + seq * MLA_V * 2
    return pl.pallas_call(
        functools.partial(_mla_body, seq=seq, tq=tq, tk=tk),
        out_shape=jax.ShapeDtypeStruct((rows, MLA_HEADS * MLA_V), BF16),
        grid=(batch, MLA_HEADS),
        in_specs=[pl.BlockSpec((seq, MLA_QK_PAD), lambda b, h: (b, h)),
                  pl.BlockSpec((seq, MLA_NOPE), lambda b, h: (b, h)),
                  pl.BlockSpec((seq, LANES), lambda b, h: (b, 0)),
                  pl.BlockSpec((seq, MLA_V), lambda b, h: (b, MLA_HEADS + h))],
        out_specs=pl.BlockSpec((seq, MLA_V), lambda b, h: (b, h)),
        compiler_params=_params(("arbitrary", "arbitrary"), _vmem_limit(blocks, temp_bytes=8 << 20)),
        name="mla",
    )(q, kv, kpe, kv)


def _t5_bucket_table():
    iq = np.arange(SWA_BLOCK)
    ik = np.arange(2 * SWA_BLOCK) - SWA_BLOCK
    rel = ik[None, :] - iq[:, None]
    nb = NUM_BUCKETS // 2
    max_exact = nb // 2
    ret = np.where(rel > 0, nb, 0)
    n = np.abs(rel)
    nf = np.maximum(n, 1).astype(np.float64)
    large = max_exact + (np.log(nf / max_exact) / math.log(MAX_DISTANCE / max_exact)
                         * (nb - max_exact)).astype(np.int32)
    large = np.minimum(large, nb - 1)
    return (ret + np.where(n < max_exact, n, large)).astype(np.int32)


def _bias_body(rb_ref, bucket_ref, o_ref):
    hd = pl.program_id(0)
    bk = bucket_ref[...]
    tab = jnp.zeros(bk.shape, F32)
    for b in range(NUM_BUCKETS):
        tab = jnp.where(bk == b, rb_ref[b, hd], tab)
    qc = lax.broadcasted_iota(jnp.int32, bk.shape, 0) // CHUNK
    kc = lax.broadcasted_iota(jnp.int32, bk.shape, 1) // CHUNK - SWA_BLOCK // CHUNK
    valid = jnp.logical_and(kc <= qc, kc >= qc - WINDOW // CHUNK)
    o_ref[0] = jnp.where(valid, tab, NEG)


def _bias_table(rel_bias):
    bucket = jnp.asarray(_t5_bucket_table())
    return pl.pallas_call(
        _bias_body,
        out_shape=jax.ShapeDtypeStruct((SWA_HEADS, SWA_BLOCK, 2 * SWA_BLOCK), F32),
        grid=(SWA_HEADS,),
        in_specs=[pl.BlockSpec(memory_space=pltpu.SMEM),
                  pl.BlockSpec((SWA_BLOCK, 2 * SWA_BLOCK), lambda h: (0, 0))],
        out_specs=pl.BlockSpec((1, SWA_BLOCK, 2 * SWA_BLOCK), lambda h: (h, 0, 0)),
        compiler_params=_params(("arbitrary",), 16 << 20),
        name="swa_bias",
    )(rel_bias, bucket)


def _swa_body(sink_ref, q_ref, kp_ref, kc_ref, vp_ref, vc_ref, bias_ref, o_ref):
    n = pl.program_id(1)
    scale = SWA_HEAD_DIM ** -0.5
    col = lax.broadcasted_iota(jnp.int32, (SWA_BLOCK, 2 * SWA_BLOCK), 1)
    no_prev = jnp.logical_and(n == 0, col < SWA_BLOCK)
    kcat = jnp.concatenate([kp_ref[...], kc_ref[...]], axis=0)
    vcat = jnp.concatenate([vp_ref[...], vc_ref[...]], axis=0)
    for g in range(SWA_KV_HEADS):
        kg = kcat[:, g * SWA_HEAD_DIM:(g + 1) * SWA_HEAD_DIM]
        vg = vcat[:, g * SWA_HEAD_DIM:(g + 1) * SWA_HEAD_DIM]
        for r in range(SWA_GROUP):
            hd = g * SWA_GROUP + r
            c0 = hd * SWA_HEAD_DIM
            qh = q_ref[:, c0:c0 + SWA_HEAD_DIM]
            s = lax.dot_general(qh, kg, (((1,), (1,)), ((), ())), preferred_element_type=F32)
            s = s * scale + bias_ref[hd]
            s = jnp.where(no_prev, NEG, s)
            sk = sink_ref[0, hd]
            m = jnp.maximum(jnp.max(s, axis=-1, keepdims=True), sk)
            e = jnp.exp(s - m)
            den = jnp.sum(e, axis=-1, keepdims=True) + jnp.exp(sk - m)
            o = jnp.dot(e.astype(BF16), vg, preferred_element_type=F32) / den
            o_ref[:, c0:c0 + SWA_HEAD_DIM] = o.astype(o_ref.dtype)


def _swa_attention(proj, sinks, bias_tab, batch, seq, qcol, kcol, vcol):
    rows = proj.shape[0]
    nb = seq // SWA_BLOCK
    blk = SWA_BLOCK
    blocks = (blk * SWA_Q_WIDTH * 2 + 4 * blk * SWA_KV_WIDTH * 2
              + SWA_HEADS * blk * 2 * blk * 4 + blk * SWA_Q_WIDTH * 2)
    cur = lambda b, n: b * nb + n
    prev = lambda b, n: b * nb + jnp.maximum(n - 1, 0)
    return pl.pallas_call(
        _swa_body,
        out_shape=jax.ShapeDtypeStruct((rows, SWA_Q_WIDTH), BF16),
        grid=(batch, nb),
        in_specs=[pl.BlockSpec(memory_space=pltpu.SMEM),
                  pl.BlockSpec((blk, SWA_Q_WIDTH), lambda b, n: (cur(b, n), qcol)),
                  pl.BlockSpec((blk, SWA_KV_WIDTH), lambda b, n: (prev(b, n), kcol)),
                  pl.BlockSpec((blk, SWA_KV_WIDTH), lambda b, n: (cur(b, n), kcol)),
                  pl.BlockSpec((blk, SWA_KV_WIDTH), lambda b, n: (prev(b, n), vcol)),
                  pl.BlockSpec((blk, SWA_KV_WIDTH), lambda b, n: (cur(b, n), vcol)),
                  pl.BlockSpec((SWA_HEADS, blk, 2 * blk), lambda b, n: (0, 0, 0))],
        out_specs=pl.BlockSpec((blk, SWA_Q_WIDTH), lambda b, n: (cur(b, n), 0)),
        compiler_params=_params(("arbitrary", "arbitrary"), _vmem_limit(blocks, temp_bytes=4 << 20)),
        name="swa",
    )(sinks, proj, proj, proj, proj, proj, bias_tab)


def _rope_tables(positions):
    half = MLA_ROPE // 2
    inv = ROPE_THETA ** (-jnp.arange(half, dtype=F32) * (2.0 / MLA_ROPE))
    ang = positions.astype(F32).reshape(-1, 1) * inv
    cos, sin = jnp.cos(ang), jnp.sin(ang)
    zero = jnp.zeros((ang.shape[0], LANES - MLA_ROPE), F32)
    return (jnp.concatenate([cos, cos, zero], axis=1), jnp.concatenate([-sin, sin, zero], axis=1))


def kernel(x, c, positions, w_ada, b_ada, pre_norm_g, post_norm_g, w_in, q_norm_g, kv_norm_g, w_uq, w_ukv,
           swa_sinks, rel_bias, w_gate, w_proj_a, w_proj_b, w_out, w_ff_up, w_ff_down):
    batch, seq, d = x.shape
    depth = w_ada.shape[0]
    rows = batch * seq
    x2 = x.reshape(rows, d)

    cos_t, sin_t = _rope_tables(positions)
    bias_tab = _bias_table(rel_bias.astype(F32))
    c_pad = jnp.concatenate([c, jnp.zeros((16 - batch, d), c.dtype)], axis=0)

    o_qlat, o_kvlat, o_rope = 0, Q_LORA, Q_LORA + KV_LORA
    o_qs = o_rope + MLA_ROPE
    o_ks = o_qs + SWA_Q_WIDTH
    o_vs = o_ks + SWA_KV_WIDTH

    for l in range(depth):
        wi = w_in[l]
        w_main = jnp.concatenate([wi[:, o_qs:o_ks], wi[:, o_qlat:o_kvlat], wi[:, o_kvlat:o_rope],
                                  wi[:, o_ks:o_vs], wi[:, o_vs:o_vs + SWA_KV_WIDTH]], axis=1).astype(BF16)
        w_rope = jnp.pad(wi[:, o_rope:o_qs], ((0, 0), (0, LANES - MLA_ROPE))).astype(BF16)
        w_q = jnp.pad(w_uq[l].reshape(Q_LORA, MLA_HEADS, MLA_NOPE + MLA_ROPE),
                      ((0, 0), (0, 0), (0, MLA_QK_PAD - MLA_NOPE - MLA_ROPE))
                      ).reshape(Q_LORA, MLA_HEADS * MLA_QK_PAD).astype(BF16)
        w_kv3 = w_ukv[l].reshape(KV_LORA, MLA_HEADS, MLA_NOPE + MLA_V)
        w_kv = jnp.concatenate([w_kv3[:, :, :MLA_NOPE].reshape(KV_LORA, -1),
                                w_kv3[:, :, MLA_NOPE:].reshape(KV_LORA, -1)], axis=1).astype(BF16)
        qcol_s, col_qlat, col_kvlat = 0, SWA_Q_WIDTH // Q_LORA, (SWA_Q_WIDTH + Q_LORA) // KV_LORA
        col_ks = (SWA_Q_WIDTH + Q_LORA + KV_LORA) // SWA_KV_WIDTH
        col_vs = col_ks + 1

        mod = _modulation(c_pad, w_ada[l], b_ada[l].reshape(1, -1))[:batch]
        mod3 = mod.reshape(batch, 6, d)

        h = _prenorm(x2, mod3, pre_norm_g[l, 0].reshape(1, d), seq)
        proj = _matmul(h, w_main, name="mm_in")
        kpe = _matmul_rope(h, w_rope, cos_t, sin_t)
        q = _matmul_norm(proj, col_qlat, Q_LORA, q_norm_g[l].reshape(1, -1), w_q,
                         rope_tables=(cos_t, sin_t), name="mm_uq")
        kv = _matmul_norm(proj, col_kvlat, KV_LORA, kv_norm_g[l].reshape(1, -1), w_kv, name="mm_ukv")
        att_a = _mla_attention(q, kv, kpe, batch, seq)
        att_b = _swa_attention(proj, swa_sinks[l].reshape(1, -1).astype(F32), bias_tab, batch, seq,
                               qcol_s, col_ks, col_vs)
        mix = _gated_mix(h, att_a, att_b, w_gate[l].astype(BF16), w_proj_a[l].astype(BF16),
                         w_proj_b[l].astype(BF16))
        y = _matmul(mix, w_out[l].astype(BF16), name="mm_out")
        x2, h2 = _post1(x2, y, mod3, post_norm_g[l, 0].reshape(1, d), pre_norm_g[l, 1].reshape(1, d), seq)

        u = _matmul(h2, w_ff_up[l].astype(BF16), act="relu2", name="mm_up")
        y2 = _matmul_ksplit(u, w_ff_down[l].astype(BF16), name="mm_down")
        x2 = _post2(x2, y2, mod3, post_norm_g[l, 1].reshape(1, d), seq)

    return x2.reshape(batch, seq, d)
```

```python
import functools
import math

import numpy as np
import jax
import jax.numpy as jnp
from jax import lax
from jax.experimental import pallas as pl
from jax.experimental.pallas import tpu as pltpu

F32 = jnp.float32
BF16 = jnp.bfloat16

D_MODEL = 4096
CHUNK = 64
MLA_HEADS = 16
MLA_NOPE = 128
MLA_ROPE = 64
MLA_V = 128
Q_LORA = 1024
KV_LORA = 512
ROPE_THETA = 10000.0
SWA_HEADS = 32
SWA_KV_HEADS = 4
SWA_HEAD_DIM = 64
SWA_GROUP = SWA_HEADS // SWA_KV_HEADS
WINDOW = 128
SWA_BLOCK = 128
NUM_BUCKETS = 32
MAX_DISTANCE = 128
EPS = 1e-6
NEG = -1e30
LOG2E = math.log2(math.e)

LANES = 128
MLA_QK_PAD = 256
SWA_Q_WIDTH = SWA_HEADS * SWA_HEAD_DIM
SWA_KV_WIDTH = SWA_KV_HEADS * SWA_HEAD_DIM
VMEM_CAP_BYTES = 56 * 1024 * 1024


def _vmem_limit(block_bytes, scratch_bytes=0, temp_bytes=0):
    need = 2 * block_bytes + scratch_bytes + temp_bytes + (4 << 20)
    return int(min(max(need, 16 << 20), VMEM_CAP_BYTES))


def _params(sem, vmem):
    return pltpu.CompilerParams(dimension_semantics=sem, vmem_limit_bytes=vmem)


def _rms(x):
    return x * lax.rsqrt(jnp.mean(x * x, axis=-1, keepdims=True) + EPS)


def _rope_half_swap(x, cos_t, sin_t):
    lane = lax.broadcasted_iota(jnp.int32, x.shape, 1)
    swapped = jnp.where(lane < MLA_ROPE // 2,
                        pltpu.roll(x, LANES - MLA_ROPE // 2, 1),
                        pltpu.roll(x, MLA_ROPE // 2, 1))
    return x * cos_t + swapped * sin_t


def _with_riders(body, n_in, n_out, n_riders):
    if n_riders == 0:
        return body

    def wrapped(*refs):
        ins = refs[:n_in]
        rider_ins = refs[n_in:n_in + n_riders]
        outs = refs[n_in + n_riders:n_in + n_riders + n_out]
        rider_outs = refs[n_in + n_riders + n_out:n_in + 2 * n_riders + n_out]
        scratch = refs[n_in + 2 * n_riders + n_out:]
        for src, dst in zip(rider_ins, rider_outs):
            dst[...] = src[...].astype(dst.dtype)
        body(*ins, *outs, *scratch)

    return wrapped


def _rider_specs(riders, n_steps, step_of):
    in_specs, out_specs, out_shapes, block_bytes = [], [], [], 0
    for w3, layer in riders:
        _, r, cols = w3.shape
        tr = r // n_steps
        assert tr * n_steps == r and tr % 16 == 0, (w3.shape, n_steps)
        in_specs.append(pl.BlockSpec((None, tr, cols), lambda *g, layer=layer: (layer, step_of(*g), 0)))
        out_specs.append(pl.BlockSpec((tr, cols), lambda *g: (step_of(*g), 0)))
        out_shapes.append(jax.ShapeDtypeStruct((r, cols), BF16))
        block_bytes += tr * cols * (4 + 2)
    return in_specs, out_specs, out_shapes, block_bytes


def _mod_body(c_ref, w_ref, b_ref, o_ref):
    c = c_ref[...]
    ca = (c * jax.nn.sigmoid(c)).astype(BF16)
    o_ref[...] = jnp.dot(ca, w_ref[...].astype(BF16), preferred_element_type=F32) + b_ref[...]


def _modulation(c_pad, w_ada, b_ada):
    m, k = c_pad.shape
    n = w_ada.shape[1]
    tn = 512
    blocks = m * k * 4 + k * tn * 4 + tn * 4 + m * tn * 4
    return pl.pallas_call(
        _mod_body,
        out_shape=jax.ShapeDtypeStruct((m, n), F32),
        grid=(n // tn,),
        in_specs=[pl.BlockSpec((m, k), lambda j: (0, 0)),
                  pl.BlockSpec((k, tn), lambda j: (0, j)),
                  pl.BlockSpec((1, tn), lambda j: (0, j))],
        out_specs=pl.BlockSpec((m, tn), lambda j: (0, j)),
        compiler_params=_params(("arbitrary",), _vmem_limit(blocks, temp_bytes=k * tn * 2)),
        name="mod",
    )(c_pad, w_ada, b_ada)


def _prenorm_body(x_ref, mod_ref, g_ref, o_ref):
    y = _rms(x_ref[...]) * g_ref[...]
    o_ref[...] = (y * (1.0 + mod_ref[1:2, :]) + mod_ref[0:1, :]).astype(o_ref.dtype)


def _prenorm(x2, mod3, g, seq, tr=256):
    rows, d = x2.shape
    per_b = seq // tr
    blocks = tr * d * 4 + 6 * d * 4 + d * 4 + tr * d * 2
    return pl.pallas_call(
        _prenorm_body,
        out_shape=jax.ShapeDtypeStruct((rows, d), BF16),
        grid=(rows // tr,),
        in_specs=[pl.BlockSpec((tr, d), lambda i: (i, 0)),
                  pl.BlockSpec((None, 6, d), lambda i: (i // per_b, 0, 0)),
                  pl.BlockSpec((1, d), lambda i: (0, 0))],
        out_specs=pl.BlockSpec((tr, d), lambda i: (i, 0)),
        compiler_params=_params(("arbitrary",), _vmem_limit(blocks, temp_bytes=3 * tr * d * 4)),
        name="prenorm",
    )(x2, mod3, g)


def _post1_body(x_ref, y_ref, mod_ref, pg_ref, g2_ref, x1_ref, h2_ref):
    yn = _rms(y_ref[...].astype(F32)) * pg_ref[...]
    x1 = x_ref[...] + mod_ref[2:3, :] * yn
    x1_ref[...] = x1
    h = _rms(x1) * g2_ref[...]
    h2_ref[...] = (h * (1.0 + mod_ref[4:5, :]) + mod_ref[3:4, :]).astype(h2_ref.dtype)


def _post1(x2, y, mod3, post_g, pre_g2, seq, tr=256):
    rows, d = x2.shape
    per_b = seq // tr
    blocks = tr * d * (4 + 2 + 4 + 2) + 8 * d * 4
    row = pl.BlockSpec((tr, d), lambda i: (i, 0))
    vec = pl.BlockSpec((1, d), lambda i: (0, 0))
    return pl.pallas_call(
        _post1_body,
        out_shape=(jax.ShapeDtypeStruct((rows, d), F32), jax.ShapeDtypeStruct((rows, d), BF16)),
        grid=(rows // tr,),
        in_specs=[row, row, pl.BlockSpec((None, 6, d), lambda i: (i // per_b, 0, 0)), vec, vec],
        out_specs=(row, row),
        compiler_params=_params(("arbitrary",), _vmem_limit(blocks, temp_bytes=4 * tr * d * 4)),
        name="post1",
    )(x2, y, mod3, post_g, pre_g2)


def _post2_body(x_ref, y_ref, mod_ref, pg_ref, o_ref):
    yn = _rms(y_ref[...].astype(F32)) * pg_ref[...]
    o_ref[...] = x_ref[...] + mod_ref[5:6, :] * yn


def _post2(x1, y, mod3, post_g, seq, tr=256):
    rows, d = x1.shape
    per_b = seq // tr
    blocks = tr * d * (4 + 2 + 4) + 7 * d * 4
    row = pl.BlockSpec((tr, d), lambda i: (i, 0))
    return pl.pallas_call(
        _post2_body,
        out_shape=jax.ShapeDtypeStruct((rows, d), F32),
        grid=(rows // tr,),
        in_specs=[row, row, pl.BlockSpec((None, 6, d), lambda i: (i // per_b, 0, 0)),
                  pl.BlockSpec((1, d), lambda i: (0, 0))],
        out_specs=row,
        compiler_params=_params(("arbitrary",), _vmem_limit(blocks, temp_bytes=3 * tr * d * 4)),
        name="post2",
    )(x1, y, mod3, post_g)


def _mm_body(x_ref, w_ref, o_ref, *, act):
    acc = jnp.dot(x_ref[...], w_ref[...], preferred_element_type=F32)
    if act == "relu2":
        r = jnp.maximum(acc, 0.0)
        acc = r * r
    o_ref[...] = acc.astype(o_ref.dtype)


def _matmul(x, w, *, act=None, tm=1024, tn=1024, riders=(), name):
    m, k = x.shape
    n = w.shape[1]
    tn = min(tn, n)
    gm, gn = m // tm, n // tn
    r_in, r_out, r_shapes, r_bytes = _rider_specs(riders, gm * gn, lambda i, j: i * gn + j)
    blocks = tm * k * 2 + k * tn * 2 + tm * tn * 2 + r_bytes
    return pl.pallas_call(
        _with_riders(functools.partial(_mm_body, act=act), 2, 1, len(riders)),
        out_shape=[jax.ShapeDtypeStruct((m, n), BF16)] + r_shapes,
        grid=(gm, gn),
        in_specs=[pl.BlockSpec((tm, k), lambda i, j: (i, 0)),
                  pl.BlockSpec((k, tn), lambda i, j: (0, j))] + r_in,
        out_specs=[pl.BlockSpec((tm, tn), lambda i, j: (i, j))] + r_out,
        compiler_params=_params(("arbitrary", "arbitrary"), _vmem_limit(blocks, temp_bytes=2 * tm * tn * 4)),
        name=name,
    )(x, w, *[w3 for w3, _ in riders])


def _mm_rope_body(x_ref, w_ref, cos_ref, sin_ref, o_ref):
    acc = jnp.dot(x_ref[...], w_ref[...], preferred_element_type=F32)
    o_ref[...] = _rope_half_swap(acc, cos_ref[...], sin_ref[...]).astype(o_ref.dtype)


def _matmul_rope(x, w, cos_t, sin_t, *, tm=1024):
    m, k = x.shape
    n = w.shape[1]
    blocks = tm * k * 2 + k * n * 2 + 2 * tm * n * 4 + tm * n * 2
    return pl.pallas_call(
        _mm_rope_body,
        out_shape=jax.ShapeDtypeStruct((m, n), BF16),
        grid=(m // tm,),
        in_specs=[pl.BlockSpec((tm, k), lambda i: (i, 0)),
                  pl.BlockSpec((k, n), lambda i: (0, 0)),
                  pl.BlockSpec((tm, n), lambda i: (i, 0)),
                  pl.BlockSpec((tm, n), lambda i: (i, 0))],
        out_specs=pl.BlockSpec((tm, n), lambda i: (i, 0)),
        compiler_params=_params(("arbitrary",), _vmem_limit(blocks, temp_bytes=4 * tm * n * 4)),
        name="mm_krope",
    )(x, w, cos_t, sin_t)


def _mm_norm_body(x_ref, g_ref, w_ref, *rest, rope):
    xn_ref = rest[-1]

    @pl.when(pl.program_id(1) == 0)
    def _():
        xn_ref[...] = (_rms(x_ref[...].astype(F32)) * g_ref[...]).astype(BF16)

    acc = jnp.dot(xn_ref[...], w_ref[...], preferred_element_type=F32)
    if not rope:
        o_ref = rest[0]
        o_ref[...] = acc.astype(o_ref.dtype)
        return
    cos_ref, sin_ref, o_ref = rest[:3]
    cos_t = cos_ref[...]
    sin_t = sin_ref[...]
    for hd in range(acc.shape[1] // MLA_QK_PAD):
        c0 = hd * MLA_QK_PAD
        o_ref[:, c0:c0 + MLA_NOPE] = acc[:, c0:c0 + MLA_NOPE].astype(o_ref.dtype)
        pe = _rope_half_swap(acc[:, c0 + MLA_NOPE:c0 + MLA_QK_PAD], cos_t, sin_t)
        o_ref[:, c0 + MLA_NOPE:c0 + MLA_QK_PAD] = pe.astype(o_ref.dtype)


def _matmul_norm(x, xcol, kdim, g, w, *, rope_tables=None, tm=1024, tn=1024, name):
    m = x.shape[0]
    n = w.shape[1]
    blocks = tm * kdim * 2 + kdim * 4 + kdim * tn * 2 + tm * tn * 2
    in_specs = [pl.BlockSpec((tm, kdim), lambda i, j: (i, xcol)),
                pl.BlockSpec((1, kdim), lambda i, j: (0, 0)),
                pl.BlockSpec((kdim, tn), lambda i, j: (0, j))]
    args = [x, g, w]
    if rope_tables is not None:
        in_specs += [pl.BlockSpec((tm, LANES), lambda i, j: (i, 0))] * 2
        args += list(rope_tables)
        blocks += 2 * tm * LANES * 4
    return pl.pallas_call(
        functools.partial(_mm_norm_body, rope=rope_tables is not None),
        out_shape=jax.ShapeDtypeStruct((m, n), BF16),
        grid=(m // tm, n // tn),
        in_specs=in_specs,
        out_specs=pl.BlockSpec((tm, tn), lambda i, j: (i, j)),
        scratch_shapes=[pltpu.VMEM((tm, kdim), BF16)],
        compiler_params=_params(("arbitrary", "arbitrary"),
                                _vmem_limit(blocks, scratch_bytes=tm * kdim * 2,
                                            temp_bytes=2 * tm * tn * 4 + 3 * tm * kdim * 4)),
        name=name,
    )(*args)


def _mm_acc_body(x_ref, w_ref, o_ref, acc_ref):
    kk = pl.program_id(2)

    @pl.when(kk == 0)
    def _():
        acc_ref[...] = jnp.zeros_like(acc_ref)

    acc_ref[...] += jnp.dot(x_ref[...], w_ref[...], preferred_element_type=F32)

    @pl.when(kk == pl.num_programs(2) - 1)
    def _():
        o_ref[...] = acc_ref[...].astype(o_ref.dtype)


def _matmul_ksplit(x, w, *, tm=1024, tn=1024, tk=2048, name):
    m, k = x.shape
    n = w.shape[1]
    blocks = tm * tk * 2 + tk * tn * 2 + tm * tn * 2
    return pl.pallas_call(
        _mm_acc_body,
        out_shape=jax.ShapeDtypeStruct((m, n), BF16),
        grid=(m // tm, n // tn, k // tk),
        in_specs=[pl.BlockSpec((tm, tk), lambda i, j, kk: (i, kk)),
                  pl.BlockSpec((tk, tn), lambda i, j, kk: (kk, j))],
        out_specs=pl.BlockSpec((tm, tn), lambda i, j, kk: (i, j)),
        scratch_shapes=[pltpu.VMEM((tm, tn), F32)],
        compiler_params=_params(("arbitrary", "arbitrary", "arbitrary"),
                                _vmem_limit(blocks, scratch_bytes=tm * tn * 4, temp_bytes=2 * tm * tn * 4)),
        name=name,
    )(x, w)


def _mix_body(h_ref, a_ref, b_ref, wga_ref, wgb_ref, wa_ref, wb_ref, o_ref):
    h = h_ref[...]
    ga = jax.nn.sigmoid(jnp.dot(h, wga_ref[...], preferred_element_type=F32))
    t = ga * jnp.dot(a_ref[...], wa_ref[...], preferred_element_type=F32)
    gb = jax.nn.sigmoid(jnp.dot(h, wgb_ref[...], preferred_element_type=F32))
    t = t + gb * jnp.dot(b_ref[...], wb_ref[...], preferred_element_type=F32)
    o_ref[...] = t.astype(o_ref.dtype)


def _gated_mix(h, a, b, w_gate, w_pa, w_pb, *, tm=512, tn=512, riders=()):
    m, d = h.shape
    ka = a.shape[1]
    kb = b.shape[1]
    n = w_pa.shape[1]
    gm, nb = m // tm, n // tn
    r_in, r_out, r_shapes, r_bytes = _rider_specs(riders, gm * nb, lambda i, j: i * nb + j)
    blocks = tm * (d + ka + kb) * 2 + (2 * d + ka + kb) * tn * 2 + tm * tn * 2 + r_bytes
    return pl.pallas_call(
        _with_riders(_mix_body, 7, 1, len(riders)),
        out_shape=[jax.ShapeDtypeStruct((m, n), BF16)] + r_shapes,
        grid=(gm, nb),
        in_specs=[pl.BlockSpec((tm, d), lambda i, j: (i, 0)),
                  pl.BlockSpec((tm, ka), lambda i, j: (i, 0)),
                  pl.BlockSpec((tm, kb), lambda i, j: (i, 0)),
                  pl.BlockSpec((d, tn), lambda i, j: (0, j)),
                  pl.BlockSpec((d, tn), lambda i, j: (0, j + nb)),
                  pl.BlockSpec((ka, tn), lambda i, j: (0, j)),
                  pl.BlockSpec((kb, tn), lambda i, j: (0, j))] + r_in,
        out_specs=[pl.BlockSpec((tm, tn), lambda i, j: (i, j))] + r_out,
        compiler_params=_params(("arbitrary", "arbitrary"), _vmem_limit(blocks, temp_bytes=4 * tm * tn * 4)),
        name="mix",
    )(h, a, b, w_gate, w_gate, w_pa, w_pb, *[w3 for w3, _ in riders])


def _mla_body(q_ref, kn_ref, kp_ref, v_ref, o_ref, *, seq, tq, tk):
    c = (MLA_NOPE + MLA_ROPE) ** -0.5 * math.log2(math.e)
    ones = jnp.ones((tk, LANES), BF16)
    row = lax.broadcasted_iota(jnp.int32, (tq, tk), 0)
    col = lax.broadcasted_iota(jnp.int32, (tq, tk), 1)
    for qi in range(seq // tq):
        q = q_ref[qi * tq:(qi + 1) * tq, :]
        n_kb = -(-((qi + 1) * tq) // tk)
        m = None
        acc = None
        for kj in range(n_kb):
            ks = slice(kj * tk, (kj + 1) * tk)
            kf = jnp.concatenate([kn_ref[ks, :], kp_ref[ks, :]], axis=1)
            s = lax.dot_general(q, kf, (((1,), (1,)), ((), ())), preferred_element_type=F32)
            if (kj + 1) * tk > qi * tq + CHUNK:
                visible = (col + kj * tk) // CHUNK <= (row + qi * tq) // CHUNK
                s = jnp.where(visible, s, NEG)
            bm = jnp.max(s, axis=-1, keepdims=True)
            v_ext = jnp.concatenate([v_ref[ks, :], ones], axis=1)
            if m is None:
                m = bm
                p = jnp.exp2((s - m) * c)
                acc = jnp.dot(p.astype(BF16), v_ext, preferred_element_type=F32)
            else:
                m_new = jnp.maximum(m, bm)
                alpha = jnp.exp2((m - m_new) * c)
                p = jnp.exp2((s - m_new) * c)
                acc = alpha * acc + jnp.dot(p.astype(BF16), v_ext, preferred_element_type=F32)
                m = m_new
        o_ref[qi * tq:(qi + 1) * tq, :] = (acc[:, :MLA_V] / acc[:, MLA_V:]).astype(o_ref.dtype)


def _mla_attention(q, kv, kpe, batch, seq, tq=256, tk=256, riders=()):
    rows = q.shape[0]
    r_in, r_out, r_shapes, r_bytes = _rider_specs(riders, batch * MLA_HEADS, lambda b, h: b * MLA_HEADS + h)
    blocks = seq * MLA_QK_PAD * 2 + 3 * seq * LANES * 2 + seq * MLA_V * 2 + r_bytes
    return pl.pallas_call(
        _with_riders(functools.partial(_mla_body, seq=seq, tq=tq, tk=tk), 4, 1, len(riders)),
        out_shape=[jax.ShapeDtypeStruct((rows, MLA_HEADS * MLA_V), BF16)] + r_shapes,
        grid=(batch, MLA_HEADS),
        in_specs=[pl.BlockSpec((seq, MLA_QK_PAD), lambda b, h: (b, h)),
                  pl.BlockSpec((seq, MLA_NOPE), lambda b, h: (b, h)),
                  pl.BlockSpec((seq, LANES), lambda b, h: (b, 0)),
                  pl.BlockSpec((seq, MLA_V), lambda b, h: (b, MLA_HEADS + h))] + r_in,
        out_specs=[pl.BlockSpec((seq, MLA_V), lambda b, h: (b, h))] + r_out,
        compiler_params=_params(("arbitrary", "arbitrary"), _vmem_limit(blocks, temp_bytes=8 << 20)),
        name="mla",
    )(q, kv, kpe, kv, *[w3 for w3, _ in riders])


def _t5_bucket_table():
    iq = np.arange(SWA_BLOCK)
    ik = np.arange(2 * SWA_BLOCK) - SWA_BLOCK
    rel = ik[None, :] - iq[:, None]
    nb = NUM_BUCKETS // 2
    max_exact = nb // 2
    ret = np.where(rel > 0, nb, 0)
    n = np.abs(rel)
    nf = np.maximum(n, 1).astype(np.float64)
    large = max_exact + (np.log(nf / max_exact) / math.log(MAX_DISTANCE / max_exact)
                         * (nb - max_exact)).astype(np.int32)
    large = np.minimum(large, nb - 1)
    return (ret + np.where(n < max_exact, n, large)).astype(np.int32)


def _bias_body(rb_ref, bucket_ref, o_ref):
    first = pl.program_id(0) == 0
    hd = pl.program_id(1)
    bk = bucket_ref[...]
    tab = jnp.zeros(bk.shape, F32)
    for b in range(NUM_BUCKETS):
        tab = jnp.where(bk == b, rb_ref[b, hd] * LOG2E, tab)
    col = lax.broadcasted_iota(jnp.int32, bk.shape, 1)
    qc = lax.broadcasted_iota(jnp.int32, bk.shape, 0) // CHUNK
    kc = col // CHUNK - SWA_BLOCK // CHUNK
    valid = jnp.logical_and(kc <= qc, kc >= qc - WINDOW // CHUNK)
    valid = jnp.logical_and(valid, jnp.logical_not(jnp.logical_and(first, col < SWA_BLOCK)))
    o_ref[...] = jnp.where(valid, tab, NEG)


def _bias_table(rel_bias):
    bucket = jnp.asarray(_t5_bucket_table())
    return pl.pallas_call(
        _bias_body,
        out_shape=jax.ShapeDtypeStruct((2, SWA_HEADS, SWA_BLOCK, 2 * SWA_BLOCK), F32),
        grid=(2, SWA_HEADS),
        in_specs=[pl.BlockSpec(memory_space=pltpu.SMEM),
                  pl.BlockSpec((SWA_BLOCK, 2 * SWA_BLOCK), lambda f, h: (0, 0))],
        out_specs=pl.BlockSpec((None, None, SWA_BLOCK, 2 * SWA_BLOCK), lambda f, h: (f, h, 0, 0)),
        compiler_params=_params(("arbitrary", "arbitrary"), 16 << 20),
        name="swa_bias",
    )(rel_bias, bucket)


def _swa_body(sink_ref, q_ref, kp_ref, kc_ref, vp_ref, vc_ref, bias_ref, o_ref):
    c = SWA_HEAD_DIM ** -0.5 * LOG2E
    blk = SWA_BLOCK
    low = lax.broadcasted_iota(jnp.int32, (2 * blk, LANES), 1) < SWA_HEAD_DIM
    low_q = lax.broadcasted_iota(jnp.int32, (blk, LANES), 1) < SWA_HEAD_DIM

    def both_halves(pair, want_low):
        other = pltpu.roll(pair, SWA_HEAD_DIM, 1)
        return jnp.where(low, pair, other) if want_low else jnp.where(low, other, pair)

    k2, v2 = [], []
    for j in range(SWA_KV_HEADS // 2):
        cols = slice(j * LANES, (j + 1) * LANES)
        kpair = jnp.concatenate([kp_ref[:, cols], kc_ref[:, cols]], axis=0)
        vpair = jnp.concatenate([vp_ref[:, cols], vc_ref[:, cols]], axis=0)
        k2 += [both_halves(kpair, True), both_halves(kpair, False)]
        v2 += [both_halves(vpair, True), both_halves(vpair, False)]

    zero = jnp.zeros((blk, LANES), BF16)
    for p in range(SWA_HEADS // 2):
        g = (2 * p) // SWA_GROUP
        qp = q_ref[:, p * LANES:(p + 1) * LANES]
        qs = jnp.concatenate([jnp.where(low_q, qp, zero), jnp.where(low_q, zero, qp)], axis=0)
        s = lax.dot_general(qs, k2[g], (((1,), (1,)), ((), ())), preferred_element_type=F32)
        t = s * c + bias_ref[2 * p:2 * p + 2].reshape(2 * blk, 2 * blk)
        es, dens = [], []
        for half in range(2):
            th = t[half * blk:(half + 1) * blk]
            sk = sink_ref[0, 2 * p + half] * LOG2E
            m = jnp.maximum(jnp.max(th, axis=-1, keepdims=True), sk)
            e = jnp.exp2(th - m)
            es.append(e.astype(BF16))
            dens.append(jnp.sum(e, axis=-1, keepdims=True) + jnp.exp2(sk - m))
        o2 = jnp.dot(jnp.concatenate(es, axis=0), v2[g], preferred_element_type=F32)
        o_ref[:, p * LANES:(p + 1) * LANES] = jnp.where(
            low_q, o2[:blk] / dens[0], o2[blk:] / dens[1]).astype(o_ref.dtype)


def _swa_attention(proj, sinks, bias_tab, batch, seq, qcol, kcol, vcol, riders=()):
    rows = proj.shape[0]
    nb = seq // SWA_BLOCK
    blk = SWA_BLOCK
    cur = lambda b, n: b * nb + n
    prev = lambda b, n: b * nb + jnp.maximum(n - 1, 0)
    r_in, r_out, r_shapes, r_bytes = _rider_specs(riders, batch * nb, cur)
    blocks = (blk * SWA_Q_WIDTH * 2 + 4 * blk * SWA_KV_WIDTH * 2
              + SWA_HEADS * blk * 2 * blk * 4 + blk * SWA_Q_WIDTH * 2 + r_bytes)
    return pl.pallas_call(
        _with_riders(_swa_body, 7, 1, len(riders)),
        out_shape=[jax.ShapeDtypeStruct((rows, SWA_Q_WIDTH), BF16)] + r_shapes,
        grid=(batch, nb),
        in_specs=[pl.BlockSpec(memory_space=pltpu.SMEM),
                  pl.BlockSpec((blk, SWA_Q_WIDTH), lambda b, n: (cur(b, n), qcol)),
                  pl.BlockSpec((blk, SWA_KV_WIDTH), lambda b, n: (prev(b, n), kcol)),
                  pl.BlockSpec((blk, SWA_KV_WIDTH), lambda b, n: (cur(b, n), kcol)),
                  pl.BlockSpec((blk, SWA_KV_WIDTH), lambda b, n: (prev(b, n), vcol)),
                  pl.BlockSpec((blk, SWA_KV_WIDTH), lambda b, n: (cur(b, n), vcol)),
                  pl.BlockSpec((None, SWA_HEADS, blk, 2 * blk), lambda b, n: (jnp.minimum(n, 1), 0, 0, 0))] + r_in,
        out_specs=[pl.BlockSpec((blk, SWA_Q_WIDTH), lambda b, n: (cur(b, n), 0))] + r_out,
        compiler_params=_params(("arbitrary", "arbitrary"), _vmem_limit(blocks, temp_bytes=8 << 20)),
        name="swa",
    )(sinks, proj, proj, proj, proj, proj, bias_tab, *[w3 for w3, _ in riders])


def _rope_tables(positions):
    half = MLA_ROPE // 2
    inv = ROPE_THETA ** (-jnp.arange(half, dtype=F32) * (2.0 / MLA_ROPE))
    ang = positions.astype(F32).reshape(-1, 1) * inv
    cos, sin = jnp.cos(ang), jnp.sin(ang)
    zero = jnp.zeros((ang.shape[0], LANES - MLA_ROPE), F32)
    return (jnp.concatenate([cos, cos, zero], axis=1), jnp.concatenate([-sin, sin, zero], axis=1))


def kernel(x, c, positions, w_ada, b_ada, pre_norm_g, post_norm_g, w_in, q_norm_g, kv_norm_g, w_uq, w_ukv,
           swa_sinks, rel_bias, w_gate, w_proj_a, w_proj_b, w_out, w_ff_up, w_ff_down):
    batch, seq, d = x.shape
    depth = w_ada.shape[0]
    rows = batch * seq
    x2 = x.reshape(rows, d)

    cos_t, sin_t = _rope_tables(positions)
    bias_tab = _bias_table(rel_bias.astype(F32))
    c_pad = jnp.concatenate([c, jnp.zeros((16 - batch, d), c.dtype)], axis=0)

    o_qlat, o_kvlat, o_rope = 0, Q_LORA, Q_LORA + KV_LORA
    o_qs = o_rope + MLA_ROPE
    o_ks = o_qs + SWA_Q_WIDTH
    o_vs = o_ks + SWA_KV_WIDTH

    for l in range(depth):
        wi = w_in[l]
        w_main = jnp.concatenate([wi[:, o_qs:o_ks], wi[:, o_qlat:o_kvlat], wi[:, o_kvlat:o_rope],
                                  wi[:, o_ks:o_vs], wi[:, o_vs:o_vs + SWA_KV_WIDTH]], axis=1).astype(BF16)
        w_rope = jnp.pad(wi[:, o_rope:o_qs], ((0, 0), (0, LANES - MLA_ROPE))).astype(BF16)
        w_q = jnp.pad(w_uq[l].reshape(Q_LORA, MLA_HEADS, MLA_NOPE + MLA_ROPE),
                      ((0, 0), (0, 0), (0, MLA_QK_PAD - MLA_NOPE - MLA_ROPE))
                      ).reshape(Q_LORA, MLA_HEADS * MLA_QK_PAD).astype(BF16)
        w_kv3 = w_ukv[l].reshape(KV_LORA, MLA_HEADS, MLA_NOPE + MLA_V)
        w_kv = jnp.concatenate([w_kv3[:, :, :MLA_NOPE].reshape(KV_LORA, -1),
                                w_kv3[:, :, MLA_NOPE:].reshape(KV_LORA, -1)], axis=1).astype(BF16)
        qcol_s, col_qlat, col_kvlat = 0, SWA_Q_WIDTH // Q_LORA, (SWA_Q_WIDTH + Q_LORA) // KV_LORA
        col_ks = (SWA_Q_WIDTH + Q_LORA + KV_LORA) // SWA_KV_WIDTH
        col_vs = col_ks + 1

        mod = _modulation(c_pad, w_ada[l], b_ada[l].reshape(1, -1))[:batch]
        mod3 = mod.reshape(batch, 6, d)

        h = _prenorm(x2, mod3, pre_norm_g[l, 0].reshape(1, d), seq)
        (proj,) = _matmul(h, w_main, name="mm_in")
        kpe = _matmul_rope(h, w_rope, cos_t, sin_t)
        q = _matmul_norm(proj, col_qlat, Q_LORA, q_norm_g[l].reshape(1, -1), w_q,
                         rope_tables=(cos_t, sin_t), name="mm_uq")
        kv = _matmul_norm(proj, col_kvlat, KV_LORA, kv_norm_g[l].reshape(1, -1), w_kv, name="mm_ukv")
        att_a, w_gate_b, w_pa_b, w_pb_b = _mla_attention(
            q, kv, kpe, batch, seq, riders=[(w_gate, l), (w_proj_a, l), (w_proj_b, l)])
        att_b, w_out_b = _swa_attention(proj, swa_sinks[l].reshape(1, -1).astype(F32), bias_tab, batch, seq,
                                        qcol_s, col_ks, col_vs, riders=[(w_out, l)])
        mix, w_up_b = _gated_mix(h, att_a, att_b, w_gate_b, w_pa_b, w_pb_b, riders=[(w_ff_up, l)])
        (y,) = _matmul(mix, w_out_b, name="mm_out")
        x2, h2 = _post1(x2, y, mod3, post_norm_g[l, 0].reshape(1, d), pre_norm_g[l, 1].reshape(1, d), seq)

        u, w_down_b = _matmul(h2, w_up_b, act="relu2", riders=[(w_ff_down, l)], name="mm_up")
        y2 = _matmul_ksplit(u, w_down_b, tk=4096, name="mm_down")
        x2 = _post2(x2, y2, mod3, post_norm_g[l, 1].reshape(1, d), seq)

    return x2.reshape(batch, seq, d)
```

```python
import functools
import math

import numpy as np
import jax
import jax.numpy as jnp
from jax import lax
from jax.experimental import pallas as pl
from jax.experimental.pallas import tpu as pltpu

F32 = jnp.float32
BF16 = jnp.bfloat16

D_MODEL = 4096
CHUNK = 64
MLA_HEADS = 16
MLA_NOPE = 128
MLA_ROPE = 64
MLA_V = 128
Q_LORA = 1024
KV_LORA = 512
ROPE_THETA = 10000.0
SWA_HEADS = 32
SWA_KV_HEADS = 4
SWA_HEAD_DIM = 64
SWA_GROUP = SWA_HEADS // SWA_KV_HEADS
WINDOW = 128
SWA_BLOCK = 128
NUM_BUCKETS = 32
MAX_DISTANCE = 128
EPS = 1e-6
NEG = -1e30
LOG2E = math.log2(math.e)

LANES = 128
ROW_CHUNK = 16
ROW_UNROLL = 4
MLA_QK_PAD = 256
SWA_Q_WIDTH = SWA_HEADS * SWA_HEAD_DIM
SWA_KV_WIDTH = SWA_KV_HEADS * SWA_HEAD_DIM
VMEM_CAP_BYTES = 56 * 1024 * 1024


def _vmem_limit(block_bytes, scratch_bytes=0, temp_bytes=0):
    need = 2 * block_bytes + scratch_bytes + temp_bytes + (4 << 20)
    return int(min(max(need, 16 << 20), VMEM_CAP_BYTES))


def _params(sem, vmem):
    return pltpu.CompilerParams(dimension_semantics=sem, vmem_limit_bytes=vmem)


def _rms(x):
    return x * lax.rsqrt(jnp.mean(x * x, axis=-1, keepdims=True) + EPS)


def _rope_half_swap(x, cos_t, sin_t):
    return x * cos_t + pltpu.roll(x, LANES // 2, 1) * sin_t


def _spread_rope_columns(w, axis):
    half = MLA_ROPE // 2
    x1, x2 = jnp.split(w, 2, axis=axis)
    zero = jnp.zeros_like(x1)
    assert x1.shape[axis] == half and 4 * half == LANES
    return jnp.concatenate([x1, zero, x2, zero], axis=axis)


def _with_riders(body, n_in, n_out, n_riders):
    if n_riders == 0:
        return body

    def wrapped(*refs):
        ins = refs[:n_in]
        rider_ins = refs[n_in:n_in + n_riders]
        outs = refs[n_in + n_riders:n_in + n_riders + n_out]
        rider_outs = refs[n_in + n_riders + n_out:n_in + 2 * n_riders + n_out]
        scratch = refs[n_in + 2 * n_riders + n_out:]
        for src, dst in zip(rider_ins, rider_outs):
            dst[...] = src[...].astype(dst.dtype)
        body(*ins, *outs, *scratch)

    return wrapped


def _rider_specs(riders, n_steps, step_of):
    in_specs, out_specs, out_shapes, block_bytes = [], [], [], 0
    for w3, layer in riders:
        _, r, cols = w3.shape
        tr = r // n_steps
        assert tr * n_steps == r and tr % 16 == 0, (w3.shape, n_steps)
        in_specs.append(pl.BlockSpec((None, tr, cols), lambda *g, layer=layer: (layer, step_of(*g), 0)))
        out_specs.append(pl.BlockSpec((tr, cols), lambda *g: (step_of(*g), 0)))
        out_shapes.append(jax.ShapeDtypeStruct((r, cols), BF16))
        block_bytes += tr * cols * (4 + 2)
    return in_specs, out_specs, out_shapes, block_bytes


def _mod_body(c_ref, w_ref, b_ref, o_ref):
    c = c_ref[...]
    ca = (c * jax.nn.sigmoid(c)).astype(BF16)
    o_ref[...] = jnp.dot(ca, w_ref[...].astype(BF16), preferred_element_type=F32) + b_ref[...]


def _modulation(c_pad, w_ada, b_ada):
    m, k = c_pad.shape
    n = w_ada.shape[1]
    tn = 512
    blocks = m * k * 4 + k * tn * 4 + tn * 4 + m * tn * 4
    return pl.pallas_call(
        _mod_body,
        out_shape=jax.ShapeDtypeStruct((m, n), F32),
        grid=(n // tn,),
        in_specs=[pl.BlockSpec((m, k), lambda j: (0, 0)),
                  pl.BlockSpec((k, tn), lambda j: (0, j)),
                  pl.BlockSpec((1, tn), lambda j: (0, j))],
        out_specs=pl.BlockSpec((m, tn), lambda j: (0, j)),
        compiler_params=_params(("arbitrary",), _vmem_limit(blocks, temp_bytes=k * tn * 2)),
        name="mod",
    )(c_pad, w_ada, b_ada)


def _prenorm_body(x_ref, mod_ref, g_ref, o_ref):
    def chunk(r, carry):
        rows = pl.ds(pl.multiple_of(r * ROW_CHUNK, ROW_CHUNK), ROW_CHUNK)
        h = _rms(x_ref[rows, :]) * (g_ref[...] * (1.0 + mod_ref[1:2, :])) + mod_ref[0:1, :]
        o_ref[rows, :] = h.astype(o_ref.dtype)
        return carry

    lax.fori_loop(0, x_ref.shape[0] // ROW_CHUNK, chunk, 0, unroll=ROW_UNROLL)


def _prenorm(x2, mod3, g, seq, tr=512):
    rows, d = x2.shape
    per_b = seq // tr
    blocks = tr * d * 4 + 6 * d * 4 + d * 4 + tr * d * 2
    return pl.pallas_call(
        _prenorm_body,
        out_shape=jax.ShapeDtypeStruct((rows, d), BF16),
        grid=(rows // tr,),
        in_specs=[pl.BlockSpec((tr, d), lambda i: (i, 0)),
                  pl.BlockSpec((None, 6, d), lambda i: (i // per_b, 0, 0)),
                  pl.BlockSpec((1, d), lambda i: (0, 0))],
        out_specs=pl.BlockSpec((tr, d), lambda i: (i, 0)),
        compiler_params=_params(("arbitrary",), _vmem_limit(blocks, temp_bytes=2 * tr * d * 4)),
        name="prenorm",
    )(x2, mod3, g)


def _post1_body(x_ref, y_ref, mod_ref, pg_ref, g2_ref, h2_ref):
    def chunk(r, carry):
        rows = pl.ds(pl.multiple_of(r * ROW_CHUNK, ROW_CHUNK), ROW_CHUNK)
        x1 = x_ref[rows, :] + _rms(y_ref[rows, :].astype(F32)) * (mod_ref[2:3, :] * pg_ref[...])
        h = _rms(x1) * (g2_ref[...] * (1.0 + mod_ref[4:5, :])) + mod_ref[3:4, :]
        h2_ref[rows, :] = h.astype(h2_ref.dtype)
        return carry

    lax.fori_loop(0, x_ref.shape[0] // ROW_CHUNK, chunk, 0, unroll=ROW_UNROLL)


def _post1(x2, y, mod3, post_g, pre_g2, seq, tr=256):
    rows, d = x2.shape
    per_b = seq // tr
    blocks = tr * d * (4 + 2 + 2) + 8 * d * 4
    row = pl.BlockSpec((tr, d), lambda i: (i, 0))
    vec = pl.BlockSpec((1, d), lambda i: (0, 0))
    return pl.pallas_call(
        _post1_body,
        out_shape=jax.ShapeDtypeStruct((rows, d), BF16),
        grid=(rows // tr,),
        in_specs=[row, row, pl.BlockSpec((None, 6, d), lambda i: (i // per_b, 0, 0)), vec, vec],
        out_specs=row,
        compiler_params=_params(("arbitrary",), _vmem_limit(blocks, temp_bytes=4 * tr * d * 4)),
        name="post1",
    )(x2, y, mod3, post_g, pre_g2)


def _post2_body(x_ref, y_ref, y2_ref, mod_ref, pg1_ref, pg2_ref, o_ref):
    def chunk(r, carry):
        rows = pl.ds(pl.multiple_of(r * ROW_CHUNK, ROW_CHUNK), ROW_CHUNK)
        x1 = x_ref[rows, :] + _rms(y_ref[rows, :].astype(F32)) * (mod_ref[2:3, :] * pg1_ref[...])
        o_ref[rows, :] = x1 + _rms(y2_ref[rows, :].astype(F32)) * (mod_ref[5:6, :] * pg2_ref[...])
        return carry

    lax.fori_loop(0, x_ref.shape[0] // ROW_CHUNK, chunk, 0, unroll=ROW_UNROLL)


def _post2(x2, y, y2, mod3, post_g1, post_g2, seq, tr=256):
    rows, d = x2.shape
    per_b = seq // tr
    blocks = tr * d * (4 + 2 + 2 + 4) + 8 * d * 4
    row = pl.BlockSpec((tr, d), lambda i: (i, 0))
    vec = pl.BlockSpec((1, d), lambda i: (0, 0))
    return pl.pallas_call(
        _post2_body,
        out_shape=jax.ShapeDtypeStruct((rows, d), F32),
        grid=(rows // tr,),
        in_specs=[row, row, row, pl.BlockSpec((None, 6, d), lambda i: (i // per_b, 0, 0)), vec, vec],
        out_specs=row,
        compiler_params=_params(("arbitrary",), _vmem_limit(blocks, temp_bytes=4 * tr * d * 4)),
        name="post2",
    )(x2, y, y2, mod3, post_g1, post_g2)


def _mm_body(x_ref, w_ref, o_ref, *, act):
    acc = jnp.dot(x_ref[...], w_ref[...], preferred_element_type=F32)
    if act == "relu2":
        r = jnp.maximum(acc, 0.0)
        acc = r * r
    o_ref[...] = acc.astype(o_ref.dtype)


def _matmul(x, w, *, act=None, tm=1024, tn=1024, riders=(), name):
    m, k = x.shape
    n = w.shape[1]
    tn = min(tn, n)
    gm, gn = m // tm, n // tn
    r_in, r_out, r_shapes, r_bytes = _rider_specs(riders, gm * gn, lambda i, j: i * gn + j)
    blocks = tm * k * 2 + k * tn * 2 + tm * tn * 2 + r_bytes
    return pl.pallas_call(
        _with_riders(functools.partial(_mm_body, act=act), 2, 1, len(riders)),
        out_shape=[jax.ShapeDtypeStruct((m, n), BF16)] + r_shapes,
        grid=(gm, gn),
        in_specs=[pl.BlockSpec((tm, k), lambda i, j: (i, 0)),
                  pl.BlockSpec((k, tn), lambda i, j: (0, j))] + r_in,
        out_specs=[pl.BlockSpec((tm, tn), lambda i, j: (i, j))] + r_out,
        compiler_params=_params(("arbitrary", "arbitrary"), _vmem_limit(blocks, temp_bytes=2 * tm * tn * 4)),
        name=name,
    )(x, w, *[w3 for w3, _ in riders])


def _mm_rope_body(x_ref, w_ref, cos_ref, sin_ref, o_ref):
    acc = jnp.dot(x_ref[...], w_ref[...], preferred_element_type=F32)
    o_ref[...] = _rope_half_swap(acc, cos_ref[...], sin_ref[...]).astype(o_ref.dtype)


def _matmul_rope(x, w, cos_t, sin_t, *, tm=1024):
    m, k = x.shape
    n = w.shape[1]
    blocks = tm * k * 2 + k * n * 2 + 2 * tm * n * 4 + tm * n * 2
    return pl.pallas_call(
        _mm_rope_body,
        out_shape=jax.ShapeDtypeStruct((m, n), BF16),
        grid=(m // tm,),
        in_specs=[pl.BlockSpec((tm, k), lambda i: (i, 0)),
                  pl.BlockSpec((k, n), lambda i: (0, 0)),
                  pl.BlockSpec((tm, n), lambda i: (i, 0)),
                  pl.BlockSpec((tm, n), lambda i: (i, 0))],
        out_specs=pl.BlockSpec((tm, n), lambda i: (i, 0)),
        compiler_params=_params(("arbitrary",), _vmem_limit(blocks, temp_bytes=4 * tm * n * 4)),
        name="mm_krope",
    )(x, w, cos_t, sin_t)


def _mm_norm_body(x_ref, g_ref, w_ref, *rest, rope):
    xn_ref = rest[-1]

    @pl.when(pl.program_id(1) == 0)
    def _():
        xn_ref[...] = (_rms(x_ref[...].astype(F32)) * g_ref[...]).astype(BF16)

    acc = jnp.dot(xn_ref[...], w_ref[...], preferred_element_type=F32)
    if not rope:
        o_ref = rest[0]
        o_ref[...] = acc.astype(o_ref.dtype)
        return
    cos_ref, sin_ref, o_ref = rest[:3]
    cos_t = cos_ref[...]
    sin_t = sin_ref[...]
    for hd in range(acc.shape[1] // MLA_QK_PAD):
        c0 = hd * MLA_QK_PAD
        o_ref[:, c0:c0 + MLA_NOPE] = acc[:, c0:c0 + MLA_NOPE].astype(o_ref.dtype)
        pe = _rope_half_swap(acc[:, c0 + MLA_NOPE:c0 + MLA_QK_PAD], cos_t, sin_t)
        o_ref[:, c0 + MLA_NOPE:c0 + MLA_QK_PAD] = pe.astype(o_ref.dtype)


def _matmul_norm(x, xcol, kdim, g, w, *, rope_tables=None, tm=1024, tn=1024, name):
    m = x.shape[0]
    n = w.shape[1]
    blocks = tm * kdim * 2 + kdim * 4 + kdim * tn * 2 + tm * tn * 2
    in_specs = [pl.BlockSpec((tm, kdim), lambda i, j: (i, xcol)),
                pl.BlockSpec((1, kdim), lambda i, j: (0, 0)),
                pl.BlockSpec((kdim, tn), lambda i, j: (0, j))]
    args = [x, g, w]
    if rope_tables is not None:
        in_specs += [pl.BlockSpec((tm, LANES), lambda i, j: (i, 0))] * 2
        args += list(rope_tables)
        blocks += 2 * tm * LANES * 4
    return pl.pallas_call(
        functools.partial(_mm_norm_body, rope=rope_tables is not None),
        out_shape=jax.ShapeDtypeStruct((m, n), BF16),
        grid=(m // tm, n // tn),
        in_specs=in_specs,
        out_specs=pl.BlockSpec((tm, tn), lambda i, j: (i, j)),
        scratch_shapes=[pltpu.VMEM((tm, kdim), BF16)],
        compiler_params=_params(("arbitrary", "arbitrary"),
                                _vmem_limit(blocks, scratch_bytes=tm * kdim * 2,
                                            temp_bytes=2 * tm * tn * 4 + 3 * tm * kdim * 4)),
        name=name,
    )(*args)


def _mm_acc_body(x_ref, w_ref, o_ref, acc_ref):
    kk = pl.program_id(2)

    @pl.when(kk == 0)
    def _():
        acc_ref[...] = jnp.zeros_like(acc_ref)

    acc_ref[...] += jnp.dot(x_ref[...], w_ref[...], preferred_element_type=F32)

    @pl.when(kk == pl.num_programs(2) - 1)
    def _():
        o_ref[...] = acc_ref[...].astype(o_ref.dtype)


def _matmul_ksplit(x, w, *, tm=1024, tn=1024, tk=2048, name):
    m, k = x.shape
    n = w.shape[1]
    blocks = tm * tk * 2 + tk * tn * 2 + tm * tn * 2
    return pl.pallas_call(
        _mm_acc_body,
        out_shape=jax.ShapeDtypeStruct((m, n), BF16),
        grid=(m // tm, n // tn, k // tk),
        in_specs=[pl.BlockSpec((tm, tk), lambda i, j, kk: (i, kk)),
                  pl.BlockSpec((tk, tn), lambda i, j, kk: (kk, j))],
        out_specs=pl.BlockSpec((tm, tn), lambda i, j, kk: (i, j)),
        scratch_shapes=[pltpu.VMEM((tm, tn), F32)],
        compiler_params=_params(("arbitrary", "arbitrary", "arbitrary"),
                                _vmem_limit(blocks, scratch_bytes=tm * tn * 4, temp_bytes=2 * tm * tn * 4)),
        name=name,
    )(x, w)


def _mix_body(h_ref, a_ref, b_ref, wga_ref, wgb_ref, wa_ref, wb_ref, o_ref):
    h = h_ref[...]
    ga = jax.nn.sigmoid(jnp.dot(h, wga_ref[...], preferred_element_type=F32))
    t = ga * jnp.dot(a_ref[...], wa_ref[...], preferred_element_type=F32)
    gb = jax.nn.sigmoid(jnp.dot(h, wgb_ref[...], preferred_element_type=F32))
    t = t + gb * jnp.dot(b_ref[...], wb_ref[...], preferred_element_type=F32)
    o_ref[...] = t.astype(o_ref.dtype)


def _gated_mix(h, a, b, w_gate, w_pa, w_pb, *, tm=512, tn=512, riders=()):
    m, d = h.shape
    ka = a.shape[1]
    kb = b.shape[1]
    n = w_pa.shape[1]
    gm, nb = m // tm, n // tn
    r_in, r_out, r_shapes, r_bytes = _rider_specs(riders, gm * nb, lambda i, j: i * nb + j)
    blocks = tm * (d + ka + kb) * 2 + (2 * d + ka + kb) * tn * 2 + tm * tn * 2 + r_bytes
    return pl.pallas_call(
        _with_riders(_mix_body, 7, 1, len(riders)),
        out_shape=[jax.ShapeDtypeStruct((m, n), BF16)] + r_shapes,
        grid=(gm, nb),
        in_specs=[pl.BlockSpec((tm, d), lambda i, j: (i, 0)),
                  pl.BlockSpec((tm, ka), lambda i, j: (i, 0)),
                  pl.BlockSpec((tm, kb), lambda i, j: (i, 0)),
                  pl.BlockSpec((d, tn), lambda i, j: (0, j)),
                  pl.BlockSpec((d, tn), lambda i, j: (0, j + nb)),
                  pl.BlockSpec((ka, tn), lambda i, j: (0, j)),
                  pl.BlockSpec((kb, tn), lambda i, j: (0, j))] + r_in,
        out_specs=[pl.BlockSpec((tm, tn), lambda i, j: (i, j))] + r_out,
        compiler_params=_params(("arbitrary", "arbitrary"), _vmem_limit(blocks, temp_bytes=4 * tm * tn * 4)),
        name="mix",
    )(h, a, b, w_gate, w_gate, w_pa, w_pb, *[w3 for w3, _ in riders])


def _mla_body(q_ref, kn_ref, kp_ref, v_ref, o_ref, *, seq, tq, tk):
    c = (MLA_NOPE + MLA_ROPE) ** -0.5 * math.log2(math.e)
    ones = jnp.ones((tk, LANES), BF16)
    row = lax.broadcasted_iota(jnp.int32, (tq, tk), 0)
    col = lax.broadcasted_iota(jnp.int32, (tq, tk), 1)
    for qi in range(seq // tq):
        q = q_ref[qi * tq:(qi + 1) * tq, :]
        n_kb = -(-((qi + 1) * tq) // tk)
        m = None
        acc = None
        for kj in range(n_kb):
            ks = slice(kj * tk, (kj + 1) * tk)
            kf = jnp.concatenate([kn_ref[ks, :], kp_ref[ks, :]], axis=1)
            s = lax.dot_general(q, kf, (((1,), (1,)), ((), ())), preferred_element_type=F32)
            if (kj + 1) * tk > qi * tq + CHUNK:
                visible = (col + kj * tk) // CHUNK <= (row + qi * tq) // CHUNK
                s = jnp.where(visible, s, NEG)
            bm = jnp.max(s, axis=-1, keepdims=True)
            v_ext = jnp.concatenate([v_ref[ks, :], ones], axis=1)
            if m is None:
                m = bm
                p = jnp.exp2((s - m) * c)
                acc = jnp.dot(p.astype(BF16), v_ext, preferred_element_type=F32)
            else:
                m_new = jnp.maximum(m, bm)
                alpha = jnp.exp2((m - m_new) * c)
                p = jnp.exp2((s - m_new) * c)
                acc = alpha * acc + jnp.dot(p.astype(BF16), v_ext, preferred_element_type=F32)
                m = m_new
        o_ref[qi * tq:(qi + 1) * tq, :] = (acc[:, :MLA_V] / acc[:, MLA_V:]).astype(o_ref.dtype)


def _mla_attention(q, kv, kpe, batch, seq, tq=256, tk=256, riders=()):
    rows = q.shape[0]
    r_in, r_out, r_shapes, r_bytes = _rider_specs(riders, batch * MLA_HEADS, lambda b, h: b * MLA_HEADS + h)
    blocks = seq * MLA_QK_PAD * 2 + 3 * seq * LANES * 2 + seq * MLA_V * 2 + r_bytes
    return pl.pallas_call(
        _with_riders(functools.partial(_mla_body, seq=seq, tq=tq, tk=tk), 4, 1, len(riders)),
        out_shape=[jax.ShapeDtypeStruct((rows, MLA_HEADS * MLA_V), BF16)] + r_shapes,
        grid=(batch, MLA_HEADS),
        in_specs=[pl.BlockSpec((seq, MLA_QK_PAD), lambda b, h: (b, h)),
                  pl.BlockSpec((seq, MLA_NOPE), lambda b, h: (b, h)),
                  pl.BlockSpec((seq, LANES), lambda b, h: (b, 0)),
                  pl.BlockSpec((seq, MLA_V), lambda b, h: (b, MLA_HEADS + h))] + r_in,
        out_specs=[pl.BlockSpec((seq, MLA_V), lambda b, h: (b, h))] + r_out,
        compiler_params=_params(("arbitrary", "arbitrary"), _vmem_limit(blocks, temp_bytes=8 << 20)),
        name="mla",
    )(q, kv, kpe, kv, *[w3 for w3, _ in riders])


def _t5_bucket_table():
    iq = np.arange(SWA_BLOCK)
    ik = np.arange(2 * SWA_BLOCK) - SWA_BLOCK
    rel = ik[None, :] - iq[:, None]
    nb = NUM_BUCKETS // 2
    max_exact = nb // 2
    ret = np.where(rel > 0, nb, 0)
    n = np.abs(rel)
    nf = np.maximum(n, 1).astype(np.float64)
    large = max_exact + (np.log(nf / max_exact) / math.log(MAX_DISTANCE / max_exact)
                         * (nb - max_exact)).astype(np.int32)
    large = np.minimum(large, nb - 1)
    return (ret + np.where(n < max_exact, n, large)).astype(np.int32)


def _bias_body(rb_ref, bucket_ref, o_ref):
    first = pl.program_id(0) == 0
    hd = pl.program_id(1)
    bk = bucket_ref[...]
    tab = jnp.zeros(bk.shape, F32)
    for b in range(NUM_BUCKETS):
        tab = jnp.where(bk == b, rb_ref[b, hd] * LOG2E, tab)
    col = lax.broadcasted_iota(jnp.int32, bk.shape, 1)
    qc = lax.broadcasted_iota(jnp.int32, bk.shape, 0) // CHUNK
    kc = col // CHUNK - SWA_BLOCK // CHUNK
    valid = jnp.logical_and(kc <= qc, kc >= qc - WINDOW // CHUNK)
    valid = jnp.logical_and(valid, jnp.logical_not(jnp.logical_and(first, col < SWA_BLOCK)))
    o_ref[...] = jnp.where(valid, tab, NEG)


def _bias_table(rel_bias):
    bucket = jnp.asarray(_t5_bucket_table())
    return pl.pallas_call(
        _bias_body,
        out_shape=jax.ShapeDtypeStruct((2, SWA_HEADS, SWA_BLOCK, 2 * SWA_BLOCK), F32),
        grid=(2, SWA_HEADS),
        in_specs=[pl.BlockSpec(memory_space=pltpu.SMEM),
                  pl.BlockSpec((SWA_BLOCK, 2 * SWA_BLOCK), lambda f, h: (0, 0))],
        out_specs=pl.BlockSpec((None, None, SWA_BLOCK, 2 * SWA_BLOCK), lambda f, h: (f, h, 0, 0)),
        compiler_params=_params(("arbitrary", "arbitrary"), 16 << 20),
        name="swa_bias",
    )(rel_bias, bucket)


def _swa_body(sink_ref, q_ref, kp_ref, kc_ref, vp_ref, vc_ref, bias_ref, o_ref):
    c = SWA_HEAD_DIM ** -0.5 * LOG2E
    blk = SWA_BLOCK
    low = lax.broadcasted_iota(jnp.int32, (2 * blk, LANES), 1) < SWA_HEAD_DIM
    low_q = lax.broadcasted_iota(jnp.int32, (blk, LANES), 1) < SWA_HEAD_DIM

    def both_halves(pair, want_low):
        other = pltpu.roll(pair, SWA_HEAD_DIM, 1)
        return jnp.where(low, pair, other) if want_low else jnp.where(low, other, pair)

    k2, v2 = [], []
    for j in range(SWA_KV_HEADS // 2):
        cols = slice(j * LANES, (j + 1) * LANES)
        kpair = jnp.concatenate([kp_ref[:, cols], kc_ref[:, cols]], axis=0)
        vpair = jnp.concatenate([vp_ref[:, cols], vc_ref[:, cols]], axis=0)
        k2 += [both_halves(kpair, True), both_halves(kpair, False)]
        v2 += [both_halves(vpair, True), both_halves(vpair, False)]

    zero = jnp.zeros((blk, LANES), BF16)
    for p in range(SWA_HEADS // 2):
        g = (2 * p) // SWA_GROUP
        qp = q_ref[:, p * LANES:(p + 1) * LANES]
        qs = jnp.concatenate([jnp.where(low_q, qp, zero), jnp.where(low_q, zero, qp)], axis=0)
        s = lax.dot_general(qs, k2[g], (((1,), (1,)), ((), ())), preferred_element_type=F32)
        t = s * c + bias_ref[2 * p:2 * p + 2].reshape(2 * blk, 2 * blk)
        es, dens = [], []
        for half in range(2):
            th = t[half * blk:(half + 1) * blk]
            sk = sink_ref[0, 2 * p + half] * LOG2E
            m = jnp.maximum(jnp.max(th, axis=-1, keepdims=True), sk)
            e = jnp.exp2(th - m)
            es.append(e.astype(BF16))
            dens.append(jnp.sum(e, axis=-1, keepdims=True) + jnp.exp2(sk - m))
        o2 = jnp.dot(jnp.concatenate(es, axis=0), v2[g], preferred_element_type=F32)
        o_ref[:, p * LANES:(p + 1) * LANES] = jnp.where(
            low_q, o2[:blk] / dens[0], o2[blk:] / dens[1]).astype(o_ref.dtype)


def _swa_attention(proj, sinks, bias_tab, batch, seq, qcol, kcol, vcol, riders=()):
    rows = proj.shape[0]
    nb = seq // SWA_BLOCK
    blk = SWA_BLOCK
    cur = lambda b, n: b * nb + n
    prev = lambda b, n: b * nb + jnp.maximum(n - 1, 0)
    r_in, r_out, r_shapes, r_bytes = _rider_specs(riders, batch * nb, cur)
    blocks = (blk * SWA_Q_WIDTH * 2 + 4 * blk * SWA_KV_WIDTH * 2
              + SWA_HEADS * blk * 2 * blk * 4 + blk * SWA_Q_WIDTH * 2 + r_bytes)
    return pl.pallas_call(
        _with_riders(_swa_body, 7, 1, len(riders)),
        out_shape=[jax.ShapeDtypeStruct((rows, SWA_Q_WIDTH), BF16)] + r_shapes,
        grid=(batch, nb),
        in_specs=[pl.BlockSpec(memory_space=pltpu.SMEM),
                  pl.BlockSpec((blk, SWA_Q_WIDTH), lambda b, n: (cur(b, n), qcol)),
                  pl.BlockSpec((blk, SWA_KV_WIDTH), lambda b, n: (prev(b, n), kcol)),
                  pl.BlockSpec((blk, SWA_KV_WIDTH), lambda b, n: (cur(b, n), kcol)),
                  pl.BlockSpec((blk, SWA_KV_WIDTH), lambda b, n: (prev(b, n), vcol)),
                  pl.BlockSpec((blk, SWA_KV_WIDTH), lambda b, n: (cur(b, n), vcol)),
                  pl.BlockSpec((None, SWA_HEADS, blk, 2 * blk), lambda b, n: (jnp.minimum(n, 1), 0, 0, 0))] + r_in,
        out_specs=[pl.BlockSpec((blk, SWA_Q_WIDTH), lambda b, n: (cur(b, n), 0))] + r_out,
        compiler_params=_params(("arbitrary", "arbitrary"), _vmem_limit(blocks, temp_bytes=8 << 20)),
        name="swa",
    )(sinks, proj, proj, proj, proj, proj, bias_tab, *[w3 for w3, _ in riders])


def _rope_tables(positions):
    half = MLA_ROPE // 2
    inv = ROPE_THETA ** (-jnp.arange(half, dtype=F32) * (2.0 / MLA_ROPE))
    ang = positions.astype(F32).reshape(-1, 1) * inv
    cos, sin = jnp.cos(ang), jnp.sin(ang)
    zero = jnp.zeros_like(cos)
    return (jnp.concatenate([cos, zero, cos, zero], axis=1), jnp.concatenate([-sin, zero, sin, zero], axis=1))


def kernel(x, c, positions, w_ada, b_ada, pre_norm_g, post_norm_g, w_in, q_norm_g, kv_norm_g, w_uq, w_ukv,
           swa_sinks, rel_bias, w_gate, w_proj_a, w_proj_b, w_out, w_ff_up, w_ff_down):
    batch, seq, d = x.shape
    depth = w_ada.shape[0]
    rows = batch * seq
    x2 = x.reshape(rows, d)

    cos_t, sin_t = _rope_tables(positions)
    bias_tab = _bias_table(rel_bias.astype(F32))
    c_pad = jnp.concatenate([c, jnp.zeros((16 - batch, d), c.dtype)], axis=0)

    o_qlat, o_kvlat, o_rope = 0, Q_LORA, Q_LORA + KV_LORA
    o_qs = o_rope + MLA_ROPE
    o_ks = o_qs + SWA_Q_WIDTH
    o_vs = o_ks + SWA_KV_WIDTH

    for l in range(depth):
        wi = w_in[l].astype(BF16)
        w_main = jnp.concatenate([wi[:, o_qs:o_ks], wi[:, o_qlat:o_rope], wi[:, o_ks:o_vs + SWA_KV_WIDTH]],
                                 axis=1)
        w_rope = _spread_rope_columns(wi[:, o_rope:o_qs], 1)
        w_q3 = w_uq[l].astype(BF16).reshape(Q_LORA, MLA_HEADS, MLA_NOPE + MLA_ROPE)
        w_q = jnp.concatenate([w_q3[:, :, :MLA_NOPE], _spread_rope_columns(w_q3[:, :, MLA_NOPE:], 2)],
                              axis=2).reshape(Q_LORA, MLA_HEADS * MLA_QK_PAD)
        w_kv3 = w_ukv[l].reshape(KV_LORA, MLA_HEADS, MLA_NOPE + MLA_V)
        w_kv = jnp.concatenate([w_kv3[:, :, :MLA_NOPE].reshape(KV_LORA, -1),
                                w_kv3[:, :, MLA_NOPE:].reshape(KV_LORA, -1)], axis=1).astype(BF16)
        qcol_s, col_qlat, col_kvlat = 0, SWA_Q_WIDTH // Q_LORA, (SWA_Q_WIDTH + Q_LORA) // KV_LORA
        col_ks = (SWA_Q_WIDTH + Q_LORA + KV_LORA) // SWA_KV_WIDTH
        col_vs = col_ks + 1

        mod = _modulation(c_pad, w_ada[l], b_ada[l].reshape(1, -1))[:batch]
        mod3 = mod.reshape(batch, 6, d)

        h = _prenorm(x2, mod3, pre_norm_g[l, 0].reshape(1, d), seq)
        (proj,) = _matmul(h, w_main, name="mm_in")
        kpe = _matmul_rope(h, w_rope, cos_t, sin_t)
        q = _matmul_norm(proj, col_qlat, Q_LORA, q_norm_g[l].reshape(1, -1), w_q,
                         rope_tables=(cos_t, sin_t), name="mm_uq")
        kv = _matmul_norm(proj, col_kvlat, KV_LORA, kv_norm_g[l].reshape(1, -1), w_kv, name="mm_ukv")
        att_a, w_gate_b, w_pa_b, w_pb_b = _mla_attention(
            q, kv, kpe, batch, seq, riders=[(w_gate, l), (w_proj_a, l), (w_proj_b, l)])
        att_b, w_out_b = _swa_attention(proj, swa_sinks[l].reshape(1, -1).astype(F32), bias_tab, batch, seq,
                                        qcol_s, col_ks, col_vs, riders=[(w_out, l)])
        mix, w_up_b = _gated_mix(h, att_a, att_b, w_gate_b, w_pa_b, w_pb_b, riders=[(w_ff_up, l)])
        (y,) = _matmul(mix, w_out_b, name="mm_out")
        post_g1 = post_norm_g[l, 0].reshape(1, d)
        h2 = _post1(x2, y, mod3, post_g1, pre_norm_g[l, 1].reshape(1, d), seq)

        u, w_down_b = _matmul(h2, w_up_b, act="relu2", riders=[(w_ff_down, l)], name="mm_up")
        y2 = _matmul_ksplit(u, w_down_b, tk=4096, name="mm_down")
        x2 = _post2(x2, y, y2, mod3, post_g1, post_norm_g[l, 1].reshape(1, d), seq)

    return x2.reshape(batch, seq, d)
```

```python
import functools
import math

import numpy as np
import jax
import jax.numpy as jnp
from jax import lax
from jax.experimental import pallas as pl
from jax.experimental.pallas import tpu as pltpu

F32 = jnp.float32
BF16 = jnp.bfloat16

D_MODEL = 4096
CHUNK = 64
MLA_HEADS = 16
MLA_NOPE = 128
MLA_ROPE = 64
MLA_V = 128
Q_LORA = 1024
KV_LORA = 512
ROPE_THETA = 10000.0
SWA_HEADS = 32
SWA_KV_HEADS = 4
SWA_HEAD_DIM = 64
SWA_GROUP = SWA_HEADS // SWA_KV_HEADS
WINDOW = 128
SWA_BLOCK = 128
NUM_BUCKETS = 32
MAX_DISTANCE = 128
EPS = 1e-6
NEG = -1e30
LOG2E = math.log2(math.e)

LANES = 128
ROW_CHUNK = 16
ROW_UNROLL = 4
MLA_QK_PAD = 256
SWA_Q_WIDTH = SWA_HEADS * SWA_HEAD_DIM
SWA_KV_WIDTH = SWA_KV_HEADS * SWA_HEAD_DIM
VMEM_CAP_BYTES = 56 * 1024 * 1024


def _vmem_limit(block_bytes, scratch_bytes=0, temp_bytes=0):
    need = 2 * block_bytes + scratch_bytes + temp_bytes + (4 << 20)
    return int(min(max(need, 16 << 20), VMEM_CAP_BYTES))


def _params(sem, vmem):
    return pltpu.CompilerParams(dimension_semantics=sem, vmem_limit_bytes=vmem)


def _rms(x):
    return x * lax.rsqrt(jnp.mean(x * x, axis=-1, keepdims=True) + EPS)


def _rope_half_swap(x, cos_t, sin_t):
    return x * cos_t + pltpu.roll(x, LANES // 2, 1) * sin_t


def _spread_rope_columns(w, axis):
    half = MLA_ROPE // 2
    x1, x2 = jnp.split(w, 2, axis=axis)
    zero = jnp.zeros_like(x1)
    assert x1.shape[axis] == half and 4 * half == LANES
    return jnp.concatenate([x1, zero, x2, zero], axis=axis)


def _with_riders(body, n_in, n_out, n_riders):
    if n_riders == 0:
        return body

    def wrapped(*refs):
        ins = refs[:n_in]
        rider_ins = refs[n_in:n_in + n_riders]
        outs = refs[n_in + n_riders:n_in + n_riders + n_out]
        rider_outs = refs[n_in + n_riders + n_out:n_in + 2 * n_riders + n_out]
        scratch = refs[n_in + 2 * n_riders + n_out:]
        for src, dst in zip(rider_ins, rider_outs):
            dst[...] = src[...].astype(dst.dtype)
        body(*ins, *outs, *scratch)

    return wrapped


def _rider_specs(riders, n_steps, step_of):
    in_specs, out_specs, out_shapes, block_bytes = [], [], [], 0
    for w3, layer in riders:
        _, r, cols = w3.shape
        tr = r // n_steps
        assert tr * n_steps == r and tr % 16 == 0, (w3.shape, n_steps)
        in_specs.append(pl.BlockSpec((None, tr, cols), lambda *g, layer=layer: (layer, step_of(*g), 0)))
        out_specs.append(pl.BlockSpec((tr, cols), lambda *g: (step_of(*g), 0)))
        out_shapes.append(jax.ShapeDtypeStruct((r, cols), BF16))
        block_bytes += tr * cols * (4 + 2)
    return in_specs, out_specs, out_shapes, block_bytes


def _mod_body(c_ref, w_ref, b_ref, o_ref):
    c = c_ref[...]
    ca = (c * jax.nn.sigmoid(c)).astype(BF16)
    o_ref[...] = jnp.dot(ca, w_ref[...].astype(BF16), preferred_element_type=F32) + b_ref[...]


def _modulation(c_pad, w_ada, b_ada):
    m, k = c_pad.shape
    n = w_ada.shape[1]
    tn = 512
    blocks = m * k * 4 + k * tn * 4 + tn * 4 + m * tn * 4
    return pl.pallas_call(
        _mod_body,
        out_shape=jax.ShapeDtypeStruct((m, n), F32),
        grid=(n // tn,),
        in_specs=[pl.BlockSpec((m, k), lambda j: (0, 0)),
                  pl.BlockSpec((k, tn), lambda j: (0, j)),
                  pl.BlockSpec((1, tn), lambda j: (0, j))],
        out_specs=pl.BlockSpec((m, tn), lambda j: (0, j)),
        compiler_params=_params(("arbitrary",), _vmem_limit(blocks, temp_bytes=k * tn * 2)),
        name="mod",
    )(c_pad, w_ada, b_ada)


def _token_mixer_input(src, mod_ref, vec, rows):
    (x_ref,), (g_ref,) = src, vec
    return _rms(x_ref[rows, :]) * (g_ref[...] * (1.0 + mod_ref[1:2, :])) + mod_ref[0:1, :]


def _channel_mixer_input(src, mod_ref, vec, rows):
    (x_ref, y_ref), (pg_ref, g_ref) = src, vec
    x1 = x_ref[rows, :] + _rms(y_ref[rows, :].astype(F32)) * (mod_ref[2:3, :] * pg_ref[...])
    return _rms(x1) * (g_ref[...] * (1.0 + mod_ref[4:5, :])) + mod_ref[3:4, :]


def _produce_matmul_body(*refs, n_src, n_vec, produce, keep_lhs, act):
    src, mod_ref = refs[:n_src], refs[n_src]
    vec, w_ref = refs[n_src + 1:n_src + 1 + n_vec], refs[n_src + 1 + n_vec]
    outs = refs[n_src + 2 + n_vec:]
    o_ref, lhs_out_ref = outs[0], (outs[1] if keep_lhs else None)
    buf_a, buf_b = outs[-2:]
    i = pl.program_id(0)
    j = pl.program_id(1)
    slab = src[0].shape[0]

    def step(build_ref, ready_ref):
        for c in range(slab // ROW_CHUNK):
            rows = slice(c * ROW_CHUNK, (c + 1) * ROW_CHUNK)
            h = produce(src, mod_ref, vec, rows).astype(build_ref.dtype)
            dst = pl.ds(pl.multiple_of(j * slab + c * ROW_CHUNK, ROW_CHUNK), ROW_CHUNK)
            build_ref[dst, :] = h
            if keep_lhs:
                lhs_out_ref[rows, :] = h
        if ready_ref is None:
            return
        acc = jnp.dot(ready_ref[...], w_ref[...], preferred_element_type=F32)
        if act == "relu2":
            acc = jnp.maximum(acc, 0.0)
            acc = acc * acc
        o_ref[...] = acc.astype(o_ref.dtype)

    @pl.when(i == 0)
    def _():
        step(buf_a, None)

    @pl.when(jnp.logical_and(i > 0, i % 2 == 0))
    def _():
        step(buf_a, buf_b)

    @pl.when(i % 2 == 1)
    def _():
        step(buf_b, buf_a)


def _produce_matmul(src, mod3, vec, w, seq, *, produce, keep_lhs=False, act=None, tm=1024, tn=1024,
                    riders=(), name):
    m, d = src[0].shape
    n = w.shape[1]
    gm, gn = m // tm, n // tn
    slab = tm // gn
    assert slab * gn == tm and slab % ROW_CHUNK == 0 and seq % tm == 0
    build_tile = lambda i: jnp.minimum(i, gm - 1)
    ready_tile = lambda i: jnp.maximum(i - 1, 0)
    ready_col = lambda i, j: jnp.where(i == 0, 0, j)
    slab_of = lambda i, j: jnp.where(i < gm, i * gn + j, gm * gn - 1)
    slab_spec = pl.BlockSpec((slab, d), lambda i, j: (slab_of(i, j), 0))
    vec_spec = pl.BlockSpec((1, d), lambda i, j: (0, 0))
    r_in, r_out, r_shapes, r_bytes = _rider_specs(
        riders, gm * gn, lambda i, j: ready_tile(i) * gn + ready_col(i, j))
    src_bytes = sum(slab * d * a.dtype.itemsize for a in src)
    blocks = src_bytes + (6 + len(vec)) * d * 4 + d * tn * 2 + tm * tn * 2 + keep_lhs * slab * d * 2 + r_bytes
    body = functools.partial(_produce_matmul_body, n_src=len(src), n_vec=len(vec), produce=produce,
                             keep_lhs=keep_lhs, act=act)
    return pl.pallas_call(
        _with_riders(body, len(src) + len(vec) + 2, 1 + keep_lhs, len(riders)),
        out_shape=([jax.ShapeDtypeStruct((m, n), BF16)] + [jax.ShapeDtypeStruct((m, d), BF16)] * keep_lhs
                   + r_shapes),
        grid=(gm + 1, gn),
        in_specs=([slab_spec] * len(src)
                  + [pl.BlockSpec((None, 6, d), lambda i, j: (build_tile(i) * tm // seq, 0, 0))]
                  + [vec_spec] * len(vec)
                  + [pl.BlockSpec((d, tn), lambda i, j: (0, ready_col(i, j)))] + r_in),
        out_specs=([pl.BlockSpec((tm, tn), lambda i, j: (ready_tile(i), ready_col(i, j)))]
                   + [slab_spec] * keep_lhs + r_out),
        scratch_shapes=[pltpu.VMEM((tm, d), BF16), pltpu.VMEM((tm, d), BF16)],
        compiler_params=_params(("arbitrary", "arbitrary"),
                                _vmem_limit(blocks, scratch_bytes=2 * tm * d * 2, temp_bytes=2 * tm * tn * 4)),
        name=name,
    )(*src, mod3, *vec, w, *[w3 for w3, _ in riders])


def _post2_body(x_ref, y_ref, y2_ref, mod_ref, pg1_ref, pg2_ref, o_ref):
    def chunk(r, carry):
        rows = pl.ds(pl.multiple_of(r * ROW_CHUNK, ROW_CHUNK), ROW_CHUNK)
        x1 = x_ref[rows, :] + _rms(y_ref[rows, :].astype(F32)) * (mod_ref[2:3, :] * pg1_ref[...])
        o_ref[rows, :] = x1 + _rms(y2_ref[rows, :].astype(F32)) * (mod_ref[5:6, :] * pg2_ref[...])
        return carry

    lax.fori_loop(0, x_ref.shape[0] // ROW_CHUNK, chunk, 0, unroll=ROW_UNROLL)


def _post2(x2, y, y2, mod3, post_g1, post_g2, seq, tr=256):
    rows, d = x2.shape
    per_b = seq // tr
    blocks = tr * d * (4 + 2 + 2 + 4) + 8 * d * 4
    row = pl.BlockSpec((tr, d), lambda i: (i, 0))
    vec = pl.BlockSpec((1, d), lambda i: (0, 0))
    return pl.pallas_call(
        _post2_body,
        out_shape=jax.ShapeDtypeStruct((rows, d), F32),
        grid=(rows // tr,),
        in_specs=[row, row, row, pl.BlockSpec((None, 6, d), lambda i: (i // per_b, 0, 0)), vec, vec],
        out_specs=row,
        compiler_params=_params(("arbitrary",), _vmem_limit(blocks, temp_bytes=4 * tr * d * 4)),
        name="post2",
    )(x2, y, y2, mod3, post_g1, post_g2)


def _mm_body(x_ref, w_ref, o_ref):
    o_ref[...] = jnp.dot(x_ref[...], w_ref[...], preferred_element_type=F32).astype(o_ref.dtype)


def _matmul(x, w, *, tm=1024, tn=1024, riders=(), name):
    m, k = x.shape
    n = w.shape[1]
    tn = min(tn, n)
    gm, gn = m // tm, n // tn
    r_in, r_out, r_shapes, r_bytes = _rider_specs(riders, gm * gn, lambda i, j: i * gn + j)
    blocks = tm * k * 2 + k * tn * 2 + tm * tn * 2 + r_bytes
    return pl.pallas_call(
        _with_riders(_mm_body, 2, 1, len(riders)),
        out_shape=[jax.ShapeDtypeStruct((m, n), BF16)] + r_shapes,
        grid=(gm, gn),
        in_specs=[pl.BlockSpec((tm, k), lambda i, j: (i, 0)),
                  pl.BlockSpec((k, tn), lambda i, j: (0, j))] + r_in,
        out_specs=[pl.BlockSpec((tm, tn), lambda i, j: (i, j))] + r_out,
        compiler_params=_params(("arbitrary", "arbitrary"), _vmem_limit(blocks, temp_bytes=2 * tm * tn * 4)),
        name=name,
    )(x, w, *[w3 for w3, _ in riders])


def _mm_rope_body(x_ref, w_ref, cos_ref, sin_ref, o_ref):
    acc = jnp.dot(x_ref[...], w_ref[...], preferred_element_type=F32)
    o_ref[...] = _rope_half_swap(acc, cos_ref[...], sin_ref[...]).astype(o_ref.dtype)


def _matmul_rope(x, w, cos_t, sin_t, *, tm=1024):
    m, k = x.shape
    n = w.shape[1]
    blocks = tm * k * 2 + k * n * 2 + 2 * tm * n * 4 + tm * n * 2
    return pl.pallas_call(
        _mm_rope_body,
        out_shape=jax.ShapeDtypeStruct((m, n), BF16),
        grid=(m // tm,),
        in_specs=[pl.BlockSpec((tm, k), lambda i: (i, 0)),
                  pl.BlockSpec((k, n), lambda i: (0, 0)),
                  pl.BlockSpec((tm, n), lambda i: (i, 0)),
                  pl.BlockSpec((tm, n), lambda i: (i, 0))],
        out_specs=pl.BlockSpec((tm, n), lambda i: (i, 0)),
        compiler_params=_params(("arbitrary",), _vmem_limit(blocks, temp_bytes=4 * tm * n * 4)),
        name="mm_krope",
    )(x, w, cos_t, sin_t)


def _mm_norm_body(x_ref, g_ref, w_ref, *rest, rope):
    xn_ref = rest[-1]

    @pl.when(pl.program_id(1) == 0)
    def _():
        xn_ref[...] = (_rms(x_ref[...].astype(F32)) * g_ref[...]).astype(BF16)

    acc = jnp.dot(xn_ref[...], w_ref[...], preferred_element_type=F32)
    if not rope:
        o_ref = rest[0]
        o_ref[...] = acc.astype(o_ref.dtype)
        return
    cos_ref, sin_ref, o_ref = rest[:3]
    cos_t = cos_ref[...]
    sin_t = sin_ref[...]
    for hd in range(acc.shape[1] // MLA_QK_PAD):
        c0 = hd * MLA_QK_PAD
        o_ref[:, c0:c0 + MLA_NOPE] = acc[:, c0:c0 + MLA_NOPE].astype(o_ref.dtype)
        pe = _rope_half_swap(acc[:, c0 + MLA_NOPE:c0 + MLA_QK_PAD], cos_t, sin_t)
        o_ref[:, c0 + MLA_NOPE:c0 + MLA_QK_PAD] = pe.astype(o_ref.dtype)


def _matmul_norm(x, xcol, kdim, g, w, *, rope_tables=None, tm=1024, tn=1024, name):
    m = x.shape[0]
    n = w.shape[1]
    blocks = tm * kdim * 2 + kdim * 4 + kdim * tn * 2 + tm * tn * 2
    in_specs = [pl.BlockSpec((tm, kdim), lambda i, j: (i, xcol)),
                pl.BlockSpec((1, kdim), lambda i, j: (0, 0)),
                pl.BlockSpec((kdim, tn), lambda i, j: (0, j))]
    args = [x, g, w]
    if rope_tables is not None:
        in_specs += [pl.BlockSpec((tm, LANES), lambda i, j: (i, 0))] * 2
        args += list(rope_tables)
        blocks += 2 * tm * LANES * 4
    return pl.pallas_call(
        functools.partial(_mm_norm_body, rope=rope_tables is not None),
        out_shape=jax.ShapeDtypeStruct((m, n), BF16),
        grid=(m // tm, n // tn),
        in_specs=in_specs,
        out_specs=pl.BlockSpec((tm, tn), lambda i, j: (i, j)),
        scratch_shapes=[pltpu.VMEM((tm, kdim), BF16)],
        compiler_params=_params(("arbitrary", "arbitrary"),
                                _vmem_limit(blocks, scratch_bytes=tm * kdim * 2,
                                            temp_bytes=2 * tm * tn * 4 + 3 * tm * kdim * 4)),
        name=name,
    )(*args)


def _mm_acc_body(x_ref, w_ref, o_ref, acc_ref):
    kk = pl.program_id(2)

    @pl.when(kk == 0)
    def _():
        acc_ref[...] = jnp.zeros_like(acc_ref)

    acc_ref[...] += jnp.dot(x_ref[...], w_ref[...], preferred_element_type=F32)

    @pl.when(kk == pl.num_programs(2) - 1)
    def _():
        o_ref[...] = acc_ref[...].astype(o_ref.dtype)


def _matmul_ksplit(x, w, *, tm=1024, tn=1024, tk=2048, name):
    m, k = x.shape
    n = w.shape[1]
    blocks = tm * tk * 2 + tk * tn * 2 + tm * tn * 2
    return pl.pallas_call(
        _mm_acc_body,
        out_shape=jax.ShapeDtypeStruct((m, n), BF16),
        grid=(m // tm, n // tn, k // tk),
        in_specs=[pl.BlockSpec((tm, tk), lambda i, j, kk: (i, kk)),
                  pl.BlockSpec((tk, tn), lambda i, j, kk: (kk, j))],
        out_specs=pl.BlockSpec((tm, tn), lambda i, j, kk: (i, j)),
        scratch_shapes=[pltpu.VMEM((tm, tn), F32)],
        compiler_params=_params(("arbitrary", "arbitrary", "arbitrary"),
                                _vmem_limit(blocks, scratch_bytes=tm * tn * 4, temp_bytes=2 * tm * tn * 4)),
        name=name,
    )(x, w)


def _mix_body(h_ref, a_ref, b_ref, wga_ref, wgb_ref, wa_ref, wb_ref, o_ref):
    h = h_ref[...]
    ga = jax.nn.sigmoid(jnp.dot(h, wga_ref[...], preferred_element_type=F32))
    t = ga * jnp.dot(a_ref[...], wa_ref[...], preferred_element_type=F32)
    gb = jax.nn.sigmoid(jnp.dot(h, wgb_ref[...], preferred_element_type=F32))
    t = t + gb * jnp.dot(b_ref[...], wb_ref[...], preferred_element_type=F32)
    o_ref[...] = t.astype(o_ref.dtype)


def _gated_mix(h, a, b, w_gate, w_pa, w_pb, *, tm=512, tn=512, riders=()):
    m, d = h.shape
    ka = a.shape[1]
    kb = b.shape[1]
    n = w_pa.shape[1]
    gm, nb = m // tm, n // tn
    r_in, r_out, r_shapes, r_bytes = _rider_specs(riders, gm * nb, lambda i, j: i * nb + j)
    blocks = tm * (d + ka + kb) * 2 + (2 * d + ka + kb) * tn * 2 + tm * tn * 2 + r_bytes
    return pl.pallas_call(
        _with_riders(_mix_body, 7, 1, len(riders)),
        out_shape=[jax.ShapeDtypeStruct((m, n), BF16)] + r_shapes,
        grid=(gm, nb),
        in_specs=[pl.BlockSpec((tm, d), lambda i, j: (i, 0)),
                  pl.BlockSpec((tm, ka), lambda i, j: (i, 0)),
                  pl.BlockSpec((tm, kb), lambda i, j: (i, 0)),
                  pl.BlockSpec((d, tn), lambda i, j: (0, j)),
                  pl.BlockSpec((d, tn), lambda i, j: (0, j + nb)),
                  pl.BlockSpec((ka, tn), lambda i, j: (0, j)),
                  pl.BlockSpec((kb, tn), lambda i, j: (0, j))] + r_in,
        out_specs=[pl.BlockSpec((tm, tn), lambda i, j: (i, j))] + r_out,
        compiler_params=_params(("arbitrary", "arbitrary"), _vmem_limit(blocks, temp_bytes=4 * tm * tn * 4)),
        name="mix",
    )(h, a, b, w_gate, w_gate, w_pa, w_pb, *[w3 for w3, _ in riders])


def _mla_body(q_ref, kn_ref, kp_ref, v_ref, o_ref, *, seq, tq, tk):
    c = (MLA_NOPE + MLA_ROPE) ** -0.5 * math.log2(math.e)
    ones = jnp.ones((tk, LANES), BF16)
    row = lax.broadcasted_iota(jnp.int32, (tq, tk), 0)
    col = lax.broadcasted_iota(jnp.int32, (tq, tk), 1)
    for qi in range(seq // tq):
        q = q_ref[qi * tq:(qi + 1) * tq, :]
        n_kb = -(-((qi + 1) * tq) // tk)
        m = None
        acc = None
        for kj in range(n_kb):
            ks = slice(kj * tk, (kj + 1) * tk)
            kf = jnp.concatenate([kn_ref[ks, :], kp_ref[ks, :]], axis=1)
            s = lax.dot_general(q, kf, (((1,), (1,)), ((), ())), preferred_element_type=F32)
            if (kj + 1) * tk > qi * tq + CHUNK:
                visible = (col + kj * tk) // CHUNK <= (row + qi * tq) // CHUNK
                s = jnp.where(visible, s, NEG)
            bm = jnp.max(s, axis=-1, keepdims=True)
            v_ext = jnp.concatenate([v_ref[ks, :], ones], axis=1)
            if m is None:
                m = bm
                p = jnp.exp2((s - m) * c)
                acc = jnp.dot(p.astype(BF16), v_ext, preferred_element_type=F32)
            else:
                m_new = jnp.maximum(m, bm)
                alpha = jnp.exp2((m - m_new) * c)
                p = jnp.exp2((s - m_new) * c)
                acc = alpha * acc + jnp.dot(p.astype(BF16), v_ext, preferred_element_type=F32)
                m = m_new
        o_ref[qi * tq:(qi + 1) * tq, :] = (acc[:, :MLA_V] / acc[:, MLA_V:]).astype(o_ref.dtype)


def _mla_attention(q, kv, kpe, batch, seq, tq=256, tk=256, riders=()):
    rows = q.shape[0]
    r_in, r_out, r_shapes, r_bytes = _rider_specs(riders, batch * MLA_HEADS, lambda b, h: b * MLA_HEADS + h)
    blocks = seq * MLA_QK_PAD * 2 + 3 * seq * LANES * 2 + seq * MLA_V * 2 + r_bytes
    return pl.pallas_call(
        _with_riders(functools.partial(_mla_body, seq=seq, tq=tq, tk=tk), 4, 1, len(riders)),
        out_shape=[jax.ShapeDtypeStruct((rows, MLA_HEADS * MLA_V), BF16)] + r_shapes,
        grid=(batch, MLA_HEADS),
        in_specs=[pl.BlockSpec((seq, MLA_QK_PAD), lambda b, h: (b, h)),
                  pl.BlockSpec((seq, MLA_NOPE), lambda b, h: (b, h)),
                  pl.BlockSpec((seq, LANES), lambda b, h: (b, 0)),
                  pl.BlockSpec((seq, MLA_V), lambda b, h: (b, MLA_HEADS + h))] + r_in,
        out_specs=[pl.BlockSpec((seq, MLA_V), lambda b, h: (b, h))] + r_out,
        compiler_params=_params(("arbitrary", "arbitrary"), _vmem_limit(blocks, temp_bytes=8 << 20)),
        name="mla",
    )(q, kv, kpe, kv, *[w3 for w3, _ in riders])


def _t5_bucket_table():
    iq = np.arange(SWA_BLOCK)
    ik = np.arange(2 * SWA_BLOCK) - SWA_BLOCK
    rel = ik[None, :] - iq[:, None]
    nb = NUM_BUCKETS // 2
    max_exact = nb // 2
    ret = np.where(rel > 0, nb, 0)
    n = np.abs(rel)
    nf = np.maximum(n, 1).astype(np.float64)
    large = max_exact + (np.log(nf / max_exact) / math.log(MAX_DISTANCE / max_exact)
                         * (nb - max_exact)).astype(np.int32)
    large = np.minimum(large, nb - 1)
    return (ret + np.where(n < max_exact, n, large)).astype(np.int32)


def _bias_body(rb_ref, bucket_ref, o_ref):
    first = pl.program_id(0) == 0
    hd = pl.program_id(1)
    bk = bucket_ref[...]
    tab = jnp.zeros(bk.shape, F32)
    for b in range(NUM_BUCKETS):
        tab = jnp.where(bk == b, rb_ref[b, hd] * LOG2E, tab)
    col = lax.broadcasted_iota(jnp.int32, bk.shape, 1)
    qc = lax.broadcasted_iota(jnp.int32, bk.shape, 0) // CHUNK
    kc = col // CHUNK - SWA_BLOCK // CHUNK
    valid = jnp.logical_and(kc <= qc, kc >= qc - WINDOW // CHUNK)
    valid = jnp.logical_and(valid, jnp.logical_not(jnp.logical_and(first, col < SWA_BLOCK)))
    o_ref[...] = jnp.where(valid, tab, NEG)


def _bias_table(rel_bias):
    bucket = jnp.asarray(_t5_bucket_table())
    return pl.pallas_call(
        _bias_body,
        out_shape=jax.ShapeDtypeStruct((2, SWA_HEADS, SWA_BLOCK, 2 * SWA_BLOCK), F32),
        grid=(2, SWA_HEADS),
        in_specs=[pl.BlockSpec(memory_space=pltpu.SMEM),
                  pl.BlockSpec((SWA_BLOCK, 2 * SWA_BLOCK), lambda f, h: (0, 0))],
        out_specs=pl.BlockSpec((None, None, SWA_BLOCK, 2 * SWA_BLOCK), lambda f, h: (f, h, 0, 0)),
        compiler_params=_params(("arbitrary", "arbitrary"), 16 << 20),
        name="swa_bias",
    )(rel_bias, bucket)


def _swa_body(sink_ref, q_ref, kp_ref, kc_ref, vp_ref, vc_ref, bias_ref, o_ref):
    c = SWA_HEAD_DIM ** -0.5 * LOG2E
    blk = SWA_BLOCK
    low = lax.broadcasted_iota(jnp.int32, (2 * blk, LANES), 1) < SWA_HEAD_DIM
    low_q = lax.broadcasted_iota(jnp.int32, (blk, LANES), 1) < SWA_HEAD_DIM

    def both_halves(pair, want_low):
        other = pltpu.roll(pair, SWA_HEAD_DIM, 1)
        return jnp.where(low, pair, other) if want_low else jnp.where(low, other, pair)

    k2, v2 = [], []
    for j in range(SWA_KV_HEADS // 2):
        cols = slice(j * LANES, (j + 1) * LANES)
        kpair = jnp.concatenate([kp_ref[:, cols], kc_ref[:, cols]], axis=0)
        vpair = jnp.concatenate([vp_ref[:, cols], vc_ref[:, cols]], axis=0)
        k2 += [both_halves(kpair, True), both_halves(kpair, False)]
        v2 += [both_halves(vpair, True), both_halves(vpair, False)]

    zero = jnp.zeros((blk, LANES), BF16)
    for p in range(SWA_HEADS // 2):
        g = (2 * p) // SWA_GROUP
        qp = q_ref[:, p * LANES:(p + 1) * LANES]
        qs = jnp.concatenate([jnp.where(low_q, qp, zero), jnp.where(low_q, zero, qp)], axis=0)
        s = lax.dot_general(qs, k2[g], (((1,), (1,)), ((), ())), preferred_element_type=F32)
        t = s * c + bias_ref[2 * p:2 * p + 2].reshape(2 * blk, 2 * blk)
        es, dens = [], []
        for half in range(2):
            th = t[half * blk:(half + 1) * blk]
            sk = sink_ref[0, 2 * p + half] * LOG2E
            m = jnp.maximum(jnp.max(th, axis=-1, keepdims=True), sk)
            e = jnp.exp2(th - m)
            es.append(e.astype(BF16))
            dens.append(jnp.sum(e, axis=-1, keepdims=True) + jnp.exp2(sk - m))
        o2 = jnp.dot(jnp.concatenate(es, axis=0), v2[g], preferred_element_type=F32)
        o_ref[:, p * LANES:(p + 1) * LANES] = jnp.where(
            low_q, o2[:blk] / dens[0], o2[blk:] / dens[1]).astype(o_ref.dtype)


def _swa_attention(proj, sinks, bias_tab, batch, seq, qcol, kcol, vcol, riders=()):
    rows = proj.shape[0]
    nb = seq // SWA_BLOCK
    blk = SWA_BLOCK
    cur = lambda b, n: b * nb + n
    prev = lambda b, n: b * nb + jnp.maximum(n - 1, 0)
    r_in, r_out, r_shapes, r_bytes = _rider_specs(riders, batch * nb, cur)
    blocks = (blk * SWA_Q_WIDTH * 2 + 4 * blk * SWA_KV_WIDTH * 2
              + SWA_HEADS * blk * 2 * blk * 4 + blk * SWA_Q_WIDTH * 2 + r_bytes)
    return pl.pallas_call(
        _with_riders(_swa_body, 7, 1, len(riders)),
        out_shape=[jax.ShapeDtypeStruct((rows, SWA_Q_WIDTH), BF16)] + r_shapes,
        grid=(batch, nb),
        in_specs=[pl.BlockSpec(memory_space=pltpu.SMEM),
                  pl.BlockSpec((blk, SWA_Q_WIDTH), lambda b, n: (cur(b, n), qcol)),
                  pl.BlockSpec((blk, SWA_KV_WIDTH), lambda b, n: (prev(b, n), kcol)),
                  pl.BlockSpec((blk, SWA_KV_WIDTH), lambda b, n: (cur(b, n), kcol)),
                  pl.BlockSpec((blk, SWA_KV_WIDTH), lambda b, n: (prev(b, n), vcol)),
                  pl.BlockSpec((blk, SWA_KV_WIDTH), lambda b, n: (cur(b, n), vcol)),
                  pl.BlockSpec((None, SWA_HEADS, blk, 2 * blk), lambda b, n: (jnp.minimum(n, 1), 0, 0, 0))] + r_in,
        out_specs=[pl.BlockSpec((blk, SWA_Q_WIDTH), lambda b, n: (cur(b, n), 0))] + r_out,
        compiler_params=_params(("arbitrary", "arbitrary"), _vmem_limit(blocks, temp_bytes=8 << 20)),
        name="swa",
    )(sinks, proj, proj, proj, proj, proj, bias_tab, *[w3 for w3, _ in riders])


def _rope_tables(positions):
    half = MLA_ROPE // 2
    inv = ROPE_THETA ** (-jnp.arange(half, dtype=F32) * (2.0 / MLA_ROPE))
    ang = positions.astype(F32).reshape(-1, 1) * inv
    cos, sin = jnp.cos(ang), jnp.sin(ang)
    zero = jnp.zeros_like(cos)
    return (jnp.concatenate([cos, zero, cos, zero], axis=1), jnp.concatenate([-sin, zero, sin, zero], axis=1))


def kernel(x, c, positions, w_ada, b_ada, pre_norm_g, post_norm_g, w_in, q_norm_g, kv_norm_g, w_uq, w_ukv,
           swa_sinks, rel_bias, w_gate, w_proj_a, w_proj_b, w_out, w_ff_up, w_ff_down):
    batch, seq, d = x.shape
    depth = w_ada.shape[0]
    rows = batch * seq
    x2 = x.reshape(rows, d)

    cos_t, sin_t = _rope_tables(positions)
    bias_tab = _bias_table(rel_bias.astype(F32))
    c_pad = jnp.concatenate([c, jnp.zeros((16 - batch, d), c.dtype)], axis=0)

    o_qlat, o_kvlat, o_rope = 0, Q_LORA, Q_LORA + KV_LORA
    o_qs = o_rope + MLA_ROPE
    o_ks = o_qs + SWA_Q_WIDTH
    o_vs = o_ks + SWA_KV_WIDTH

    for l in range(depth):
        wi = w_in[l].astype(BF16)
        w_main = jnp.concatenate([wi[:, o_qs:o_ks], wi[:, o_qlat:o_rope], wi[:, o_ks:o_vs + SWA_KV_WIDTH]],
                                 axis=1)
        w_rope = _spread_rope_columns(wi[:, o_rope:o_qs], 1)
        w_q3 = w_uq[l].astype(BF16).reshape(Q_LORA, MLA_HEADS, MLA_NOPE + MLA_ROPE)
        w_q = jnp.concatenate([w_q3[:, :, :MLA_NOPE], _spread_rope_columns(w_q3[:, :, MLA_NOPE:], 2)],
                              axis=2).reshape(Q_LORA, MLA_HEADS * MLA_QK_PAD)
        w_kv3 = w_ukv[l].reshape(KV_LORA, MLA_HEADS, MLA_NOPE + MLA_V)
        w_kv = jnp.concatenate([w_kv3[:, :, :MLA_NOPE].reshape(KV_LORA, -1),
                                w_kv3[:, :, MLA_NOPE:].reshape(KV_LORA, -1)], axis=1).astype(BF16)
        qcol_s, col_qlat, col_kvlat = 0, SWA_Q_WIDTH // Q_LORA, (SWA_Q_WIDTH + Q_LORA) // KV_LORA
        col_ks = (SWA_Q_WIDTH + Q_LORA + KV_LORA) // SWA_KV_WIDTH
        col_vs = col_ks + 1

        mod = _modulation(c_pad, w_ada[l], b_ada[l].reshape(1, -1))[:batch]
        mod3 = mod.reshape(batch, 6, d)

        proj, h = _produce_matmul([x2], mod3, [pre_norm_g[l, 0].reshape(1, d)], w_main, seq,
                                  produce=_token_mixer_input, keep_lhs=True, tn=512, name="mm_in")
        kpe = _matmul_rope(h, w_rope, cos_t, sin_t)
        q = _matmul_norm(proj, col_qlat, Q_LORA, q_norm_g[l].reshape(1, -1), w_q,
                         rope_tables=(cos_t, sin_t), name="mm_uq")
        kv = _matmul_norm(proj, col_kvlat, KV_LORA, kv_norm_g[l].reshape(1, -1), w_kv, name="mm_ukv")
        att_a, w_gate_b, w_pa_b, w_pb_b = _mla_attention(
            q, kv, kpe, batch, seq, riders=[(w_gate, l), (w_proj_a, l), (w_proj_b, l)])
        att_b, w_out_b = _swa_attention(proj, swa_sinks[l].reshape(1, -1).astype(F32), bias_tab, batch, seq,
                                        qcol_s, col_ks, col_vs, riders=[(w_out, l)])
        mix, w_up_b = _gated_mix(h, att_a, att_b, w_gate_b, w_pa_b, w_pb_b, riders=[(w_ff_up, l)])
        (y,) = _matmul(mix, w_out_b, name="mm_out")
        post_g1 = post_norm_g[l, 0].reshape(1, d)

        u, w_down_b = _produce_matmul([x2, y], mod3, [post_g1, pre_norm_g[l, 1].reshape(1, d)], w_up_b, seq,
                                      produce=_channel_mixer_input, act="relu2", riders=[(w_ff_down, l)],
                                      name="mm_up")
        y2 = _matmul_ksplit(u, w_down_b, tk=4096, name="mm_down")
        x2 = _post2(x2, y, y2, mod3, post_g1, post_norm_g[l, 1].reshape(1, d), seq)

    return x2.reshape(batch, seq, d)
```

```python
import functools
import math

import numpy as np
import jax
import jax.numpy as jnp
from jax import lax
from jax.experimental import pallas as pl
from jax.experimental.pallas import tpu as pltpu

F32 = jnp.float32
BF16 = jnp.bfloat16

D_MODEL = 4096
CHUNK = 64
MLA_HEADS = 16
MLA_NOPE = 128
MLA_ROPE = 64
MLA_V = 128
Q_LORA = 1024
KV_LORA = 512
ROPE_THETA = 10000.0
SWA_HEADS = 32
SWA_KV_HEADS = 4
SWA_HEAD_DIM = 64
SWA_GROUP = SWA_HEADS // SWA_KV_HEADS
WINDOW = 128
SWA_BLOCK = 128
NUM_BUCKETS = 32
MAX_DISTANCE = 128
EPS = 1e-6
NEG = -1e30
LOG2E = math.log2(math.e)

LANES = 128
MXU_COLS = 256
ROW_CHUNK = 16
ROW_UNROLL = 4
MLA_QK_PAD = 256
SWA_Q_WIDTH = SWA_HEADS * SWA_HEAD_DIM
SWA_KV_WIDTH = SWA_KV_HEADS * SWA_HEAD_DIM
VMEM_CAP_BYTES = 56 * 1024 * 1024


def _vmem_limit(block_bytes, scratch_bytes=0, temp_bytes=0):
    need = 2 * block_bytes + scratch_bytes + temp_bytes + (4 << 20)
    return int(min(max(need, 16 << 20), VMEM_CAP_BYTES))


def _params(sem, vmem):
    return pltpu.CompilerParams(dimension_semantics=sem, vmem_limit_bytes=vmem)


def _rms(x):
    return x * lax.rsqrt(jnp.mean(x * x, axis=-1, keepdims=True) + EPS)


def _rope_half_swap(x, cos_t, sin_t):
    return x * cos_t + pltpu.roll(x, LANES // 2, 1) * sin_t


def _spread_rope_columns(w, axis):
    half = MLA_ROPE // 2
    x1, x2 = jnp.split(w, 2, axis=axis)
    zero = jnp.zeros_like(x1)
    assert x1.shape[axis] == half and 4 * half == LANES
    return jnp.concatenate([x1, zero, x2, zero], axis=axis)


def _with_riders(body, n_in, n_out, n_riders):
    if n_riders == 0:
        return body

    def wrapped(*refs):
        ins = refs[:n_in]
        rider_ins = refs[n_in:n_in + n_riders]
        outs = refs[n_in + n_riders:n_in + n_riders + n_out]
        rider_outs = refs[n_in + n_riders + n_out:n_in + 2 * n_riders + n_out]
        scratch = refs[n_in + 2 * n_riders + n_out:]
        for src, dst in zip(rider_ins, rider_outs):
            dst[...] = src[...].astype(dst.dtype)
        body(*ins, *outs, *scratch)

    return wrapped


def _rider_specs(riders, n_steps, step_of):
    in_specs, out_specs, out_shapes, block_bytes = [], [], [], 0
    for w3, layer in riders:
        _, r, cols = w3.shape
        tr = r // n_steps
        assert tr * n_steps == r and tr % 16 == 0, (w3.shape, n_steps)
        in_specs.append(pl.BlockSpec((None, tr, cols), lambda *g, layer=layer: (layer, step_of(*g), 0)))
        out_specs.append(pl.BlockSpec((tr, cols), lambda *g: (step_of(*g), 0)))
        out_shapes.append(jax.ShapeDtypeStruct((r, cols), BF16))
        block_bytes += tr * cols * (4 + 2)
    return in_specs, out_specs, out_shapes, block_bytes


def _regroup_w_in_body(w_ref, main_ref, rope_ref, *, segments, rope_at):
    for dst, src, width in segments:
        main_ref[:, dst:dst + width] = w_ref[:, src:src + width].astype(main_ref.dtype)
    half = MLA_ROPE // 2
    zero = jnp.zeros((w_ref.shape[0], half), rope_ref.dtype)
    x1 = w_ref[:, rope_at:rope_at + half].astype(rope_ref.dtype)
    x2 = w_ref[:, rope_at + half:rope_at + MLA_ROPE].astype(rope_ref.dtype)
    rope_ref[...] = jnp.concatenate([x1, zero, x2, zero], axis=1)


def _regroup_w_in(w_in, layer, segments, rope_at, tr=256):
    _, k, n_in = w_in.shape
    n_out = sum(width for _, _, width in segments)
    blocks = tr * n_in * 4 + tr * n_out * 2 + tr * LANES * 2
    return pl.pallas_call(
        functools.partial(_regroup_w_in_body, segments=segments, rope_at=rope_at),
        out_shape=(jax.ShapeDtypeStruct((k, n_out), BF16), jax.ShapeDtypeStruct((k, LANES), BF16)),
        grid=(k // tr,),
        in_specs=[pl.BlockSpec((None, tr, n_in), lambda i: (layer, i, 0))],
        out_specs=(pl.BlockSpec((tr, n_out), lambda i: (i, 0)), pl.BlockSpec((tr, LANES), lambda i: (i, 0))),
        compiler_params=_params(("arbitrary",), _vmem_limit(blocks, temp_bytes=2 * tr * n_in * 4)),
        name="regroup_w_in",
    )(w_in)


def _mod_body(c_ref, w_ref, b_ref, o_ref):
    c = c_ref[...]
    ca = (c * jax.nn.sigmoid(c)).astype(BF16)
    o_ref[...] = jnp.dot(ca, w_ref[...].astype(BF16), preferred_element_type=F32) + b_ref[...]


def _modulation(c_pad, w_ada, b_ada):
    m, k = c_pad.shape
    n = w_ada.shape[1]
    tn = 512
    blocks = m * k * 4 + k * tn * 4 + tn * 4 + m * tn * 4
    return pl.pallas_call(
        _mod_body,
        out_shape=jax.ShapeDtypeStruct((m, n), F32),
        grid=(n // tn,),
        in_specs=[pl.BlockSpec((m, k), lambda j: (0, 0)),
                  pl.BlockSpec((k, tn), lambda j: (0, j)),
                  pl.BlockSpec((1, tn), lambda j: (0, j))],
        out_specs=pl.BlockSpec((m, tn), lambda j: (0, j)),
        compiler_params=_params(("arbitrary",), _vmem_limit(blocks, temp_bytes=k * tn * 2)),
        name="mod",
    )(c_pad, w_ada, b_ada)


def _token_mixer_input(src, mod_ref, vec, rows):
    (x_ref,), (g_ref,) = src, vec
    return _rms(x_ref[rows, :]) * (g_ref[...] * (1.0 + mod_ref[1:2, :])) + mod_ref[0:1, :]


def _channel_mixer_input(src, mod_ref, vec, rows):
    (x_ref, y_ref), (pg_ref, g_ref) = src, vec
    x1 = x_ref[rows, :] + _rms(y_ref[rows, :].astype(F32)) * (mod_ref[2:3, :] * pg_ref[...])
    return _rms(x1) * (g_ref[...] * (1.0 + mod_ref[4:5, :])) + mod_ref[3:4, :]


def _produce_matmul_body(*refs, n_src, n_vec, produce, keep_lhs, act):
    src, mod_ref = refs[:n_src], refs[n_src]
    vec, w_ref, zero_ref = refs[n_src + 1:n_src + 1 + n_vec], refs[n_src + 1 + n_vec], refs[n_src + 2 + n_vec]
    outs = refs[n_src + 3 + n_vec:]
    o_ref, lhs_out_ref = outs[0], (outs[1] if keep_lhs else None)
    buf_a, buf_b = outs[-2:]
    i = pl.program_id(0)
    j = pl.program_id(1)
    slab = src[0].shape[0]
    n_chunks = slab // ROW_CHUNK
    n_parts = o_ref.shape[1] // MXU_COLS

    def fold_bits(h):
        words = pltpu.bitcast(h, jnp.int32)
        folded = words[:, 0:LANES]
        for k in range(1, words.shape[1] // LANES):
            folded = folded | words[:, k * LANES:(k + 1) * LANES]
        return folded

    def step(build_ref, ready_ref):
        for part in range(n_parts):
            bits = None
            for c in range(part * n_chunks // n_parts, (part + 1) * n_chunks // n_parts):
                rows = slice(c * ROW_CHUNK, (c + 1) * ROW_CHUNK)
                h = produce(src, mod_ref, vec, rows).astype(build_ref.dtype)
                dst = pl.ds(pl.multiple_of(j * slab + c * ROW_CHUNK, ROW_CHUNK), ROW_CHUNK)
                build_ref[dst, :] = h
                if keep_lhs:
                    lhs_out_ref[rows, :] = h
                bits = fold_bits(h) if bits is None else bits | fold_bits(h)
            if ready_ref is None:
                continue
            cols = slice(part * MXU_COLS, (part + 1) * MXU_COLS)
            acc = jnp.dot(ready_ref[...], w_ref[:, cols], preferred_element_type=F32)
            zero = pltpu.bitcast(bits & zero_ref[...], F32)
            top = acc[0:8, :] + jnp.concatenate([zero] * (MXU_COLS // LANES), axis=1)
            acc = jnp.concatenate([top, acc[8:, :]], axis=0)
            if act == "relu2":
                acc = jnp.maximum(acc, 0.0)
                acc = acc * acc
            o_ref[:, cols] = acc.astype(o_ref.dtype)

    @pl.when(i == 0)
    def _():
        step(buf_a, None)

    @pl.when(jnp.logical_and(i > 0, i % 2 == 0))
    def _():
        step(buf_a, buf_b)

    @pl.when(i % 2 == 1)
    def _():
        step(buf_b, buf_a)


def _produce_matmul(src, mod3, vec, w, seq, *, produce, keep_lhs=False, act=None, tm=1024, tn=1024,
                    riders=(), name):
    m, d = src[0].shape
    n = w.shape[1]
    gm, gn = m // tm, n // tn
    slab = tm // gn
    assert slab * gn == tm and slab % ROW_CHUNK == 0 and seq % tm == 0
    build_tile = lambda i: jnp.minimum(i, gm - 1)
    ready_tile = lambda i: jnp.maximum(i - 1, 0)
    ready_col = lambda i, j: jnp.where(i == 0, 0, j)
    slab_of = lambda i, j: jnp.where(i < gm, i * gn + j, gm * gn - 1)
    slab_spec = pl.BlockSpec((slab, d), lambda i, j: (slab_of(i, j), 0))
    vec_spec = pl.BlockSpec((1, d), lambda i, j: (0, 0))
    r_in, r_out, r_shapes, r_bytes = _rider_specs(
        riders, gm * gn, lambda i, j: ready_tile(i) * gn + ready_col(i, j))
    src_bytes = sum(slab * d * a.dtype.itemsize for a in src)
    blocks = src_bytes + (6 + len(vec)) * d * 4 + d * tn * 2 + tm * tn * 2 + keep_lhs * slab * d * 2 + r_bytes
    body = functools.partial(_produce_matmul_body, n_src=len(src), n_vec=len(vec), produce=produce,
                             keep_lhs=keep_lhs, act=act)
    zero_words = jnp.zeros((8, LANES), jnp.int32)
    return pl.pallas_call(
        _with_riders(body, len(src) + len(vec) + 3, 1 + keep_lhs, len(riders)),
        out_shape=([jax.ShapeDtypeStruct((m, n), BF16)] + [jax.ShapeDtypeStruct((m, d), BF16)] * keep_lhs
                   + r_shapes),
        grid=(gm + 1, gn),
        in_specs=([slab_spec] * len(src)
                  + [pl.BlockSpec((None, 6, d), lambda i, j: (build_tile(i) * tm // seq, 0, 0))]
                  + [vec_spec] * len(vec)
                  + [pl.BlockSpec((d, tn), lambda i, j: (0, ready_col(i, j))),
                     pl.BlockSpec((8, LANES), lambda i, j: (0, 0))] + r_in),
        out_specs=([pl.BlockSpec((tm, tn), lambda i, j: (ready_tile(i), ready_col(i, j)))]
                   + [slab_spec] * keep_lhs + r_out),
        scratch_shapes=[pltpu.VMEM((tm, d), BF16), pltpu.VMEM((tm, d), BF16)],
        compiler_params=_params(("arbitrary", "arbitrary"),
                                _vmem_limit(blocks, scratch_bytes=2 * tm * d * 2, temp_bytes=2 * tm * tn * 4)),
        name=name,
    )(*src, mod3, *vec, w, zero_words, *[w3 for w3, _ in riders])


def _post2_body(x_ref, y_ref, y2_ref, mod_ref, pg1_ref, pg2_ref, o_ref):
    def chunk(r, carry):
        rows = pl.ds(pl.multiple_of(r * ROW_CHUNK, ROW_CHUNK), ROW_CHUNK)
        x1 = x_ref[rows, :] + _rms(y_ref[rows, :].astype(F32)) * (mod_ref[2:3, :] * pg1_ref[...])
        o_ref[rows, :] = x1 + _rms(y2_ref[rows, :].astype(F32)) * (mod_ref[5:6, :] * pg2_ref[...])
        return carry

    lax.fori_loop(0, x_ref.shape[0] // ROW_CHUNK, chunk, 0, unroll=ROW_UNROLL)


def _post2(x2, y, y2, mod3, post_g1, post_g2, seq, tr=256):
    rows, d = x2.shape
    per_b = seq // tr
    blocks = tr * d * (4 + 2 + 2 + 4) + 8 * d * 4
    row = pl.BlockSpec((tr, d), lambda i: (i, 0))
    vec = pl.BlockSpec((1, d), lambda i: (0, 0))
    return pl.pallas_call(
        _post2_body,
        out_shape=jax.ShapeDtypeStruct((rows, d), F32),
        grid=(rows // tr,),
        in_specs=[row, row, row, pl.BlockSpec((None, 6, d), lambda i: (i // per_b, 0, 0)), vec, vec],
        out_specs=row,
        compiler_params=_params(("arbitrary",), _vmem_limit(blocks, temp_bytes=4 * tr * d * 4)),
        name="post2",
    )(x2, y, y2, mod3, post_g1, post_g2)


def _mm_body(x_ref, w_ref, o_ref):
    o_ref[...] = jnp.dot(x_ref[...], w_ref[...], preferred_element_type=F32).astype(o_ref.dtype)


def _matmul(x, w, *, tm=1024, tn=1024, riders=(), name):
    m, k = x.shape
    n = w.shape[1]
    tn = min(tn, n)
    gm, gn = m // tm, n // tn
    r_in, r_out, r_shapes, r_bytes = _rider_specs(riders, gm * gn, lambda i, j: i * gn + j)
    blocks = tm * k * 2 + k * tn * 2 + tm * tn * 2 + r_bytes
    return pl.pallas_call(
        _with_riders(_mm_body, 2, 1, len(riders)),
        out_shape=[jax.ShapeDtypeStruct((m, n), BF16)] + r_shapes,
        grid=(gm, gn),
        in_specs=[pl.BlockSpec((tm, k), lambda i, j: (i, 0)),
                  pl.BlockSpec((k, tn), lambda i, j: (0, j))] + r_in,
        out_specs=[pl.BlockSpec((tm, tn), lambda i, j: (i, j))] + r_out,
        compiler_params=_params(("arbitrary", "arbitrary"), _vmem_limit(blocks, temp_bytes=2 * tm * tn * 4)),
        name=name,
    )(x, w, *[w3 for w3, _ in riders])


def _mm_rope_body(x_ref, w_ref, cos_ref, sin_ref, o_ref):
    acc = jnp.dot(x_ref[...], w_ref[...], preferred_element_type=F32)
    o_ref[...] = _rope_half_swap(acc, cos_ref[...], sin_ref[...]).astype(o_ref.dtype)


def _matmul_rope(x, w, cos_t, sin_t, *, tm=1024):
    m, k = x.shape
    n = w.shape[1]
    blocks = tm * k * 2 + k * n * 2 + 2 * tm * n * 4 + tm * n * 2
    return pl.pallas_call(
        _mm_rope_body,
        out_shape=jax.ShapeDtypeStruct((m, n), BF16),
        grid=(m // tm,),
        in_specs=[pl.BlockSpec((tm, k), lambda i: (i, 0)),
                  pl.BlockSpec((k, n), lambda i: (0, 0)),
                  pl.BlockSpec((tm, n), lambda i: (i, 0)),
                  pl.BlockSpec((tm, n), lambda i: (i, 0))],
        out_specs=pl.BlockSpec((tm, n), lambda i: (i, 0)),
        compiler_params=_params(("arbitrary",), _vmem_limit(blocks, temp_bytes=4 * tm * n * 4)),
        name="mm_krope",
    )(x, w, cos_t, sin_t)


def _mm_norm_body(x_ref, g_ref, w_ref, *rest, rope):
    xn_ref = rest[-1]

    @pl.when(pl.program_id(1) == 0)
    def _():
        xn_ref[...] = (_rms(x_ref[...].astype(F32)) * g_ref[...]).astype(BF16)

    acc = jnp.dot(xn_ref[...], w_ref[...], preferred_element_type=F32)
    if not rope:
        o_ref = rest[0]
        o_ref[...] = acc.astype(o_ref.dtype)
        return
    cos_ref, sin_ref, o_ref = rest[:3]
    cos_t = cos_ref[...]
    sin_t = sin_ref[...]
    for hd in range(acc.shape[1] // MLA_QK_PAD):
        c0 = hd * MLA_QK_PAD
        o_ref[:, c0:c0 + MLA_NOPE] = acc[:, c0:c0 + MLA_NOPE].astype(o_ref.dtype)
        pe = _rope_half_swap(acc[:, c0 + MLA_NOPE:c0 + MLA_QK_PAD], cos_t, sin_t)
        o_ref[:, c0 + MLA_NOPE:c0 + MLA_QK_PAD] = pe.astype(o_ref.dtype)


def _matmul_norm(x, xcol, kdim, g, w, *, rope_tables=None, tm=1024, tn=1024, name):
    m = x.shape[0]
    n = w.shape[1]
    blocks = tm * kdim * 2 + kdim * 4 + kdim * tn * 2 + tm * tn * 2
    in_specs = [pl.BlockSpec((tm, kdim), lambda i, j: (i, xcol)),
                pl.BlockSpec((1, kdim), lambda i, j: (0, 0)),
                pl.BlockSpec((kdim, tn), lambda i, j: (0, j))]
    args = [x, g, w]
    if rope_tables is not None:
        in_specs += [pl.BlockSpec((tm, LANES), lambda i, j: (i, 0))] * 2
        args += list(rope_tables)
        blocks += 2 * tm * LANES * 4
    return pl.pallas_call(
        functools.partial(_mm_norm_body, rope=rope_tables is not None),
        out_shape=jax.ShapeDtypeStruct((m, n), BF16),
        grid=(m // tm, n // tn),
        in_specs=in_specs,
        out_specs=pl.BlockSpec((tm, tn), lambda i, j: (i, j)),
        scratch_shapes=[pltpu.VMEM((tm, kdim), BF16)],
        compiler_params=_params(("arbitrary", "arbitrary"),
                                _vmem_limit(blocks, scratch_bytes=tm * kdim * 2,
                                            temp_bytes=2 * tm * tn * 4 + 3 * tm * kdim * 4)),
        name=name,
    )(*args)


def _mm_acc_body(x_ref, w_ref, o_ref, acc_ref):
    kk = pl.program_id(2)

    @pl.when(kk == 0)
    def _():
        acc_ref[...] = jnp.zeros_like(acc_ref)

    acc_ref[...] += jnp.dot(x_ref[...], w_ref[...], preferred_element_type=F32)

    @pl.when(kk == pl.num_programs(2) - 1)
    def _():
        o_ref[...] = acc_ref[...].astype(o_ref.dtype)


def _matmul_ksplit(x, w, *, tm=1024, tn=1024, tk=2048, name):
    m, k = x.shape
    n = w.shape[1]
    blocks = tm * tk * 2 + tk * tn * 2 + tm * tn * 2
    return pl.pallas_call(
        _mm_acc_body,
        out_shape=jax.ShapeDtypeStruct((m, n), BF16),
        grid=(m // tm, n // tn, k // tk),
        in_specs=[pl.BlockSpec((tm, tk), lambda i, j, kk: (i, kk)),
                  pl.BlockSpec((tk, tn), lambda i, j, kk: (kk, j))],
        out_specs=pl.BlockSpec((tm, tn), lambda i, j, kk: (i, j)),
        scratch_shapes=[pltpu.VMEM((tm, tn), F32)],
        compiler_params=_params(("arbitrary", "arbitrary", "arbitrary"),
                                _vmem_limit(blocks, scratch_bytes=tm * tn * 4, temp_bytes=2 * tm * tn * 4)),
        name=name,
    )(x, w)


def _mix_body(h_ref, a_ref, b_ref, wga_ref, wgb_ref, wa_ref, wb_ref, o_ref):
    h = h_ref[...]
    ga = jax.nn.sigmoid(jnp.dot(h, wga_ref[...], preferred_element_type=F32))
    t = ga * jnp.dot(a_ref[...], wa_ref[...], preferred_element_type=F32)
    gb = jax.nn.sigmoid(jnp.dot(h, wgb_ref[...], preferred_element_type=F32))
    t = t + gb * jnp.dot(b_ref[...], wb_ref[...], preferred_element_type=F32)
    o_ref[...] = t.astype(o_ref.dtype)


def _gated_mix(h, a, b, w_gate, w_pa, w_pb, *, tm=512, tn=512, riders=()):
    m, d = h.shape
    ka = a.shape[1]
    kb = b.shape[1]
    n = w_pa.shape[1]
    gm, nb = m // tm, n // tn
    r_in, r_out, r_shapes, r_bytes = _rider_specs(riders, gm * nb, lambda i, j: i * nb + j)
    blocks = tm * (d + ka + kb) * 2 + (2 * d + ka + kb) * tn * 2 + tm * tn * 2 + r_bytes
    return pl.pallas_call(
        _with_riders(_mix_body, 7, 1, len(riders)),
        out_shape=[jax.ShapeDtypeStruct((m, n), BF16)] + r_shapes,
        grid=(gm, nb),
        in_specs=[pl.BlockSpec((tm, d), lambda i, j: (i, 0)),
                  pl.BlockSpec((tm, ka), lambda i, j: (i, 0)),
                  pl.BlockSpec((tm, kb), lambda i, j: (i, 0)),
                  pl.BlockSpec((d, tn), lambda i, j: (0, j)),
                  pl.BlockSpec((d, tn), lambda i, j: (0, j + nb)),
                  pl.BlockSpec((ka, tn), lambda i, j: (0, j)),
                  pl.BlockSpec((kb, tn), lambda i, j: (0, j))] + r_in,
        out_specs=[pl.BlockSpec((tm, tn), lambda i, j: (i, j))] + r_out,
        compiler_params=_params(("arbitrary", "arbitrary"), _vmem_limit(blocks, temp_bytes=4 * tm * tn * 4)),
        name="mix",
    )(h, a, b, w_gate, w_gate, w_pa, w_pb, *[w3 for w3, _ in riders])


def _mla_body(q_ref, kn_ref, kp_ref, v_ref, o_ref, *, seq, tq, tk):
    c = (MLA_NOPE + MLA_ROPE) ** -0.5 * math.log2(math.e)
    ones = jnp.ones((tk, LANES), BF16)
    row = lax.broadcasted_iota(jnp.int32, (tq, tk), 0)
    col = lax.broadcasted_iota(jnp.int32, (tq, tk), 1)
    for qi in range(seq // tq):
        q = q_ref[qi * tq:(qi + 1) * tq, :]
        n_kb = -(-((qi + 1) * tq) // tk)
        m = None
        acc = None
        for kj in range(n_kb):
            ks = slice(kj * tk, (kj + 1) * tk)
            kf = jnp.concatenate([kn_ref[ks, :], kp_ref[ks, :]], axis=1)
            s = lax.dot_general(q, kf, (((1,), (1,)), ((), ())), preferred_element_type=F32)
            if (kj + 1) * tk > qi * tq + CHUNK:
                visible = (col + kj * tk) // CHUNK <= (row + qi * tq) // CHUNK
                s = jnp.where(visible, s, NEG)
            bm = jnp.max(s, axis=-1, keepdims=True)
            v_ext = jnp.concatenate([v_ref[ks, :], ones], axis=1)
            if m is None:
                m = bm
                p = jnp.exp2((s - m) * c)
                acc = jnp.dot(p.astype(BF16), v_ext, preferred_element_type=F32)
            else:
                m_new = jnp.maximum(m, bm)
                alpha = jnp.exp2((m - m_new) * c)
                p = jnp.exp2((s - m_new) * c)
                acc = alpha * acc + jnp.dot(p.astype(BF16), v_ext, preferred_element_type=F32)
                m = m_new
        o_ref[qi * tq:(qi + 1) * tq, :] = (acc[:, :MLA_V] / acc[:, MLA_V:]).astype(o_ref.dtype)


def _mla_attention(q, kv, kpe, batch, seq, tq=256, tk=256, riders=()):
    rows = q.shape[0]
    r_in, r_out, r_shapes, r_bytes = _rider_specs(riders, batch * MLA_HEADS, lambda b, h: b * MLA_HEADS + h)
    blocks = seq * MLA_QK_PAD * 2 + 3 * seq * LANES * 2 + seq * MLA_V * 2 + r_bytes
    return pl.pallas_call(
        _with_riders(functools.partial(_mla_body, seq=seq, tq=tq, tk=tk), 4, 1, len(riders)),
        out_shape=[jax.ShapeDtypeStruct((rows, MLA_HEADS * MLA_V), BF16)] + r_shapes,
        grid=(batch, MLA_HEADS),
        in_specs=[pl.BlockSpec((seq, MLA_QK_PAD), lambda b, h: (b, h)),
                  pl.BlockSpec((seq, MLA_NOPE), lambda b, h: (b, h)),
                  pl.BlockSpec((seq, LANES), lambda b, h: (b, 0)),
                  pl.BlockSpec((seq, MLA_V), lambda b, h: (b, MLA_HEADS + h))] + r_in,
        out_specs=[pl.BlockSpec((seq, MLA_V), lambda b, h: (b, h))] + r_out,
        compiler_params=_params(("arbitrary", "arbitrary"), _vmem_limit(blocks, temp_bytes=8 << 20)),
        name="mla",
    )(q, kv, kpe, kv, *[w3 for w3, _ in riders])


def _t5_bucket_table():
    iq = np.arange(SWA_BLOCK)
    ik = np.arange(2 * SWA_BLOCK) - SWA_BLOCK
    rel = ik[None, :] - iq[:, None]
    nb = NUM_BUCKETS // 2
    max_exact = nb // 2
    ret = np.where(rel > 0, nb, 0)
    n = np.abs(rel)
    nf = np.maximum(n, 1).astype(np.float64)
    large = max_exact + (np.log(nf / max_exact) / math.log(MAX_DISTANCE / max_exact)
                         * (nb - max_exact)).astype(np.int32)
    large = np.minimum(large, nb - 1)
    return (ret + np.where(n < max_exact, n, large)).astype(np.int32)


def _bias_body(rb_ref, bucket_ref, o_ref):
    first = pl.program_id(0) == 0
    hd = pl.program_id(1)
    bk = bucket_ref[...]
    tab = jnp.zeros(bk.shape, F32)
    for b in range(NUM_BUCKETS):
        tab = jnp.where(bk == b, rb_ref[b, hd] * LOG2E, tab)
    col = lax.broadcasted_iota(jnp.int32, bk.shape, 1)
    qc = lax.broadcasted_iota(jnp.int32, bk.shape, 0) // CHUNK
    kc = col // CHUNK - SWA_BLOCK // CHUNK
    valid = jnp.logical_and(kc <= qc, kc >= qc - WINDOW // CHUNK)
    valid = jnp.logical_and(valid, jnp.logical_not(jnp.logical_and(first, col < SWA_BLOCK)))
    o_ref[...] = jnp.where(valid, tab, NEG)


def _bias_table(rel_bias):
    bucket = jnp.asarray(_t5_bucket_table())
    return pl.pallas_call(
        _bias_body,
        out_shape=jax.ShapeDtypeStruct((2, SWA_HEADS, SWA_BLOCK, 2 * SWA_BLOCK), F32),
        grid=(2, SWA_HEADS),
        in_specs=[pl.BlockSpec(memory_space=pltpu.SMEM),
                  pl.BlockSpec((SWA_BLOCK, 2 * SWA_BLOCK), lambda f, h: (0, 0))],
        out_specs=pl.BlockSpec((None, None, SWA_BLOCK, 2 * SWA_BLOCK), lambda f, h: (f, h, 0, 0)),
        compiler_params=_params(("arbitrary", "arbitrary"), 16 << 20),
        name="swa_bias",
    )(rel_bias, bucket)


def _swa_body(sink_ref, q_ref, kp_ref, kc_ref, vp_ref, vc_ref, bias_ref, o_ref):
    c = SWA_HEAD_DIM ** -0.5 * LOG2E
    blk = SWA_BLOCK
    low = lax.broadcasted_iota(jnp.int32, (2 * blk, LANES), 1) < SWA_HEAD_DIM
    low_q = lax.broadcasted_iota(jnp.int32, (blk, LANES), 1) < SWA_HEAD_DIM

    def both_halves(pair, want_low):
        other = pltpu.roll(pair, SWA_HEAD_DIM, 1)
        return jnp.where(low, pair, other) if want_low else jnp.where(low, other, pair)

    k2, v2 = [], []
    for j in range(SWA_KV_HEADS // 2):
        cols = slice(j * LANES, (j + 1) * LANES)
        kpair = jnp.concatenate([kp_ref[:, cols], kc_ref[:, cols]], axis=0)
        vpair = jnp.concatenate([vp_ref[:, cols], vc_ref[:, cols]], axis=0)
        k2 += [both_halves(kpair, True), both_halves(kpair, False)]
        v2 += [both_halves(vpair, True), both_halves(vpair, False)]

    zero = jnp.zeros((blk, LANES), BF16)
    for p in range(SWA_HEADS // 2):
        g = (2 * p) // SWA_GROUP
        qp = q_ref[:, p * LANES:(p + 1) * LANES]
        qs = jnp.concatenate([jnp.where(low_q, qp, zero), jnp.where(low_q, zero, qp)], axis=0)
        s = lax.dot_general(qs, k2[g], (((1,), (1,)), ((), ())), preferred_element_type=F32)
        t = s * c + bias_ref[2 * p:2 * p + 2].reshape(2 * blk, 2 * blk)
        es, dens = [], []
        for half in range(2):
            th = t[half * blk:(half + 1) * blk]
            sk = sink_ref[0, 2 * p + half] * LOG2E
            m = jnp.maximum(jnp.max(th, axis=-1, keepdims=True), sk)
            e = jnp.exp2(th - m)
            es.append(e.astype(BF16))
            dens.append(jnp.sum(e, axis=-1, keepdims=True) + jnp.exp2(sk - m))
        o2 = jnp.dot(jnp.concatenate(es, axis=0), v2[g], preferred_element_type=F32)
        o_ref[:, p * LANES:(p + 1) * LANES] = jnp.where(
            low_q, o2[:blk] / dens[0], o2[blk:] / dens[1]).astype(o_ref.dtype)


def _swa_attention(proj, sinks, bias_tab, batch, seq, qcol, kcol, vcol, riders=()):
    rows = proj.shape[0]
    nb = seq // SWA_BLOCK
    blk = SWA_BLOCK
    cur = lambda b, n: b * nb + n
    prev = lambda b, n: b * nb + jnp.maximum(n - 1, 0)
    r_in, r_out, r_shapes, r_bytes = _rider_specs(riders, batch * nb, cur)
    blocks = (blk * SWA_Q_WIDTH * 2 + 4 * blk * SWA_KV_WIDTH * 2
              + SWA_HEADS * blk * 2 * blk * 4 + blk * SWA_Q_WIDTH * 2 + r_bytes)
    return pl.pallas_call(
        _with_riders(_swa_body, 7, 1, len(riders)),
        out_shape=[jax.ShapeDtypeStruct((rows, SWA_Q_WIDTH), BF16)] + r_shapes,
        grid=(batch, nb),
        in_specs=[pl.BlockSpec(memory_space=pltpu.SMEM),
                  pl.BlockSpec((blk, SWA_Q_WIDTH), lambda b, n: (cur(b, n), qcol)),
                  pl.BlockSpec((blk, SWA_KV_WIDTH), lambda b, n: (prev(b, n), kcol)),
                  pl.BlockSpec((blk, SWA_KV_WIDTH), lambda b, n: (cur(b, n), kcol)),
                  pl.BlockSpec((blk, SWA_KV_WIDTH), lambda b, n: (prev(b, n), vcol)),
                  pl.BlockSpec((blk, SWA_KV_WIDTH), lambda b, n: (cur(b, n), vcol)),
                  pl.BlockSpec((None, SWA_HEADS, blk, 2 * blk), lambda b, n: (jnp.minimum(n, 1), 0, 0, 0))] + r_in,
        out_specs=[pl.BlockSpec((blk, SWA_Q_WIDTH), lambda b, n: (cur(b, n), 0))] + r_out,
        compiler_params=_params(("arbitrary", "arbitrary"), _vmem_limit(blocks, temp_bytes=8 << 20)),
        name="swa",
    )(sinks, proj, proj, proj, proj, proj, bias_tab, *[w3 for w3, _ in riders])


def _rope_tables(positions):
    half = MLA_ROPE // 2
    inv = ROPE_THETA ** (-jnp.arange(half, dtype=F32) * (2.0 / MLA_ROPE))
    ang = positions.astype(F32).reshape(-1, 1) * inv
    cos, sin = jnp.cos(ang), jnp.sin(ang)
    zero = jnp.zeros_like(cos)
    return (jnp.concatenate([cos, zero, cos, zero], axis=1), jnp.concatenate([-sin, zero, sin, zero], axis=1))


def kernel(x, c, positions, w_ada, b_ada, pre_norm_g, post_norm_g, w_in, q_norm_g, kv_norm_g, w_uq, w_ukv,
           swa_sinks, rel_bias, w_gate, w_proj_a, w_proj_b, w_out, w_ff_up, w_ff_down):
    batch, seq, d = x.shape
    depth = w_ada.shape[0]
    rows = batch * seq
    x2 = x.reshape(rows, d)

    cos_t, sin_t = _rope_tables(positions)
    bias_tab = _bias_table(rel_bias.astype(F32))
    c_pad = jnp.concatenate([c, jnp.zeros((16 - batch, d), c.dtype)], axis=0)

    o_qlat, o_kvlat, o_rope = 0, Q_LORA, Q_LORA + KV_LORA
    o_qs = o_rope + MLA_ROPE
    o_ks = o_qs + SWA_Q_WIDTH
    o_vs = o_ks + SWA_KV_WIDTH

    for l in range(depth):
        w_main, w_rope = _regroup_w_in(
            w_in, l, ((0, o_qs, SWA_Q_WIDTH), (SWA_Q_WIDTH, o_qlat, o_rope - o_qlat),
                      (SWA_Q_WIDTH + o_rope - o_qlat, o_ks, 2 * SWA_KV_WIDTH)), o_rope)
        w_q3 = w_uq[l].astype(BF16).reshape(Q_LORA, MLA_HEADS, MLA_NOPE + MLA_ROPE)
        w_q = jnp.concatenate([w_q3[:, :, :MLA_NOPE], _spread_rope_columns(w_q3[:, :, MLA_NOPE:], 2)],
                              axis=2).reshape(Q_LORA, MLA_HEADS * MLA_QK_PAD)
        w_kv3 = w_ukv[l].reshape(KV_LORA, MLA_HEADS, MLA_NOPE + MLA_V)
        w_kv = jnp.concatenate([w_kv3[:, :, :MLA_NOPE].reshape(KV_LORA, -1),
                                w_kv3[:, :, MLA_NOPE:].reshape(KV_LORA, -1)], axis=1).astype(BF16)
        qcol_s, col_qlat, col_kvlat = 0, SWA_Q_WIDTH // Q_LORA, (SWA_Q_WIDTH + Q_LORA) // KV_LORA
        col_ks = (SWA_Q_WIDTH + Q_LORA + KV_LORA) // SWA_KV_WIDTH
        col_vs = col_ks + 1

        mod = _modulation(c_pad, w_ada[l], b_ada[l].reshape(1, -1))[:batch]
        mod3 = mod.reshape(batch, 6, d)

        proj, h = _produce_matmul([x2], mod3, [pre_norm_g[l, 0].reshape(1, d)], w_main, seq,
                                  produce=_token_mixer_input, keep_lhs=True, tn=512, name="mm_in")
        kpe = _matmul_rope(h, w_rope, cos_t, sin_t)
        q = _matmul_norm(proj, col_qlat, Q_LORA, q_norm_g[l].reshape(1, -1), w_q,
                         rope_tables=(cos_t, sin_t), name="mm_uq")
        kv = _matmul_norm(proj, col_kvlat, KV_LORA, kv_norm_g[l].reshape(1, -1), w_kv, name="mm_ukv")
        att_a, w_gate_b, w_pa_b, w_pb_b = _mla_attention(
            q, kv, kpe, batch, seq, riders=[(w_gate, l), (w_proj_a, l), (w_proj_b, l)])
        att_b, w_out_b = _swa_attention(proj, swa_sinks[l].reshape(1, -1).astype(F32), bias_tab, batch, seq,
                                        qcol_s, col_ks, col_vs, riders=[(w_out, l)])
        mix, w_up_b = _gated_mix(h, att_a, att_b, w_gate_b, w_pa_b, w_pb_b, riders=[(w_ff_up, l)])
        (y,) = _matmul(mix, w_out_b, name="mm_out")
        post_g1 = post_norm_g[l, 0].reshape(1, d)

        u, w_down_b = _produce_matmul([x2, y], mod3, [post_g1, pre_norm_g[l, 1].reshape(1, d)], w_up_b, seq,
                                      produce=_channel_mixer_input, act="relu2", riders=[(w_ff_down, l)],
                                      name="mm_up")
        y2 = _matmul_ksplit(u, w_down_b, tk=4096, name="mm_down")
        x2 = _post2(x2, y, y2, mod3, post_g1, post_norm_g[l, 1].reshape(1, d), seq)

    return x2.reshape(batch, seq, d)
```

```python
import functools
import math

import numpy as np
import jax
import jax.numpy as jnp
from jax import lax
from jax.experimental import pallas as pl
from jax.experimental.pallas import tpu as pltpu

F32 = jnp.float32
BF16 = jnp.bfloat16

D_MODEL = 4096
CHUNK = 64
MLA_HEADS = 16
MLA_NOPE = 128
MLA_ROPE = 64
MLA_V = 128
Q_LORA = 1024
KV_LORA = 512
ROPE_THETA = 10000.0
SWA_HEADS = 32
SWA_KV_HEADS = 4
SWA_HEAD_DIM = 64
SWA_GROUP = SWA_HEADS // SWA_KV_HEADS
WINDOW = 128
SWA_BLOCK = 128
NUM_BUCKETS = 32
MAX_DISTANCE = 128
EPS = 1e-6
NEG = -1e30
LOG2E = math.log2(math.e)

LANES = 128
ROW_CHUNK = 16
ROW_UNROLL = 4
MLA_QK_PAD = 256
SWA_Q_WIDTH = SWA_HEADS * SWA_HEAD_DIM
SWA_KV_WIDTH = SWA_KV_HEADS * SWA_HEAD_DIM
VMEM_CAP_BYTES = 56 * 1024 * 1024


def _vmem_limit(block_bytes, scratch_bytes=0, temp_bytes=0):
    need = 2 * block_bytes + scratch_bytes + temp_bytes + (4 << 20)
    return int(min(max(need, 16 << 20), VMEM_CAP_BYTES))


def _params(sem, vmem):
    return pltpu.CompilerParams(dimension_semantics=sem, vmem_limit_bytes=vmem)


def _rms(x):
    return x * lax.rsqrt(jnp.mean(x * x, axis=-1, keepdims=True) + EPS)


def _rope_half_swap(x, cos_t, sin_t):
    return x * cos_t + pltpu.roll(x, LANES // 2, 1) * sin_t


def _spread_rope_columns(w, axis):
    half = MLA_ROPE // 2
    x1, x2 = jnp.split(w, 2, axis=axis)
    zero = jnp.zeros_like(x1)
    assert x1.shape[axis] == half and 4 * half == LANES
    return jnp.concatenate([x1, zero, x2, zero], axis=axis)


def _with_riders(body, n_in, n_out, n_riders):
    if n_riders == 0:
        return body

    def wrapped(*refs):
        ins = refs[:n_in]
        rider_ins = refs[n_in:n_in + n_riders]
        outs = refs[n_in + n_riders:n_in + n_riders + n_out]
        rider_outs = refs[n_in + n_riders + n_out:n_in + 2 * n_riders + n_out]
        scratch = refs[n_in + 2 * n_riders + n_out:]
        for src, dst in zip(rider_ins, rider_outs):
            dst[...] = src[...].astype(dst.dtype)
        body(*ins, *outs, *scratch)

    return wrapped


def _rider_specs(riders, n_steps, step_of):
    in_specs, out_specs, out_shapes, block_bytes = [], [], [], 0
    for w3, layer in riders:
        _, r, cols = w3.shape
        tr = r // n_steps
        assert tr * n_steps == r and tr % 16 == 0, (w3.shape, n_steps)
        in_specs.append(pl.BlockSpec((None, tr, cols), lambda *g, layer=layer: (layer, step_of(*g), 0)))
        out_specs.append(pl.BlockSpec((tr, cols), lambda *g: (step_of(*g), 0)))
        out_shapes.append(jax.ShapeDtypeStruct((r, cols), BF16))
        block_bytes += tr * cols * (4 + 2)
    return in_specs, out_specs, out_shapes, block_bytes


def _mod_body(c_ref, w_ref, b_ref, o_ref):
    c = c_ref[...]
    ca = (c * jax.nn.sigmoid(c)).astype(BF16)
    o_ref[...] = jnp.dot(ca, w_ref[...].astype(BF16), preferred_element_type=F32) + b_ref[...]


def _modulation(c_pad, w_ada, b_ada):
    m, k = c_pad.shape
    n = w_ada.shape[1]
    tn = 512
    blocks = m * k * 4 + k * tn * 4 + tn * 4 + m * tn * 4
    return pl.pallas_call(
        _mod_body,
        out_shape=jax.ShapeDtypeStruct((m, n), F32),
        grid=(n // tn,),
        in_specs=[pl.BlockSpec((m, k), lambda j: (0, 0)),
                  pl.BlockSpec((k, tn), lambda j: (0, j)),
                  pl.BlockSpec((1, tn), lambda j: (0, j))],
        out_specs=pl.BlockSpec((m, tn), lambda j: (0, j)),
        compiler_params=_params(("arbitrary",), _vmem_limit(blocks, temp_bytes=k * tn * 2)),
        name="mod",
    )(c_pad, w_ada, b_ada)


def _token_mixer_input(src, mod_ref, vec, rows):
    (x_ref,), (g_ref,) = src, vec
    return _rms(x_ref[rows, :]) * (g_ref[...] * (1.0 + mod_ref[1:2, :])) + mod_ref[0:1, :]


def _channel_mixer_input(src, mod_ref, vec, rows):
    (x_ref, y_ref), (pg_ref, g_ref) = src, vec
    x1 = x_ref[rows, :] + _rms(y_ref[rows, :].astype(F32)) * (mod_ref[2:3, :] * pg_ref[...])
    return _rms(x1) * (g_ref[...] * (1.0 + mod_ref[4:5, :])) + mod_ref[3:4, :]


def _produce_matmul_body(*refs, n_src, n_vec, produce, keep_lhs, act):
    src, mod_ref = refs[:n_src], refs[n_src]
    vec, w_ref = refs[n_src + 1:n_src + 1 + n_vec], refs[n_src + 1 + n_vec]
    outs = refs[n_src + 2 + n_vec:]
    o_ref, lhs_out_ref = outs[0], (outs[1] if keep_lhs else None)
    buf_a, buf_b = outs[-2:]
    i = pl.program_id(0)
    j = pl.program_id(1)
    slab = src[0].shape[0]

    def step(build_ref, ready_ref):
        for c in range(slab // ROW_CHUNK):
            rows = slice(c * ROW_CHUNK, (c + 1) * ROW_CHUNK)
            h = produce(src, mod_ref, vec, rows).astype(build_ref.dtype)
            dst = pl.ds(pl.multiple_of(j * slab + c * ROW_CHUNK, ROW_CHUNK), ROW_CHUNK)
            build_ref[dst, :] = h
            if keep_lhs:
                lhs_out_ref[rows, :] = h
        if ready_ref is None:
            return
        acc = jnp.dot(ready_ref[...], w_ref[...], preferred_element_type=F32)
        if act == "relu2":
            acc = jnp.maximum(acc, 0.0)
            acc = acc * acc
        o_ref[...] = acc.astype(o_ref.dtype)

    @pl.when(i == 0)
    def _():
        step(buf_a, None)

    @pl.when(jnp.logical_and(i > 0, i % 2 == 0))
    def _():
        step(buf_a, buf_b)

    @pl.when(i % 2 == 1)
    def _():
        step(buf_b, buf_a)


def _produce_matmul(src, mod3, vec, w, seq, *, produce, keep_lhs=False, act=None, tm=1024, tn=1024,
                    riders=(), name):
    m, d = src[0].shape
    n = w.shape[1]
    gm, gn = m // tm, n // tn
    slab = tm // gn
    assert slab * gn == tm and slab % ROW_CHUNK == 0 and seq % tm == 0
    build_tile = lambda i: jnp.minimum(i, gm - 1)
    ready_tile = lambda i: jnp.maximum(i - 1, 0)
    ready_col = lambda i, j: jnp.where(i == 0, 0, j)
    slab_of = lambda i, j: jnp.where(i < gm, i * gn + j, gm * gn - 1)
    slab_spec = pl.BlockSpec((slab, d), lambda i, j: (slab_of(i, j), 0))
    vec_spec = pl.BlockSpec((1, d), lambda i, j: (0, 0))
    r_in, r_out, r_shapes, r_bytes = _rider_specs(
        riders, gm * gn, lambda i, j: ready_tile(i) * gn + ready_col(i, j))
    src_bytes = sum(slab * d * a.dtype.itemsize for a in src)
    blocks = src_bytes + (6 + len(vec)) * d * 4 + d * tn * 2 + tm * tn * 2 + keep_lhs * slab * d * 2 + r_bytes
    body = functools.partial(_produce_matmul_body, n_src=len(src), n_vec=len(vec), produce=produce,
                             keep_lhs=keep_lhs, act=act)
    return pl.pallas_call(
        _with_riders(body, len(src) + len(vec) + 2, 1 + keep_lhs, len(riders)),
        out_shape=([jax.ShapeDtypeStruct((m, n), BF16)] + [jax.ShapeDtypeStruct((m, d), BF16)] * keep_lhs
                   + r_shapes),
        grid=(gm + 1, gn),
        in_specs=([slab_spec] * len(src)
                  + [pl.BlockSpec((None, 6, d), lambda i, j: (build_tile(i) * tm // seq, 0, 0))]
                  + [vec_spec] * len(vec)
                  + [pl.BlockSpec((d, tn), lambda i, j: (0, ready_col(i, j)))] + r_in),
        out_specs=([pl.BlockSpec((tm, tn), lambda i, j: (ready_tile(i), ready_col(i, j)))]
                   + [slab_spec] * keep_lhs + r_out),
        scratch_shapes=[pltpu.VMEM((tm, d), BF16), pltpu.VMEM((tm, d), BF16)],
        compiler_params=_params(("arbitrary", "arbitrary"),
                                _vmem_limit(blocks, scratch_bytes=2 * tm * d * 2, temp_bytes=2 * tm * tn * 4)),
        name=name,
    )(*src, mod3, *vec, w, *[w3 for w3, _ in riders])


def _post2_body(x_ref, y_ref, y2_ref, mod_ref, pg1_ref, pg2_ref, o_ref):
    def chunk(r, carry):
        rows = pl.ds(pl.multiple_of(r * ROW_CHUNK, ROW_CHUNK), ROW_CHUNK)
        x1 = x_ref[rows, :] + _rms(y_ref[rows, :].astype(F32)) * (mod_ref[2:3, :] * pg1_ref[...])
        o_ref[rows, :] = x1 + _rms(y2_ref[rows, :].astype(F32)) * (mod_ref[5:6, :] * pg2_ref[...])
        return carry

    lax.fori_loop(0, x_ref.shape[0] // ROW_CHUNK, chunk, 0, unroll=ROW_UNROLL)


def _post2(x2, y, y2, mod3, post_g1, post_g2, seq, tr=256):
    rows, d = x2.shape
    per_b = seq // tr
    blocks = tr * d * (4 + 2 + 2 + 4) + 8 * d * 4
    row = pl.BlockSpec((tr, d), lambda i: (i, 0))
    vec = pl.BlockSpec((1, d), lambda i: (0, 0))
    return pl.pallas_call(
        _post2_body,
        out_shape=jax.ShapeDtypeStruct((rows, d), F32),
        grid=(rows // tr,),
        in_specs=[row, row, row, pl.BlockSpec((None, 6, d), lambda i: (i // per_b, 0, 0)), vec, vec],
        out_specs=row,
        compiler_params=_params(("arbitrary",), _vmem_limit(blocks, temp_bytes=4 * tr * d * 4)),
        name="post2",
    )(x2, y, y2, mod3, post_g1, post_g2)


def _mm_body(x_ref, w_ref, o_ref):
    o_ref[...] = jnp.dot(x_ref[...], w_ref[...], preferred_element_type=F32).astype(o_ref.dtype)


def _matmul(x, w, *, tm=1024, tn=1024, riders=(), name):
    m, k = x.shape
    n = w.shape[1]
    tn = min(tn, n)
    gm, gn = m // tm, n // tn
    r_in, r_out, r_shapes, r_bytes = _rider_specs(riders, gm * gn, lambda i, j: i * gn + j)
    blocks = tm * k * 2 + k * tn * 2 + tm * tn * 2 + r_bytes
    return pl.pallas_call(
        _with_riders(_mm_body, 2, 1, len(riders)),
        out_shape=[jax.ShapeDtypeStruct((m, n), BF16)] + r_shapes,
        grid=(gm, gn),
        in_specs=[pl.BlockSpec((tm, k), lambda i, j: (i, 0)),
                  pl.BlockSpec((k, tn), lambda i, j: (0, j))] + r_in,
        out_specs=[pl.BlockSpec((tm, tn), lambda i, j: (i, j))] + r_out,
        compiler_params=_params(("arbitrary", "arbitrary"), _vmem_limit(blocks, temp_bytes=2 * tm * tn * 4)),
        name=name,
    )(x, w, *[w3 for w3, _ in riders])


def _mm_rope_body(x_ref, w_ref, cos_ref, sin_ref, o_ref):
    acc = jnp.dot(x_ref[...], w_ref[...], preferred_element_type=F32)
    o_ref[...] = _rope_half_swap(acc, cos_ref[...], sin_ref[...]).astype(o_ref.dtype)


def _matmul_rope(x, w, cos_t, sin_t, *, tm=1024):
    m, k = x.shape
    n = w.shape[1]
    blocks = tm * k * 2 + k * n * 2 + 2 * tm * n * 4 + tm * n * 2
    return pl.pallas_call(
        _mm_rope_body,
        out_shape=jax.ShapeDtypeStruct((m, n), BF16),
        grid=(m // tm,),
        in_specs=[pl.BlockSpec((tm, k), lambda i: (i, 0)),
                  pl.BlockSpec((k, n), lambda i: (0, 0)),
                  pl.BlockSpec((tm, n), lambda i: (i, 0)),
                  pl.BlockSpec((tm, n), lambda i: (i, 0))],
        out_specs=pl.BlockSpec((tm, n), lambda i: (i, 0)),
        compiler_params=_params(("arbitrary",), _vmem_limit(blocks, temp_bytes=4 * tm * n * 4)),
        name="mm_krope",
    )(x, w, cos_t, sin_t)


def _mm_norm_body(x_ref, g_ref, w_ref, *rest, rope):
    xn_ref = rest[-1]

    @pl.when(pl.program_id(1) == 0)
    def _():
        xn_ref[...] = (_rms(x_ref[...].astype(F32)) * g_ref[...]).astype(BF16)

    acc = jnp.dot(xn_ref[...], w_ref[...], preferred_element_type=F32)
    if not rope:
        o_ref = rest[0]
        o_ref[...] = acc.astype(o_ref.dtype)
        return
    cos_ref, sin_ref, o_ref = rest[:3]
    cos_t = cos_ref[...]
    sin_t = sin_ref[...]
    for hd in range(acc.shape[1] // MLA_QK_PAD):
        c0 = hd * MLA_QK_PAD
        o_ref[:, c0:c0 + MLA_NOPE] = acc[:, c0:c0 + MLA_NOPE].astype(o_ref.dtype)
        pe = _rope_half_swap(acc[:, c0 + MLA_NOPE:c0 + MLA_QK_PAD], cos_t, sin_t)
        o_ref[:, c0 + MLA_NOPE:c0 + MLA_QK_PAD] = pe.astype(o_ref.dtype)


def _matmul_norm(x, xcol, kdim, g, w, *, rope_tables=None, tm=1024, tn=1024, name):
    m = x.shape[0]
    n = w.shape[1]
    blocks = tm * kdim * 2 + kdim * 4 + kdim * tn * 2 + tm * tn * 2
    in_specs = [pl.BlockSpec((tm, kdim), lambda i, j: (i, xcol)),
                pl.BlockSpec((1, kdim), lambda i, j: (0, 0)),
                pl.BlockSpec((kdim, tn), lambda i, j: (0, j))]
    args = [x, g, w]
    if rope_tables is not None:
        in_specs += [pl.BlockSpec((tm, LANES), lambda i, j: (i, 0))] * 2
        args += list(rope_tables)
        blocks += 2 * tm * LANES * 4
    return pl.pallas_call(
        functools.partial(_mm_norm_body, rope=rope_tables is not None),
        out_shape=jax.ShapeDtypeStruct((m, n), BF16),
        grid=(m // tm, n // tn),
        in_specs=in_specs,
        out_specs=pl.BlockSpec((tm, tn), lambda i, j: (i, j)),
        scratch_shapes=[pltpu.VMEM((tm, kdim), BF16)],
        compiler_params=_params(("arbitrary", "arbitrary"),
                                _vmem_limit(blocks, scratch_bytes=tm * kdim * 2,
                                            temp_bytes=2 * tm * tn * 4 + 3 * tm * kdim * 4)),
        name=name,
    )(*args)


def _mm_acc_body(x_ref, w_ref, o_ref, acc_ref):
    kk = pl.program_id(2)

    @pl.when(kk == 0)
    def _():
        acc_ref[...] = jnp.zeros_like(acc_ref)

    acc_ref[...] += jnp.dot(x_ref[...], w_ref[...], preferred_element_type=F32)

    @pl.when(kk == pl.num_programs(2) - 1)
    def _():
        o_ref[...] = acc_ref[...].astype(o_ref.dtype)


def _matmul_ksplit(x, w, *, tm=1024, tn=1024, tk=2048, name):
    m, k = x.shape
    n = w.shape[1]
    blocks = tm * tk * 2 + tk * tn * 2 + tm * tn * 2
    return pl.pallas_call(
        _mm_acc_body,
        out_shape=jax.ShapeDtypeStruct((m, n), BF16),
        grid=(m // tm, n // tn, k // tk),
        in_specs=[pl.BlockSpec((tm, tk), lambda i, j, kk: (i, kk)),
                  pl.BlockSpec((tk, tn), lambda i, j, kk: (kk, j))],
        out_specs=pl.BlockSpec((tm, tn), lambda i, j, kk: (i, j)),
        scratch_shapes=[pltpu.VMEM((tm, tn), F32)],
        compiler_params=_params(("arbitrary", "arbitrary", "arbitrary"),
                                _vmem_limit(blocks, scratch_bytes=tm * tn * 4, temp_bytes=2 * tm * tn * 4)),
        name=name,
    )(x, w)


def _mix_body(h_ref, a_ref, b_ref, wga_ref, wgb_ref, wa_ref, wb_ref, o_ref):
    h = h_ref[...]
    ga = jax.nn.sigmoid(jnp.dot(h, wga_ref[...], preferred_element_type=F32))
    t = ga * jnp.dot(a_ref[...], wa_ref[...], preferred_element_type=F32)
    gb = jax.nn.sigmoid(jnp.dot(h, wgb_ref[...], preferred_element_type=F32))
    t = t + gb * jnp.dot(b_ref[...], wb_ref[...], preferred_element_type=F32)
    o_ref[...] = t.astype(o_ref.dtype)


def _gated_mix(h, a, b, w_gate, w_pa, w_pb, *, tm=512, tn=512, riders=()):
    m, d = h.shape
    ka = a.shape[1]
    kb = b.shape[1]
    n = w_pa.shape[1]
    gm, nb = m // tm, n // tn
    r_in, r_out, r_shapes, r_bytes = _rider_specs(riders, gm * nb, lambda i, j: i * nb + j)
    blocks = tm * (d + ka + kb) * 2 + (2 * d + ka + kb) * tn * 2 + tm * tn * 2 + r_bytes
    return pl.pallas_call(
        _with_riders(_mix_body, 7, 1, len(riders)),
        out_shape=[jax.ShapeDtypeStruct((m, n), BF16)] + r_shapes,
        grid=(gm, nb),
        in_specs=[pl.BlockSpec((tm, d), lambda i, j: (i, 0)),
                  pl.BlockSpec((tm, ka), lambda i, j: (i, 0)),
                  pl.BlockSpec((tm, kb), lambda i, j: (i, 0)),
                  pl.BlockSpec((d, tn), lambda i, j: (0, j)),
                  pl.BlockSpec((d, tn), lambda i, j: (0, j + nb)),
                  pl.BlockSpec((ka, tn), lambda i, j: (0, j)),
                  pl.BlockSpec((kb, tn), lambda i, j: (0, j))] + r_in,
        out_specs=[pl.BlockSpec((tm, tn), lambda i, j: (i, j))] + r_out,
        compiler_params=_params(("arbitrary", "arbitrary"), _vmem_limit(blocks, temp_bytes=4 * tm * tn * 4)),
        name="mix",
    )(h, a, b, w_gate, w_gate, w_pa, w_pb, *[w3 for w3, _ in riders])


def _mla_body(q_ref, kn_ref, kp_ref, v_ref, o_ref, *, seq, tq, tk):
    c = (MLA_NOPE + MLA_ROPE) ** -0.5 * math.log2(math.e)
    ones = jnp.ones((tk, LANES), BF16)
    row = lax.broadcasted_iota(jnp.int32, (tq, tk), 0)
    col = lax.broadcasted_iota(jnp.int32, (tq, tk), 1)
    for qi in range(seq // tq):
        q = q_ref[qi * tq:(qi + 1) * tq, :]
        n_kb = -(-((qi + 1) * tq) // tk)
        m = None
        acc = None
        for kj in range(n_kb):
            ks = slice(kj * tk, (kj + 1) * tk)
            kf = jnp.concatenate([kn_ref[ks, :], kp_ref[ks, :]], axis=1)
            s = lax.dot_general(q, kf, (((1,), (1,)), ((), ())), preferred_element_type=F32)
            if (kj + 1) * tk > qi * tq + CHUNK:
                visible = (col + kj * tk) // CHUNK <= (row + qi * tq) // CHUNK
                s = jnp.where(visible, s, NEG)
            bm = jnp.max(s, axis=-1, keepdims=True)
            v_ext = jnp.concatenate([v_ref[ks, :], ones], axis=1)
            if m is None:
                m = bm
                p = jnp.exp2((s - m) * c)
                acc = jnp.dot(p.astype(BF16), v_ext, preferred_element_type=F32)
            else:
                m_new = jnp.maximum(m, bm)
                alpha = jnp.exp2((m - m_new) * c)
                p = jnp.exp2((s - m_new) * c)
                acc = alpha * acc + jnp.dot(p.astype(BF16), v_ext, preferred_element_type=F32)
                m = m_new
        o_ref[qi * tq:(qi + 1) * tq, :] = (acc[:, :MLA_V] / acc[:, MLA_V:]).astype(o_ref.dtype)


def _mla_attention(q, kv, kpe, batch, seq, tq=256, tk=256, riders=()):
    rows = q.shape[0]
    r_in, r_out, r_shapes, r_bytes = _rider_specs(riders, batch * MLA_HEADS, lambda b, h: b * MLA_HEADS + h)
    blocks = seq * MLA_QK_PAD * 2 + 3 * seq * LANES * 2 + seq * MLA_V * 2 + r_bytes
    return pl.pallas_call(
        _with_riders(functools.partial(_mla_body, seq=seq, tq=tq, tk=tk), 4, 1, len(riders)),
        out_shape=[jax.ShapeDtypeStruct((rows, MLA_HEADS * MLA_V), BF16)] + r_shapes,
        grid=(batch, MLA_HEADS),
        in_specs=[pl.BlockSpec((seq, MLA_QK_PAD), lambda b, h: (b, h)),
                  pl.BlockSpec((seq, MLA_NOPE), lambda b, h: (b, h)),
                  pl.BlockSpec((seq, LANES), lambda b, h: (b, 0)),
                  pl.BlockSpec((seq, MLA_V), lambda b, h: (b, MLA_HEADS + h))] + r_in,
        out_specs=[pl.BlockSpec((seq, MLA_V), lambda b, h: (b, h))] + r_out,
        compiler_params=_params(("arbitrary", "arbitrary"), _vmem_limit(blocks, temp_bytes=8 << 20)),
        name="mla",
    )(q, kv, kpe, kv, *[w3 for w3, _ in riders])


def _t5_bucket_table():
    iq = np.arange(SWA_BLOCK)
    ik = np.arange(2 * SWA_BLOCK) - SWA_BLOCK
    rel = ik[None, :] - iq[:, None]
    nb = NUM_BUCKETS // 2
    max_exact = nb // 2
    ret = np.where(rel > 0, nb, 0)
    n = np.abs(rel)
    nf = np.maximum(n, 1).astype(np.float64)
    large = max_exact + (np.log(nf / max_exact) / math.log(MAX_DISTANCE / max_exact)
                         * (nb - max_exact)).astype(np.int32)
    large = np.minimum(large, nb - 1)
    return (ret + np.where(n < max_exact, n, large)).astype(np.int32)


def _bias_body(rb_ref, bucket_ref, o_ref):
    first = pl.program_id(0) == 0
    hd = pl.program_id(1)
    bk = bucket_ref[...]
    tab = jnp.zeros(bk.shape, F32)
    for b in range(NUM_BUCKETS):
        tab = jnp.where(bk == b, rb_ref[b, hd] * LOG2E, tab)
    col = lax.broadcasted_iota(jnp.int32, bk.shape, 1)
    qc = lax.broadcasted_iota(jnp.int32, bk.shape, 0) // CHUNK
    kc = col // CHUNK - SWA_BLOCK // CHUNK
    valid = jnp.logical_and(kc <= qc, kc >= qc - WINDOW // CHUNK)
    valid = jnp.logical_and(valid, jnp.logical_not(jnp.logical_and(first, col < SWA_BLOCK)))
    o_ref[...] = jnp.where(valid, tab, NEG)


def _bias_table(rel_bias):
    bucket = jnp.asarray(_t5_bucket_table())
    return pl.pallas_call(
        _bias_body,
        out_shape=jax.ShapeDtypeStruct((2, SWA_HEADS, SWA_BLOCK, 2 * SWA_BLOCK), F32),
        grid=(2, SWA_HEADS),
        in_specs=[pl.BlockSpec(memory_space=pltpu.SMEM),
                  pl.BlockSpec((SWA_BLOCK, 2 * SWA_BLOCK), lambda f, h: (0, 0))],
        out_specs=pl.BlockSpec((None, None, SWA_BLOCK, 2 * SWA_BLOCK), lambda f, h: (f, h, 0, 0)),
        compiler_params=_params(("arbitrary", "arbitrary"), 16 << 20),
        name="swa_bias",
    )(rel_bias, bucket)


def _swa_body(sink_ref, q_ref, kp_ref, kc_ref, vp_ref, vc_ref, bias_ref, o_ref, *, n_sub):
    c = SWA_HEAD_DIM ** -0.5 * LOG2E
    blk = SWA_BLOCK
    first_table = jnp.minimum(pl.program_id(1), 1)
    low = lax.broadcasted_iota(jnp.int32, ((n_sub + 1) * blk, LANES), 1) < SWA_HEAD_DIM
    low_q = lax.broadcasted_iota(jnp.int32, (blk, LANES), 1) < SWA_HEAD_DIM

    def both_halves(pair, want_low):
        other = pltpu.roll(pair, SWA_HEAD_DIM, 1)
        return jnp.where(low, pair, other) if want_low else jnp.where(low, other, pair)

    k2, v2 = [], []
    for j in range(SWA_KV_HEADS // 2):
        cols = slice(j * LANES, (j + 1) * LANES)
        kpair = jnp.concatenate([kp_ref[:, cols], kc_ref[:, cols]], axis=0)
        vpair = jnp.concatenate([vp_ref[:, cols], vc_ref[:, cols]], axis=0)
        k2 += [both_halves(kpair, True), both_halves(kpair, False)]
        v2 += [both_halves(vpair, True), both_halves(vpair, False)]

    zero = jnp.zeros((blk, LANES), BF16)
    for r in range(n_sub):
        q_rows = slice(r * blk, (r + 1) * blk)
        band = slice(r * blk, (r + 2) * blk)
        table = first_table if r == 0 else 1
        for p in range(SWA_HEADS // 2):
            g = (2 * p) // SWA_GROUP
            qp = q_ref[q_rows, p * LANES:(p + 1) * LANES]
            qs = jnp.concatenate([jnp.where(low_q, qp, zero), jnp.where(low_q, zero, qp)], axis=0)
            s = lax.dot_general(qs, k2[g][band], (((1,), (1,)), ((), ())), preferred_element_type=F32)
            t = s * c + bias_ref[table, 2 * p:2 * p + 2].reshape(2 * blk, 2 * blk)
            es, dens = [], []
            for half in range(2):
                th = t[half * blk:(half + 1) * blk]
                sk = sink_ref[0, 2 * p + half] * LOG2E
                m = jnp.maximum(jnp.max(th, axis=-1, keepdims=True), sk)
                e = jnp.exp2(th - m)
                es.append(e.astype(BF16))
                dens.append(jnp.sum(e, axis=-1, keepdims=True) + jnp.exp2(sk - m))
            o2 = jnp.dot(jnp.concatenate(es, axis=0), v2[g][band], preferred_element_type=F32)
            o_ref[q_rows, p * LANES:(p + 1) * LANES] = jnp.where(
                low_q, o2[:blk] / dens[0], o2[blk:] / dens[1]).astype(o_ref.dtype)


def _swa_attention(proj, sinks, bias_tab, batch, seq, qcol, kcol, vcol, n_sub=4, riders=()):
    rows = proj.shape[0]
    blk = SWA_BLOCK
    steps = seq // (n_sub * blk)
    cur = lambda b, n: b * steps + n
    prev = lambda b, n: (b * steps + n) * n_sub - jnp.minimum(n, 1)
    r_in, r_out, r_shapes, r_bytes = _rider_specs(riders, batch * steps, cur)
    blocks = (2 * n_sub * blk * SWA_Q_WIDTH * 2 + 2 * (n_sub + 1) * blk * SWA_KV_WIDTH * 2
              + 2 * SWA_HEADS * blk * 2 * blk * 4 + r_bytes)
    return pl.pallas_call(
        _with_riders(functools.partial(_swa_body, n_sub=n_sub), 7, 1, len(riders)),
        out_shape=[jax.ShapeDtypeStruct((rows, SWA_Q_WIDTH), BF16)] + r_shapes,
        grid=(batch, steps),
        in_specs=[pl.BlockSpec(memory_space=pltpu.SMEM),
                  pl.BlockSpec((n_sub * blk, SWA_Q_WIDTH), lambda b, n: (cur(b, n), qcol)),
                  pl.BlockSpec((blk, SWA_KV_WIDTH), lambda b, n: (prev(b, n), kcol)),
                  pl.BlockSpec((n_sub * blk, SWA_KV_WIDTH), lambda b, n: (cur(b, n), kcol)),
                  pl.BlockSpec((blk, SWA_KV_WIDTH), lambda b, n: (prev(b, n), vcol)),
                  pl.BlockSpec((n_sub * blk, SWA_KV_WIDTH), lambda b, n: (cur(b, n), vcol)),
                  pl.BlockSpec((2, SWA_HEADS, blk, 2 * blk), lambda b, n: (0, 0, 0, 0))] + r_in,
        out_specs=[pl.BlockSpec((n_sub * blk, SWA_Q_WIDTH), lambda b, n: (cur(b, n), 0))] + r_out,
        compiler_params=_params(("arbitrary", "arbitrary"), _vmem_limit(blocks, temp_bytes=8 << 20)),
        name="swa",
    )(sinks, proj, proj, proj, proj, proj, bias_tab, *[w3 for w3, _ in riders])


def _rope_tables(positions):
    half = MLA_ROPE // 2
    inv = ROPE_THETA ** (-jnp.arange(half, dtype=F32) * (2.0 / MLA_ROPE))
    one, zero = jnp.ones((half,), F32), jnp.zeros((half,), F32)
    ang = positions.astype(F32).reshape(-1, 1) * jnp.concatenate([inv, zero, inv, zero])
    return (jnp.cos(ang) * jnp.concatenate([one, zero, one, zero]),
            jnp.sin(ang) * jnp.concatenate([-one, zero, one, zero]))


def kernel(x, c, positions, w_ada, b_ada, pre_norm_g, post_norm_g, w_in, q_norm_g, kv_norm_g, w_uq, w_ukv,
           swa_sinks, rel_bias, w_gate, w_proj_a, w_proj_b, w_out, w_ff_up, w_ff_down):
    batch, seq, d = x.shape
    depth = w_ada.shape[0]
    rows = batch * seq
    x2 = x.reshape(rows, d)

    cos_t, sin_t = _rope_tables(positions)
    bias_tab = _bias_table(rel_bias.astype(F32))
    c_pad = jnp.concatenate([c, jnp.zeros((16 - batch, d), c.dtype)], axis=0)

    o_qlat, o_kvlat, o_rope = 0, Q_LORA, Q_LORA + KV_LORA
    o_qs = o_rope + MLA_ROPE
    o_ks = o_qs + SWA_Q_WIDTH
    o_vs = o_ks + SWA_KV_WIDTH

    for l in range(depth):
        wi = w_in[l].astype(BF16)
        w_main = jnp.concatenate([wi[:, o_qs:o_ks], wi[:, o_qlat:o_rope], wi[:, o_ks:o_vs + SWA_KV_WIDTH]],
                                 axis=1)
        w_rope = _spread_rope_columns(wi[:, o_rope:o_qs], 1)
        w_q3 = w_uq[l].astype(BF16).reshape(Q_LORA, MLA_HEADS, MLA_NOPE + MLA_ROPE)
        w_q = jnp.concatenate([w_q3[:, :, :MLA_NOPE], _spread_rope_columns(w_q3[:, :, MLA_NOPE:], 2)],
                              axis=2).reshape(Q_LORA, MLA_HEADS * MLA_QK_PAD)
        w_kv3 = w_ukv[l].reshape(KV_LORA, MLA_HEADS, MLA_NOPE + MLA_V)
        w_kv = jnp.concatenate([w_kv3[:, :, :MLA_NOPE].reshape(KV_LORA, -1),
                                w_kv3[:, :, MLA_NOPE:].reshape(KV_LORA, -1)], axis=1).astype(BF16)
        qcol_s, col_qlat, col_kvlat = 0, SWA_Q_WIDTH // Q_LORA, (SWA_Q_WIDTH + Q_LORA) // KV_LORA
        col_ks = (SWA_Q_WIDTH + Q_LORA + KV_LORA) // SWA_KV_WIDTH
        col_vs = col_ks + 1

        mod = _modulation(c_pad, w_ada[l], b_ada[l].reshape(1, -1))[:batch]
        mod3 = mod.reshape(batch, 6, d)

        proj, h = _produce_matmul([x2], mod3, [pre_norm_g[l, 0].reshape(1, d)], w_main, seq,
                                  produce=_token_mixer_input, keep_lhs=True, tn=512, name="mm_in")
        kpe = _matmul_rope(h, w_rope, cos_t, sin_t)
        q = _matmul_norm(proj, col_qlat, Q_LORA, q_norm_g[l].reshape(1, -1), w_q,
                         rope_tables=(cos_t, sin_t), tn=2048, name="mm_uq")
        kv = _matmul_norm(proj, col_kvlat, KV_LORA, kv_norm_g[l].reshape(1, -1), w_kv, tn=2048, name="mm_ukv")
        att_a, w_gate_b, w_pa_b, w_pb_b = _mla_attention(
            q, kv, kpe, batch, seq, riders=[(w_gate, l), (w_proj_a, l), (w_proj_b, l)])
        att_b, w_out_b = _swa_attention(proj, swa_sinks[l].reshape(1, -1).astype(F32), bias_tab, batch, seq,
                                        qcol_s, col_ks, col_vs, riders=[(w_out, l)])
        mix, w_up_b = _gated_mix(h, att_a, att_b, w_gate_b, w_pa_b, w_pb_b, riders=[(w_ff_up, l)])
        (y,) = _matmul(mix, w_out_b, name="mm_out")
        post_g1 = post_norm_g[l, 0].reshape(1, d)

        u, w_down_b = _produce_matmul([x2, y], mod3, [post_g1, pre_norm_g[l, 1].reshape(1, d)], w_up_b, seq,
                                      produce=_channel_mixer_input, act="relu2", riders=[(w_ff_down, l)],
                                      name="mm_up")
        y2 = _matmul_ksplit(u, w_down_b, tk=4096, name="mm_down")
        x2 = _post2(x2, y, y2, mod3, post_g1, post_norm_g[l, 1].reshape(1, d), seq)

    return x2.reshape(batch, seq, d)
```

```python
import functools
import math

import numpy as np
import jax
import jax.numpy as jnp
from jax import lax
from jax.experimental import pallas as pl
from jax.experimental.pallas import tpu as pltpu

F32 = jnp.float32
BF16 = jnp.bfloat16

D_MODEL = 4096
CHUNK = 64
MLA_HEADS = 16
MLA_NOPE = 128
MLA_ROPE = 64
MLA_V = 128
Q_LORA = 1024
KV_LORA = 512
ROPE_THETA = 10000.0
SWA_HEADS = 32
SWA_KV_HEADS = 4
SWA_HEAD_DIM = 64
SWA_GROUP = SWA_HEADS // SWA_KV_HEADS
WINDOW = 128
SWA_BLOCK = 128
NUM_BUCKETS = 32
MAX_DISTANCE = 128
EPS = 1e-6
NEG = -1e30
LOG2E = math.log2(math.e)

LANES = 128
ROW_CHUNK = 16
ROW_UNROLL = 4
MLA_QK_PAD = 256
SWA_Q_WIDTH = SWA_HEADS * SWA_HEAD_DIM
SWA_KV_WIDTH = SWA_KV_HEADS * SWA_HEAD_DIM
VMEM_CAP_BYTES = 56 * 1024 * 1024


def _vmem_limit(block_bytes, scratch_bytes=0, temp_bytes=0):
    need = 2 * block_bytes + scratch_bytes + temp_bytes + (4 << 20)
    return int(min(max(need, 16 << 20), VMEM_CAP_BYTES))


def _params(sem, vmem):
    return pltpu.CompilerParams(dimension_semantics=sem, vmem_limit_bytes=vmem)


def _rms(x):
    return x * lax.rsqrt(jnp.mean(x * x, axis=-1, keepdims=True) + EPS)


def _rope_half_swap(x, cos_t, sin_t):
    return x * cos_t + pltpu.roll(x, LANES // 2, 1) * sin_t


def _spread_rope_columns(w, axis):
    half = MLA_ROPE // 2
    x1, x2 = jnp.split(w, 2, axis=axis)
    zero = jnp.zeros_like(x1)
    assert x1.shape[axis] == half and 4 * half == LANES
    return jnp.concatenate([x1, zero, x2, zero], axis=axis)


def _mod_body(c_ref, w_ref, b_ref, o_ref):
    c = c_ref[...]
    ca = (c * jax.nn.sigmoid(c)).astype(BF16)
    o_ref[...] = jnp.dot(ca, w_ref[...].astype(BF16), preferred_element_type=F32) + b_ref[...]


def _modulation(c_pad, w_ada, b_ada, layer, n_cols, tn=512):
    m, k = c_pad.shape
    blocks = m * k * 4 + k * tn * 4 + tn * 4 + m * tn * 4
    return pl.pallas_call(
        _mod_body,
        out_shape=jax.ShapeDtypeStruct((m, n_cols), F32),
        grid=(n_cols // tn,),
        in_specs=[pl.BlockSpec((m, k), lambda j: (0, 0)),
                  pl.BlockSpec((None, k, tn), lambda j: (layer, 0, j)),
                  pl.BlockSpec((None, 1, tn), lambda j: (layer, 0, j))],
        out_specs=pl.BlockSpec((m, tn), lambda j: (0, j)),
        compiler_params=_params(("arbitrary",), _vmem_limit(blocks, temp_bytes=k * tn * 2)),
        name="mod",
    )(c_pad, w_ada, b_ada)


def _cast_body(src_ref, dst_ref):
    dst_ref[...] = src_ref[...].astype(dst_ref.dtype)


def _with_riders(body, n_in, n_out, plan):
    if not plan:
        return body
    r_in = sum(k_in for _, k_in, _ in plan)
    r_out = sum(k_out for _, _, k_out in plan)

    def wrapped(*refs):
        ins = refs[:n_in]
        rider_ins = refs[n_in:n_in + r_in]
        outs = refs[n_in + r_in:n_in + r_in + n_out]
        rider_outs = refs[n_in + r_in + n_out:n_in + r_in + n_out + r_out]
        scratch = refs[n_in + r_in + n_out + r_out:]
        a = b = 0
        for fn, k_in, k_out in plan:
            fn(*rider_ins[a:a + k_in], *rider_outs[b:b + k_out])
            a, b = a + k_in, b + k_out
        body(*ins, *outs, *scratch)

    return wrapped


def _rider_specs(riders, n_steps, step_of):
    in_specs, out_specs, out_shapes, plan, args, block_bytes = [], [], [], [], [], 0
    for kind, *rest in riders:
        if kind == "cast":
            w3, layer = rest
            _, r, cols = w3.shape
            tr = r // n_steps
            assert tr * n_steps == r and tr % 16 == 0, (w3.shape, n_steps)
            in_specs.append(pl.BlockSpec((None, tr, cols), lambda *g, layer=layer: (layer, step_of(*g), 0)))
            out_specs.append(pl.BlockSpec((tr, cols), lambda *g: (step_of(*g), 0)))
            out_shapes.append(jax.ShapeDtypeStruct((r, cols), BF16))
            plan.append((_cast_body, 1, 1))
            args.append(w3)
            block_bytes += tr * cols * (4 + 2)
        else:
            assert kind == "mod", kind
            c_pad, w_ada, b_ada, layer, c0 = rest
            m, k = c_pad.shape
            n_cols = w_ada.shape[2] - c0
            tc = n_cols // n_steps
            assert tc * n_steps == n_cols and tc % LANES == 0 and c0 % tc == 0, (w_ada.shape, c0, n_steps)
            col = lambda *g, first=c0 // tc: first + step_of(*g)
            in_specs += [pl.BlockSpec((m, k), lambda *g: (0, 0)),
                         pl.BlockSpec((None, k, tc), lambda *g, layer=layer, col=col: (layer, 0, col(*g))),
                         pl.BlockSpec((None, 1, tc), lambda *g, layer=layer, col=col: (layer, 0, col(*g)))]
            out_specs.append(pl.BlockSpec((m, tc), lambda *g: (0, step_of(*g))))
            out_shapes.append(jax.ShapeDtypeStruct((m, n_cols), F32))
            plan.append((_mod_body, 3, 1))
            args += [c_pad, w_ada, b_ada]
            block_bytes += m * k * 4 + k * tc * (4 + 2) + tc * 4 + m * tc * 4
    return in_specs, out_specs, out_shapes, block_bytes, plan, args


def _token_mixer_input(src, mod_ref, vec, rows):
    (x_ref,), (g_ref,) = src, vec
    return _rms(x_ref[rows, :]) * (g_ref[...] * (1.0 + mod_ref[1:2, :])) + mod_ref[0:1, :]


def _channel_mixer_input(src, mod_ref, vec, rows):
    (x_ref, y_ref), (pg_ref, g_ref) = src, vec
    x1 = x_ref[rows, :] + _rms(y_ref[rows, :].astype(F32)) * (mod_ref[2:3, :] * pg_ref[...])
    return _rms(x1) * (g_ref[...] * (1.0 + mod_ref[4:5, :])) + mod_ref[3:4, :]


def _produce_matmul_body(*refs, n_src, n_vec, produce, keep_lhs, act):
    src, mod_ref = refs[:n_src], refs[n_src]
    vec, w_ref = refs[n_src + 1:n_src + 1 + n_vec], refs[n_src + 1 + n_vec]
    outs = refs[n_src + 2 + n_vec:]
    o_ref, lhs_out_ref = outs[0], (outs[1] if keep_lhs else None)
    buf_a, buf_b = outs[-2:]
    i = pl.program_id(0)
    j = pl.program_id(1)
    slab = src[0].shape[0]

    def step(build_ref, ready_ref):
        for c in range(slab // ROW_CHUNK):
            rows = slice(c * ROW_CHUNK, (c + 1) * ROW_CHUNK)
            h = produce(src, mod_ref, vec, rows).astype(build_ref.dtype)
            dst = pl.ds(pl.multiple_of(j * slab + c * ROW_CHUNK, ROW_CHUNK), ROW_CHUNK)
            build_ref[dst, :] = h
            if keep_lhs:
                lhs_out_ref[rows, :] = h
        if ready_ref is None:
            return
        acc = jnp.dot(ready_ref[...], w_ref[...], preferred_element_type=F32)
        if act == "relu2":
            acc = jnp.maximum(acc, 0.0)
            acc = acc * acc
        o_ref[...] = acc.astype(o_ref.dtype)

    @pl.when(i == 0)
    def _():
        step(buf_a, None)

    @pl.when(jnp.logical_and(i > 0, i % 2 == 0))
    def _():
        step(buf_a, buf_b)

    @pl.when(i % 2 == 1)
    def _():
        step(buf_b, buf_a)


def _produce_matmul(src, mod3, vec, w, seq, *, produce, keep_lhs=False, act=None, tm=1024, tn=1024,
                    riders=(), name):
    m, d = src[0].shape
    n = w.shape[1]
    gm, gn = m // tm, n // tn
    slab = tm // gn
    assert slab * gn == tm and slab % ROW_CHUNK == 0 and seq % tm == 0
    build_tile = lambda i: jnp.minimum(i, gm - 1)
    ready_tile = lambda i: jnp.maximum(i - 1, 0)
    ready_col = lambda i, j: jnp.where(i == 0, 0, j)
    slab_of = lambda i, j: jnp.where(i < gm, i * gn + j, gm * gn - 1)
    slab_spec = pl.BlockSpec((slab, d), lambda i, j: (slab_of(i, j), 0))
    vec_spec = pl.BlockSpec((1, d), lambda i, j: (0, 0))
    r_in, r_out, r_shapes, r_bytes, r_plan, r_args = _rider_specs(
        riders, gm * gn, lambda i, j: ready_tile(i) * gn + ready_col(i, j))
    src_bytes = sum(slab * d * a.dtype.itemsize for a in src)
    blocks = src_bytes + (6 + len(vec)) * d * 4 + d * tn * 2 + tm * tn * 2 + keep_lhs * slab * d * 2 + r_bytes
    body = functools.partial(_produce_matmul_body, n_src=len(src), n_vec=len(vec), produce=produce,
                             keep_lhs=keep_lhs, act=act)
    return pl.pallas_call(
        _with_riders(body, len(src) + len(vec) + 2, 1 + keep_lhs, r_plan),
        out_shape=([jax.ShapeDtypeStruct((m, n), BF16)] + [jax.ShapeDtypeStruct((m, d), BF16)] * keep_lhs
                   + r_shapes),
        grid=(gm + 1, gn),
        in_specs=([slab_spec] * len(src)
                  + [pl.BlockSpec((None, mod3.shape[1], d), lambda i, j: (build_tile(i) * tm // seq, 0, 0))]
                  + [vec_spec] * len(vec)
                  + [pl.BlockSpec((d, tn), lambda i, j: (0, ready_col(i, j)))] + r_in),
        out_specs=([pl.BlockSpec((tm, tn), lambda i, j: (ready_tile(i), ready_col(i, j)))]
                   + [slab_spec] * keep_lhs + r_out),
        scratch_shapes=[pltpu.VMEM((tm, d), BF16), pltpu.VMEM((tm, d), BF16)],
        compiler_params=_params(("arbitrary", "arbitrary"),
                                _vmem_limit(blocks, scratch_bytes=2 * tm * d * 2, temp_bytes=2 * tm * tn * 4)),
        name=name,
    )(*src, mod3, *vec, w, *r_args)


def _post2_body(x_ref, y_ref, y2_ref, mod_ref, pg1_ref, pg2_ref, o_ref):
    def chunk(r, carry):
        rows = pl.ds(pl.multiple_of(r * ROW_CHUNK, ROW_CHUNK), ROW_CHUNK)
        x1 = x_ref[rows, :] + _rms(y_ref[rows, :].astype(F32)) * (mod_ref[2:3, :] * pg1_ref[...])
        o_ref[rows, :] = x1 + _rms(y2_ref[rows, :].astype(F32)) * (mod_ref[5:6, :] * pg2_ref[...])
        return carry

    lax.fori_loop(0, x_ref.shape[0] // ROW_CHUNK, chunk, 0, unroll=ROW_UNROLL)


def _post2(x2, y, y2, mod3, post_g1, post_g2, seq, tr=512):
    rows, d = x2.shape
    per_b = seq // tr
    blocks = tr * d * (4 + 2 + 2 + 4) + 8 * d * 4
    chunk_temps = 8 * ROW_UNROLL * ROW_CHUNK * d * 4
    row = pl.BlockSpec((tr, d), lambda i: (i, 0))
    vec = pl.BlockSpec((1, d), lambda i: (0, 0))
    return pl.pallas_call(
        _post2_body,
        out_shape=jax.ShapeDtypeStruct((rows, d), F32),
        grid=(rows // tr,),
        in_specs=[row, row, row, pl.BlockSpec((None, 6, d), lambda i: (i // per_b, 0, 0)), vec, vec],
        out_specs=row,
        compiler_params=_params(("arbitrary",), _vmem_limit(blocks, temp_bytes=chunk_temps)),
        name="post2",
    )(x2, y, y2, mod3, post_g1, post_g2)


def _mm_body(x_ref, w_ref, o_ref):
    o_ref[...] = jnp.dot(x_ref[...], w_ref[...], preferred_element_type=F32).astype(o_ref.dtype)


def _matmul(x, w, *, tm=1024, tn=1024, riders=(), name):
    m, k = x.shape
    n = w.shape[1]
    tn = min(tn, n)
    gm, gn = m // tm, n // tn
    r_in, r_out, r_shapes, r_bytes, r_plan, r_args = _rider_specs(riders, gm * gn, lambda i, j: i * gn + j)
    blocks = tm * k * 2 + k * tn * 2 + tm * tn * 2 + r_bytes
    return pl.pallas_call(
        _with_riders(_mm_body, 2, 1, r_plan),
        out_shape=[jax.ShapeDtypeStruct((m, n), BF16)] + r_shapes,
        grid=(gm, gn),
        in_specs=[pl.BlockSpec((tm, k), lambda i, j: (i, 0)),
                  pl.BlockSpec((k, tn), lambda i, j: (0, j))] + r_in,
        out_specs=[pl.BlockSpec((tm, tn), lambda i, j: (i, j))] + r_out,
        compiler_params=_params(("arbitrary", "arbitrary"), _vmem_limit(blocks, temp_bytes=2 * tm * tn * 4)),
        name=name,
    )(x, w, *r_args)


def _mm_rope_body(x_ref, w_ref, cos_ref, sin_ref, o_ref):
    acc = jnp.dot(x_ref[...], w_ref[...], preferred_element_type=F32)
    o_ref[...] = _rope_half_swap(acc, cos_ref[...], sin_ref[...]).astype(o_ref.dtype)


def _matmul_rope(x, w, cos_t, sin_t, *, tm=1024):
    m, k = x.shape
    n = w.shape[1]
    blocks = tm * k * 2 + k * n * 2 + 2 * tm * n * 4 + tm * n * 2
    return pl.pallas_call(
        _mm_rope_body,
        out_shape=jax.ShapeDtypeStruct((m, n), BF16),
        grid=(m // tm,),
        in_specs=[pl.BlockSpec((tm, k), lambda i: (i, 0)),
                  pl.BlockSpec((k, n), lambda i: (0, 0)),
                  pl.BlockSpec((tm, n), lambda i: (i, 0)),
                  pl.BlockSpec((tm, n), lambda i: (i, 0))],
        out_specs=pl.BlockSpec((tm, n), lambda i: (i, 0)),
        compiler_params=_params(("arbitrary",), _vmem_limit(blocks, temp_bytes=4 * tm * n * 4)),
        name="mm_krope",
    )(x, w, cos_t, sin_t)


def _mm_norm_body(x_ref, g_ref, w_ref, *rest, rope):
    xn_ref = rest[-1]

    @pl.when(pl.program_id(1) == 0)
    def _():
        xn_ref[...] = (_rms(x_ref[...].astype(F32)) * g_ref[...]).astype(BF16)

    acc = jnp.dot(xn_ref[...], w_ref[...], preferred_element_type=F32)
    if not rope:
        o_ref = rest[0]
        o_ref[...] = acc.astype(o_ref.dtype)
        return
    cos_ref, sin_ref, o_ref = rest[:3]
    cos_t = cos_ref[...]
    sin_t = sin_ref[...]
    for hd in range(acc.shape[1] // MLA_QK_PAD):
        c0 = hd * MLA_QK_PAD
        o_ref[:, c0:c0 + MLA_NOPE] = acc[:, c0:c0 + MLA_NOPE].astype(o_ref.dtype)
        pe = _rope_half_swap(acc[:, c0 + MLA_NOPE:c0 + MLA_QK_PAD], cos_t, sin_t)
        o_ref[:, c0 + MLA_NOPE:c0 + MLA_QK_PAD] = pe.astype(o_ref.dtype)


def _matmul_norm(x, xcol, kdim, g, w, *, rope_tables=None, tm=1024, tn=1024, name):
    m = x.shape[0]
    n = w.shape[1]
    blocks = tm * kdim * 2 + kdim * 4 + kdim * tn * 2 + tm * tn * 2
    in_specs = [pl.BlockSpec((tm, kdim), lambda i, j: (i, xcol)),
                pl.BlockSpec((1, kdim), lambda i, j: (0, 0)),
                pl.BlockSpec((kdim, tn), lambda i, j: (0, j))]
    args = [x, g, w]
    if rope_tables is not None:
        in_specs += [pl.BlockSpec((tm, LANES), lambda i, j: (i, 0))] * 2
        args += list(rope_tables)
        blocks += 2 * tm * LANES * 4
    return pl.pallas_call(
        functools.partial(_mm_norm_body, rope=rope_tables is not None),
        out_shape=jax.ShapeDtypeStruct((m, n), BF16),
        grid=(m // tm, n // tn),
        in_specs=in_specs,
        out_specs=pl.BlockSpec((tm, tn), lambda i, j: (i, j)),
        scratch_shapes=[pltpu.VMEM((tm, kdim), BF16)],
        compiler_params=_params(("arbitrary", "arbitrary"),
                                _vmem_limit(blocks, scratch_bytes=tm * kdim * 2,
                                            temp_bytes=2 * tm * tn * 4 + 3 * tm * kdim * 4)),
        name=name,
    )(*args)


def _mm_acc_body(x_ref, w_ref, o_ref, acc_ref):
    kk = pl.program_id(2)

    @pl.when(kk == 0)
    def _():
        acc_ref[...] = jnp.zeros_like(acc_ref)

    acc_ref[...] += jnp.dot(x_ref[...], w_ref[...], preferred_element_type=F32)

    @pl.when(kk == pl.num_programs(2) - 1)
    def _():
        o_ref[...] = acc_ref[...].astype(o_ref.dtype)


def _matmul_ksplit(x, w, *, tm=1024, tn=1024, tk=2048, name):
    m, k = x.shape
    n = w.shape[1]
    blocks = tm * tk * 2 + tk * tn * 2 + tm * tn * 2
    return pl.pallas_call(
        _mm_acc_body,
        out_shape=jax.ShapeDtypeStruct((m, n), BF16),
        grid=(m // tm, n // tn, k // tk),
        in_specs=[pl.BlockSpec((tm, tk), lambda i, j, kk: (i, kk)),
                  pl.BlockSpec((tk, tn), lambda i, j, kk: (kk, j))],
        out_specs=pl.BlockSpec((tm, tn), lambda i, j, kk: (i, j)),
        scratch_shapes=[pltpu.VMEM((tm, tn), F32)],
        compiler_params=_params(("arbitrary", "arbitrary", "arbitrary"),
                                _vmem_limit(blocks, scratch_bytes=tm * tn * 4, temp_bytes=2 * tm * tn * 4)),
        name=name,
    )(x, w)


def _mix_body(h_ref, a_ref, b_ref, wga_ref, wgb_ref, wa_ref, wb_ref, o_ref):
    h = h_ref[...]
    ga = jax.nn.sigmoid(jnp.dot(h, wga_ref[...], preferred_element_type=F32))
    t = ga * jnp.dot(a_ref[...], wa_ref[...], preferred_element_type=F32)
    gb = jax.nn.sigmoid(jnp.dot(h, wgb_ref[...], preferred_element_type=F32))
    t = t + gb * jnp.dot(b_ref[...], wb_ref[...], preferred_element_type=F32)
    o_ref[...] = t.astype(o_ref.dtype)


def _gated_mix(h, a, b, w_gate, w_pa, w_pb, *, tm=512, tn=512, riders=()):
    m, d = h.shape
    ka = a.shape[1]
    kb = b.shape[1]
    n = w_pa.shape[1]
    gm, nb = m // tm, n // tn
    r_in, r_out, r_shapes, r_bytes, r_plan, r_args = _rider_specs(riders, gm * nb, lambda i, j: i * nb + j)
    blocks = tm * (d + ka + kb) * 2 + (2 * d + ka + kb) * tn * 2 + tm * tn * 2 + r_bytes
    return pl.pallas_call(
        _with_riders(_mix_body, 7, 1, r_plan),
        out_shape=[jax.ShapeDtypeStruct((m, n), BF16)] + r_shapes,
        grid=(gm, nb),
        in_specs=[pl.BlockSpec((tm, d), lambda i, j: (i, 0)),
                  pl.BlockSpec((tm, ka), lambda i, j: (i, 0)),
                  pl.BlockSpec((tm, kb), lambda i, j: (i, 0)),
                  pl.BlockSpec((d, tn), lambda i, j: (0, j)),
                  pl.BlockSpec((d, tn), lambda i, j: (0, j + nb)),
                  pl.BlockSpec((ka, tn), lambda i, j: (0, j)),
                  pl.BlockSpec((kb, tn), lambda i, j: (0, j))] + r_in,
        out_specs=[pl.BlockSpec((tm, tn), lambda i, j: (i, j))] + r_out,
        compiler_params=_params(("arbitrary", "arbitrary"), _vmem_limit(blocks, temp_bytes=4 * tm * tn * 4)),
        name="mix",
    )(h, a, b, w_gate, w_gate, w_pa, w_pb, *r_args)


def _mla_body(q_ref, kn_ref, kp_ref, v_ref, o_ref, *, seq, tq, tk):
    c = (MLA_NOPE + MLA_ROPE) ** -0.5 * math.log2(math.e)
    ones = jnp.ones((tk, LANES), BF16)
    row = lax.broadcasted_iota(jnp.int32, (tq, tk), 0)
    col = lax.broadcasted_iota(jnp.int32, (tq, tk), 1)
    for qi in range(seq // tq):
        q = q_ref[qi * tq:(qi + 1) * tq, :]
        n_kb = -(-((qi + 1) * tq) // tk)
        m = None
        acc = None
        for kj in range(n_kb):
            ks = slice(kj * tk, (kj + 1) * tk)
            kf = jnp.concatenate([kn_ref[ks, :], kp_ref[ks, :]], axis=1)
            s = lax.dot_general(q, kf, (((1,), (1,)), ((), ())), preferred_element_type=F32)
            if (kj + 1) * tk > qi * tq + CHUNK:
                visible = (col + kj * tk) // CHUNK <= (row + qi * tq) // CHUNK
                s = jnp.where(visible, s, NEG)
            bm = jnp.max(s, axis=-1, keepdims=True)
            v_ext = jnp.concatenate([v_ref[ks, :], ones], axis=1)
            if m is None:
                m = bm
                p = jnp.exp2((s - m) * c)
                acc = jnp.dot(p.astype(BF16), v_ext, preferred_element_type=F32)
            else:
                m_new = jnp.maximum(m, bm)
                alpha = jnp.exp2((m - m_new) * c)
                p = jnp.exp2((s - m_new) * c)
                acc = alpha * acc + jnp.dot(p.astype(BF16), v_ext, preferred_element_type=F32)
                m = m_new
        o_ref[qi * tq:(qi + 1) * tq, :] = (acc[:, :MLA_V] / acc[:, MLA_V:]).astype(o_ref.dtype)


def _mla_attention(q, kv, kpe, batch, seq, tq=256, tk=256, riders=()):
    rows = q.shape[0]
    r_in, r_out, r_shapes, r_bytes, r_plan, r_args = _rider_specs(
        riders, batch * MLA_HEADS, lambda b, h: b * MLA_HEADS + h)
    blocks = seq * MLA_QK_PAD * 2 + 3 * seq * LANES * 2 + seq * MLA_V * 2 + r_bytes
    return pl.pallas_call(
        _with_riders(functools.partial(_mla_body, seq=seq, tq=tq, tk=tk), 4, 1, r_plan),
        out_shape=[jax.ShapeDtypeStruct((rows, MLA_HEADS * MLA_V), BF16)] + r_shapes,
        grid=(batch, MLA_HEADS),
        in_specs=[pl.BlockSpec((seq, MLA_QK_PAD), lambda b, h: (b, h)),
                  pl.BlockSpec((seq, MLA_NOPE), lambda b, h: (b, h)),
                  pl.BlockSpec((seq, LANES), lambda b, h: (b, 0)),
                  pl.BlockSpec((seq, MLA_V), lambda b, h: (b, MLA_HEADS + h))] + r_in,
        out_specs=[pl.BlockSpec((seq, MLA_V), lambda b, h: (b, h))] + r_out,
        compiler_params=_params(("arbitrary", "arbitrary"), _vmem_limit(blocks, temp_bytes=8 << 20)),
        name="mla",
    )(q, kv, kpe, kv, *r_args)


def _t5_bucket_table():
    iq = np.arange(SWA_BLOCK)
    ik = np.arange(2 * SWA_BLOCK) - SWA_BLOCK
    rel = ik[None, :] - iq[:, None]
    nb = NUM_BUCKETS // 2
    max_exact = nb // 2
    ret = np.where(rel > 0, nb, 0)
    n = np.abs(rel)
    nf = np.maximum(n, 1).astype(np.float64)
    large = max_exact + (np.log(nf / max_exact) / math.log(MAX_DISTANCE / max_exact)
                         * (nb - max_exact)).astype(np.int32)
    large = np.minimum(large, nb - 1)
    return (ret + np.where(n < max_exact, n, large)).astype(np.int32)


def _bias_body(rb_ref, bucket_ref, o_ref):
    first = pl.program_id(0) == 0
    hd = pl.program_id(1)
    bk = bucket_ref[...]
    tab = jnp.zeros(bk.shape, F32)
    for b in range(NUM_BUCKETS):
        tab = jnp.where(bk == b, rb_ref[b, hd] * LOG2E, tab)
    col = lax.broadcasted_iota(jnp.int32, bk.shape, 1)
    qc = lax.broadcasted_iota(jnp.int32, bk.shape, 0) // CHUNK
    kc = col // CHUNK - SWA_BLOCK // CHUNK
    valid = jnp.logical_and(kc <= qc, kc >= qc - WINDOW // CHUNK)
    valid = jnp.logical_and(valid, jnp.logical_not(jnp.logical_and(first, col < SWA_BLOCK)))
    o_ref[...] = jnp.where(valid, tab, NEG)


def _bias_table(rel_bias):
    bucket = jnp.asarray(_t5_bucket_table())
    return pl.pallas_call(
        _bias_body,
        out_shape=jax.ShapeDtypeStruct((2, SWA_HEADS, SWA_BLOCK, 2 * SWA_BLOCK), F32),
        grid=(2, SWA_HEADS),
        in_specs=[pl.BlockSpec(memory_space=pltpu.SMEM),
                  pl.BlockSpec((SWA_BLOCK, 2 * SWA_BLOCK), lambda f, h: (0, 0))],
        out_specs=pl.BlockSpec((None, None, SWA_BLOCK, 2 * SWA_BLOCK), lambda f, h: (f, h, 0, 0)),
        compiler_params=_params(("arbitrary", "arbitrary"), 16 << 20),
        name="swa_bias",
    )(rel_bias, bucket)


def _swa_body(sink_ref, q_ref, kp_ref, kc_ref, vp_ref, vc_ref, bias_ref, o_ref, *, n_sub):
    c = SWA_HEAD_DIM ** -0.5 * LOG2E
    blk = SWA_BLOCK
    first_table = jnp.minimum(pl.program_id(1), 1)
    low = lax.broadcasted_iota(jnp.int32, ((n_sub + 1) * blk, LANES), 1) < SWA_HEAD_DIM
    low_q = lax.broadcasted_iota(jnp.int32, (blk, LANES), 1) < SWA_HEAD_DIM

    def both_halves(pair, want_low):
        other = pltpu.roll(pair, SWA_HEAD_DIM, 1)
        return jnp.where(low, pair, other) if want_low else jnp.where(low, other, pair)

    k2, v2 = [], []
    for j in range(SWA_KV_HEADS // 2):
        cols = slice(j * LANES, (j + 1) * LANES)
        kpair = jnp.concatenate([kp_ref[:, cols], kc_ref[:, cols]], axis=0)
        vpair = jnp.concatenate([vp_ref[:, cols], vc_ref[:, cols]], axis=0)
        k2 += [both_halves(kpair, True), both_halves(kpair, False)]
        v2 += [both_halves(vpair, True), both_halves(vpair, False)]

    zero = jnp.zeros((blk, LANES), BF16)
    for r in range(n_sub):
        q_rows = slice(r * blk, (r + 1) * blk)
        band = slice(r * blk, (r + 2) * blk)
        table = first_table if r == 0 else 1
        for p in range(SWA_HEADS // 2):
            g = (2 * p) // SWA_GROUP
            qp = q_ref[q_rows, p * LANES:(p + 1) * LANES]
            qs = jnp.concatenate([jnp.where(low_q, qp, zero), jnp.where(low_q, zero, qp)], axis=0)
            s = lax.dot_general(qs, k2[g][band], (((1,), (1,)), ((), ())), preferred_element_type=F32)
            t = s * c + bias_ref[table, 2 * p:2 * p + 2].reshape(2 * blk, 2 * blk)
            es, dens = [], []
            for half in range(2):
                th = t[half * blk:(half + 1) * blk]
                sk = sink_ref[0, 2 * p + half] * LOG2E
                m = jnp.maximum(jnp.max(th, axis=-1, keepdims=True), sk)
                e = jnp.exp2(th - m)
                es.append(e.astype(BF16))
                dens.append(jnp.sum(e, axis=-1, keepdims=True) + jnp.exp2(sk - m))
            o2 = jnp.dot(jnp.concatenate(es, axis=0), v2[g][band], preferred_element_type=F32)
            o_ref[q_rows, p * LANES:(p + 1) * LANES] = jnp.where(
                low_q, o2[:blk] / dens[0], o2[blk:] / dens[1]).astype(o_ref.dtype)


def _swa_attention(proj, sinks, bias_tab, batch, seq, qcol, kcol, vcol, n_sub=4, riders=()):
    rows = proj.shape[0]
    blk = SWA_BLOCK
    steps = seq // (n_sub * blk)
    cur = lambda b, n: b * steps + n
    prev = lambda b, n: (b * steps + n) * n_sub - jnp.minimum(n, 1)
    r_in, r_out, r_shapes, r_bytes, r_plan, r_args = _rider_specs(riders, batch * steps, cur)
    blocks = (2 * n_sub * blk * SWA_Q_WIDTH * 2 + 2 * (n_sub + 1) * blk * SWA_KV_WIDTH * 2
              + 2 * SWA_HEADS * blk * 2 * blk * 4 + r_bytes)
    return pl.pallas_call(
        _with_riders(functools.partial(_swa_body, n_sub=n_sub), 7, 1, r_plan),
        out_shape=[jax.ShapeDtypeStruct((rows, SWA_Q_WIDTH), BF16)] + r_shapes,
        grid=(batch, steps),
        in_specs=[pl.BlockSpec(memory_space=pltpu.SMEM),
                  pl.BlockSpec((n_sub * blk, SWA_Q_WIDTH), lambda b, n: (cur(b, n), qcol)),
                  pl.BlockSpec((blk, SWA_KV_WIDTH), lambda b, n: (prev(b, n), kcol)),
                  pl.BlockSpec((n_sub * blk, SWA_KV_WIDTH), lambda b, n: (cur(b, n), kcol)),
                  pl.BlockSpec((blk, SWA_KV_WIDTH), lambda b, n: (prev(b, n), vcol)),
                  pl.BlockSpec((n_sub * blk, SWA_KV_WIDTH), lambda b, n: (cur(b, n), vcol)),
                  pl.BlockSpec((2, SWA_HEADS, blk, 2 * blk), lambda b, n: (0, 0, 0, 0))] + r_in,
        out_specs=[pl.BlockSpec((n_sub * blk, SWA_Q_WIDTH), lambda b, n: (cur(b, n), 0))] + r_out,
        compiler_params=_params(("arbitrary", "arbitrary"), _vmem_limit(blocks, temp_bytes=8 << 20)),
        name="swa",
    )(sinks, proj, proj, proj, proj, proj, bias_tab, *r_args)


def _rope_tables(positions):
    half = MLA_ROPE // 2
    inv = ROPE_THETA ** (-jnp.arange(half, dtype=F32) * (2.0 / MLA_ROPE))
    one, zero = jnp.ones((half,), F32), jnp.zeros((half,), F32)
    ang = positions.astype(F32).reshape(-1, 1) * jnp.concatenate([inv, zero, inv, zero])
    return (jnp.cos(ang) * jnp.concatenate([one, zero, one, zero]),
            jnp.sin(ang) * jnp.concatenate([-one, zero, one, zero]))


def kernel(x, c, positions, w_ada, b_ada, pre_norm_g, post_norm_g, w_in, q_norm_g, kv_norm_g, w_uq, w_ukv,
           swa_sinks, rel_bias, w_gate, w_proj_a, w_proj_b, w_out, w_ff_up, w_ff_down):
    batch, seq, d = x.shape
    depth = w_ada.shape[0]
    rows = batch * seq
    x2 = x.reshape(rows, d)

    cos_t, sin_t = _rope_tables(positions)
    bias_tab = _bias_table(rel_bias.astype(F32))
    c_pad = jnp.concatenate([c, jnp.zeros((16 - batch, d), c.dtype)], axis=0)

    o_qlat, o_kvlat, o_rope = 0, Q_LORA, Q_LORA + KV_LORA
    o_qs = o_rope + MLA_ROPE
    o_ks = o_qs + SWA_Q_WIDTH
    o_vs = o_ks + SWA_KV_WIDTH

    for l in range(depth):
        wi = w_in[l].astype(BF16)
        w_main = jnp.concatenate([wi[:, o_qs:o_ks], wi[:, o_qlat:o_rope], wi[:, o_ks:o_vs + SWA_KV_WIDTH]],
                                 axis=1)
        w_rope = _spread_rope_columns(wi[:, o_rope:o_qs], 1)
        w_q3 = w_uq[l].astype(BF16).reshape(Q_LORA, MLA_HEADS, MLA_NOPE + MLA_ROPE)
        w_q = jnp.concatenate([w_q3[:, :, :MLA_NOPE], _spread_rope_columns(w_q3[:, :, MLA_NOPE:], 2)],
                              axis=2).reshape(Q_LORA, MLA_HEADS * MLA_QK_PAD)
        w_kv3 = w_ukv[l].reshape(KV_LORA, MLA_HEADS, MLA_NOPE + MLA_V)
        w_kv = jnp.concatenate([w_kv3[:, :, :MLA_NOPE].reshape(KV_LORA, -1),
                                w_kv3[:, :, MLA_NOPE:].reshape(KV_LORA, -1)], axis=1).astype(BF16)
        qcol_s, col_qlat, col_kvlat = 0, SWA_Q_WIDTH // Q_LORA, (SWA_Q_WIDTH + Q_LORA) // KV_LORA
        col_ks = (SWA_Q_WIDTH + Q_LORA + KV_LORA) // SWA_KV_WIDTH
        col_vs = col_ks + 1

        b_ada3 = b_ada.reshape(depth, 1, -1)
        mod_in = _modulation(c_pad, w_ada, b_ada3, l, 2 * d)[:batch].reshape(batch, 2, d)

        proj, h = _produce_matmul([x2], mod_in, [pre_norm_g[l, 0].reshape(1, d)], w_main, seq,
                                  produce=_token_mixer_input, keep_lhs=True, tn=512, name="mm_in")
        kpe = _matmul_rope(h, w_rope, cos_t, sin_t)
        q = _matmul_norm(proj, col_qlat, Q_LORA, q_norm_g[l].reshape(1, -1), w_q,
                         rope_tables=(cos_t, sin_t), tn=2048, name="mm_uq")
        kv = _matmul_norm(proj, col_kvlat, KV_LORA, kv_norm_g[l].reshape(1, -1), w_kv, tn=2048, name="mm_ukv")
        att_a, w_gate_b, w_pa_b, w_pb_b, mod_rest = _mla_attention(
            q, kv, kpe, batch, seq, riders=[("cast", w_gate, l), ("cast", w_proj_a, l), ("cast", w_proj_b, l),
                                            ("mod", c_pad, w_ada, b_ada3, l, 2 * d)])
        mod3 = jnp.concatenate([mod_in, mod_rest[:batch].reshape(batch, 4, d)], axis=1)
        att_b, w_out_b = _swa_attention(proj, swa_sinks[l].reshape(1, -1).astype(F32), bias_tab, batch, seq,
                                        qcol_s, col_ks, col_vs, riders=[("cast", w_out, l)])
        mix, w_up_b = _gated_mix(h, att_a, att_b, w_gate_b, w_pa_b, w_pb_b, riders=[("cast", w_ff_up, l)])
        (y,) = _matmul(mix, w_out_b, name="mm_out")
        post_g1 = post_norm_g[l, 0].reshape(1, d)

        u, w_down_b = _produce_matmul([x2, y], mod3, [post_g1, pre_norm_g[l, 1].reshape(1, d)], w_up_b, seq,
                                      produce=_channel_mixer_input, act="relu2", riders=[("cast", w_ff_down, l)],
                                      name="mm_up")
        y2 = _matmul_ksplit(u, w_down_b, tk=4096, name="mm_down")
        x2 = _post2(x2, y, y2, mod3, post_g1, post_norm_g[l, 1].reshape(1, d), seq)

    return x2.reshape(batch, seq, d)
```

```python
import functools
import math

import numpy as np
import jax
import jax.numpy as jnp
from jax import lax
from jax.experimental import pallas as pl
from jax.experimental.pallas import tpu as pltpu

F32 = jnp.float32
BF16 = jnp.bfloat16

D_MODEL = 4096
CHUNK = 64
MLA_HEADS = 16
MLA_NOPE = 128
MLA_ROPE = 64
MLA_V = 128
Q_LORA = 1024
KV_LORA = 512
ROPE_THETA = 10000.0
SWA_HEADS = 32
SWA_KV_HEADS = 4
SWA_HEAD_DIM = 64
SWA_GROUP = SWA_HEADS // SWA_KV_HEADS
WINDOW = 128
SWA_BLOCK = 128
NUM_BUCKETS = 32
MAX_DISTANCE = 128
EPS = 1e-6
NEG = -1e30
LOG2E = math.log2(math.e)

LANES = 128
ROW_CHUNK = 16
ROW_UNROLL = 4
MLA_QK_PAD = 256
SWA_Q_WIDTH = SWA_HEADS * SWA_HEAD_DIM
SWA_KV_WIDTH = SWA_KV_HEADS * SWA_HEAD_DIM
VMEM_CAP_BYTES = 56 * 1024 * 1024


def _vmem_limit(block_bytes, scratch_bytes=0, temp_bytes=0):
    need = 2 * block_bytes + scratch_bytes + temp_bytes + (4 << 20)
    return int(min(max(need, 16 << 20), VMEM_CAP_BYTES))


def _params(sem, vmem):
    return pltpu.CompilerParams(dimension_semantics=sem, vmem_limit_bytes=vmem)


def _rms(x):
    return x * lax.rsqrt(jnp.mean(x * x, axis=-1, keepdims=True) + EPS)


def _rope_half_swap(x, cos_t, sin_t):
    return x * cos_t + pltpu.roll(x, LANES // 2, 1) * sin_t


def _spread_rope_columns(w, axis):
    half = MLA_ROPE // 2
    x1, x2 = jnp.split(w, 2, axis=axis)
    zero = jnp.zeros_like(x1)
    assert x1.shape[axis] == half and 4 * half == LANES
    return jnp.concatenate([x1, zero, x2, zero], axis=axis)


def _mod_body(c_ref, w_ref, b_ref, o_ref):
    c = c_ref[...]
    ca = (c * jax.nn.sigmoid(c)).astype(BF16)
    o_ref[...] = jnp.dot(ca, w_ref[...].astype(BF16), preferred_element_type=F32) + b_ref[...]


def _modulation(c_pad, w_ada, b_ada, layer, n_cols, tn=512):
    m, k = c_pad.shape
    blocks = m * k * 4 + k * tn * 4 + tn * 4 + m * tn * 4
    return pl.pallas_call(
        _mod_body,
        out_shape=jax.ShapeDtypeStruct((m, n_cols), F32),
        grid=(n_cols // tn,),
        in_specs=[pl.BlockSpec((m, k), lambda j: (0, 0)),
                  pl.BlockSpec((None, k, tn), lambda j: (layer, 0, j)),
                  pl.BlockSpec((None, 1, tn), lambda j: (layer, 0, j))],
        out_specs=pl.BlockSpec((m, tn), lambda j: (0, j)),
        compiler_params=_params(("arbitrary",), _vmem_limit(blocks, temp_bytes=k * tn * 2)),
        name="mod",
    )(c_pad, w_ada, b_ada)


def _cast_body(src_ref, dst_ref):
    dst_ref[...] = src_ref[...].astype(dst_ref.dtype)


def _with_riders(body, n_in, n_out, plan):
    if not plan:
        return body
    r_in = sum(k_in for _, k_in, _ in plan)
    r_out = sum(k_out for _, _, k_out in plan)

    def wrapped(*refs):
        ins = refs[:n_in]
        rider_ins = refs[n_in:n_in + r_in]
        outs = refs[n_in + r_in:n_in + r_in + n_out]
        rider_outs = refs[n_in + r_in + n_out:n_in + r_in + n_out + r_out]
        scratch = refs[n_in + r_in + n_out + r_out:]
        a = b = 0
        for fn, k_in, k_out in plan:
            fn(*rider_ins[a:a + k_in], *rider_outs[b:b + k_out])
            a, b = a + k_in, b + k_out
        body(*ins, *outs, *scratch)

    return wrapped


def _rider_specs(riders, n_steps, step_of):
    in_specs, out_specs, out_shapes, plan, args, block_bytes = [], [], [], [], [], 0
    for kind, *rest in riders:
        if kind == "cast":
            w3, layer = rest
            _, r, cols = w3.shape
            tr = r // n_steps
            assert tr * n_steps == r and tr % 16 == 0, (w3.shape, n_steps)
            in_specs.append(pl.BlockSpec((None, tr, cols), lambda *g, layer=layer: (layer, step_of(*g), 0)))
            out_specs.append(pl.BlockSpec((tr, cols), lambda *g: (step_of(*g), 0)))
            out_shapes.append(jax.ShapeDtypeStruct((r, cols), BF16))
            plan.append((_cast_body, 1, 1))
            args.append(w3)
            block_bytes += tr * cols * (4 + 2)
        else:
            assert kind == "mod", kind
            c_pad, w_ada, b_ada, layer, c0 = rest
            m, k = c_pad.shape
            n_cols = w_ada.shape[2] - c0
            tc = n_cols // n_steps
            assert tc * n_steps == n_cols and tc % LANES == 0 and c0 % tc == 0, (w_ada.shape, c0, n_steps)
            col = lambda *g, first=c0 // tc: first + step_of(*g)
            in_specs += [pl.BlockSpec((m, k), lambda *g: (0, 0)),
                         pl.BlockSpec((None, k, tc), lambda *g, layer=layer, col=col: (layer, 0, col(*g))),
                         pl.BlockSpec((None, 1, tc), lambda *g, layer=layer, col=col: (layer, 0, col(*g)))]
            out_specs.append(pl.BlockSpec((m, tc), lambda *g: (0, step_of(*g))))
            out_shapes.append(jax.ShapeDtypeStruct((m, n_cols), F32))
            plan.append((_mod_body, 3, 1))
            args += [c_pad, w_ada, b_ada]
            block_bytes += m * k * 4 + k * tc * (4 + 2) + tc * 4 + m * tc * 4
    return in_specs, out_specs, out_shapes, block_bytes, plan, args


def _token_mixer_input(src, mod_ref, vec, rows):
    (x_ref,), (g_ref,) = src, vec
    return _rms(x_ref[rows, :]) * (g_ref[...] * (1.0 + mod_ref[1:2, :])) + mod_ref[0:1, :]


def _channel_mixer_input(src, mod_ref, vec, rows):
    (x_ref, y_ref), (pg_ref, g_ref) = src, vec
    x1 = x_ref[rows, :] + _rms(y_ref[rows, :].astype(F32)) * (mod_ref[2:3, :] * pg_ref[...])
    return _rms(x1) * (g_ref[...] * (1.0 + mod_ref[4:5, :])) + mod_ref[3:4, :]


def _produce_matmul_body(*refs, n_src, n_vec, produce, keep_lhs, rope_side, act):
    src, mod_ref = refs[:n_src], refs[n_src]
    vec, w_ref = refs[n_src + 1:n_src + 1 + n_vec], refs[n_src + 1 + n_vec]
    n_in = n_src + 2 + n_vec + 3 * rope_side
    side_w_ref, cos_ref, sin_ref = refs[n_in - 3:n_in] if rope_side else (None, None, None)
    outs = refs[n_in:]
    o_ref, lhs_out_ref = outs[0], (outs[1] if keep_lhs else None)
    side_out_ref = outs[1 + keep_lhs] if rope_side else None
    buf_a, buf_b = outs[-2:]
    i = pl.program_id(0)
    j = pl.program_id(1)
    slab = src[0].shape[0]

    def step(build_ref, ready_ref):
        for c in range(slab // ROW_CHUNK):
            rows = slice(c * ROW_CHUNK, (c + 1) * ROW_CHUNK)
            h = produce(src, mod_ref, vec, rows).astype(build_ref.dtype)
            dst = pl.ds(pl.multiple_of(j * slab + c * ROW_CHUNK, ROW_CHUNK), ROW_CHUNK)
            build_ref[dst, :] = h
            if keep_lhs:
                lhs_out_ref[rows, :] = h
        if ready_ref is None:
            return
        acc = jnp.dot(ready_ref[...], w_ref[...], preferred_element_type=F32)
        if act == "relu2":
            acc = jnp.maximum(acc, 0.0)
            acc = acc * acc
        o_ref[...] = acc.astype(o_ref.dtype)
        if rope_side:
            @pl.when(j == 0)
            def _():
                side = jnp.dot(ready_ref[...], side_w_ref[...], preferred_element_type=F32)
                side_out_ref[...] = _rope_half_swap(side, cos_ref[...], sin_ref[...]).astype(side_out_ref.dtype)

    @pl.when(i == 0)
    def _():
        step(buf_a, None)

    @pl.when(jnp.logical_and(i > 0, i % 2 == 0))
    def _():
        step(buf_a, buf_b)

    @pl.when(i % 2 == 1)
    def _():
        step(buf_b, buf_a)


def _produce_matmul(src, mod3, vec, w, seq, *, produce, keep_lhs=False, rope_side=None, act=None,
                    tm=1024, tn=1024, riders=(), name):
    m, d = src[0].shape
    n = w.shape[1]
    gm, gn = m // tm, n // tn
    slab = tm // gn
    assert slab * gn == tm and slab % ROW_CHUNK == 0 and seq % tm == 0
    build_tile = lambda i: jnp.minimum(i, gm - 1)
    ready_tile = lambda i: jnp.maximum(i - 1, 0)
    ready_col = lambda i, j: jnp.where(i == 0, 0, j)
    slab_of = lambda i, j: jnp.where(i < gm, i * gn + j, gm * gn - 1)
    slab_spec = pl.BlockSpec((slab, d), lambda i, j: (slab_of(i, j), 0))
    vec_spec = pl.BlockSpec((1, d), lambda i, j: (0, 0))
    r_in, r_out, r_shapes, r_bytes, r_plan, r_args = _rider_specs(
        riders, gm * gn, lambda i, j: ready_tile(i) * gn + ready_col(i, j))
    src_bytes = sum(slab * d * a.dtype.itemsize for a in src)
    blocks = src_bytes + (6 + len(vec)) * d * 4 + d * tn * 2 + tm * tn * 2 + keep_lhs * slab * d * 2 + r_bytes
    side_in, side_out, side_shapes, side_args = [], [], [], []
    if rope_side is not None:
        tile_rows = pl.BlockSpec((tm, LANES), lambda i, j: (ready_tile(i), 0))
        side_in = [pl.BlockSpec((d, LANES), lambda i, j: (0, 0)), tile_rows, tile_rows]
        side_out, side_shapes, side_args = [tile_rows], [jax.ShapeDtypeStruct((m, LANES), BF16)], list(rope_side)
        blocks += d * LANES * 2 + tm * LANES * (4 + 4 + 2)
    body = functools.partial(_produce_matmul_body, n_src=len(src), n_vec=len(vec), produce=produce,
                             keep_lhs=keep_lhs, rope_side=rope_side is not None, act=act)
    return pl.pallas_call(
        _with_riders(body, len(src) + len(vec) + 2 + len(side_in), 1 + keep_lhs + len(side_out), r_plan),
        out_shape=([jax.ShapeDtypeStruct((m, n), BF16)] + [jax.ShapeDtypeStruct((m, d), BF16)] * keep_lhs
                   + side_shapes + r_shapes),
        grid=(gm + 1, gn),
        in_specs=([slab_spec] * len(src)
                  + [pl.BlockSpec((None, mod3.shape[1], d), lambda i, j: (build_tile(i) * tm // seq, 0, 0))]
                  + [vec_spec] * len(vec)
                  + [pl.BlockSpec((d, tn), lambda i, j: (0, ready_col(i, j)))] + side_in + r_in),
        out_specs=([pl.BlockSpec((tm, tn), lambda i, j: (ready_tile(i), ready_col(i, j)))]
                   + [slab_spec] * keep_lhs + side_out + r_out),
        scratch_shapes=[pltpu.VMEM((tm, d), BF16), pltpu.VMEM((tm, d), BF16)],
        compiler_params=_params(("arbitrary", "arbitrary"),
                                _vmem_limit(blocks, scratch_bytes=2 * tm * d * 2, temp_bytes=2 * tm * tn * 4)),
        name=name,
    )(*src, mod3, *vec, w, *side_args, *r_args)


def _post2_body(x_ref, y_ref, y2_ref, mod_ref, pg1_ref, pg2_ref, o_ref):
    def chunk(r, carry):
        rows = pl.ds(pl.multiple_of(r * ROW_CHUNK, ROW_CHUNK), ROW_CHUNK)
        x1 = x_ref[rows, :] + _rms(y_ref[rows, :].astype(F32)) * (mod_ref[2:3, :] * pg1_ref[...])
        o_ref[rows, :] = x1 + _rms(y2_ref[rows, :].astype(F32)) * (mod_ref[5:6, :] * pg2_ref[...])
        return carry

    lax.fori_loop(0, x_ref.shape[0] // ROW_CHUNK, chunk, 0, unroll=ROW_UNROLL)


def _post2(x2, y, y2, mod3, post_g1, post_g2, seq, tr=512):
    rows, d = x2.shape
    per_b = seq // tr
    blocks = tr * d * (4 + 2 + 2 + 4) + 8 * d * 4
    chunk_temps = 8 * ROW_UNROLL * ROW_CHUNK * d * 4
    row = pl.BlockSpec((tr, d), lambda i: (i, 0))
    vec = pl.BlockSpec((1, d), lambda i: (0, 0))
    return pl.pallas_call(
        _post2_body,
        out_shape=jax.ShapeDtypeStruct((rows, d), F32),
        grid=(rows // tr,),
        in_specs=[row, row, row, pl.BlockSpec((None, 6, d), lambda i: (i // per_b, 0, 0)), vec, vec],
        out_specs=row,
        compiler_params=_params(("arbitrary",), _vmem_limit(blocks, temp_bytes=chunk_temps)),
        name="post2",
    )(x2, y, y2, mod3, post_g1, post_g2)


def _mm_body(x_ref, w_ref, o_ref):
    o_ref[...] = jnp.dot(x_ref[...], w_ref[...], preferred_element_type=F32).astype(o_ref.dtype)


def _matmul(x, w, *, tm=1024, tn=1024, riders=(), name):
    m, k = x.shape
    n = w.shape[1]
    tn = min(tn, n)
    gm, gn = m // tm, n // tn
    r_in, r_out, r_shapes, r_bytes, r_plan, r_args = _rider_specs(riders, gm * gn, lambda i, j: i * gn + j)
    blocks = tm * k * 2 + k * tn * 2 + tm * tn * 2 + r_bytes
    return pl.pallas_call(
        _with_riders(_mm_body, 2, 1, r_plan),
        out_shape=[jax.ShapeDtypeStruct((m, n), BF16)] + r_shapes,
        grid=(gm, gn),
        in_specs=[pl.BlockSpec((tm, k), lambda i, j: (i, 0)),
                  pl.BlockSpec((k, tn), lambda i, j: (0, j))] + r_in,
        out_specs=[pl.BlockSpec((tm, tn), lambda i, j: (i, j))] + r_out,
        compiler_params=_params(("arbitrary", "arbitrary"), _vmem_limit(blocks, temp_bytes=2 * tm * tn * 4)),
        name=name,
    )(x, w, *r_args)


def _mm_norm_body(x_ref, g_ref, w_ref, *rest, rope):
    xn_ref = rest[-1]

    @pl.when(pl.program_id(1) == 0)
    def _():
        xn_ref[...] = (_rms(x_ref[...].astype(F32)) * g_ref[...]).astype(BF16)

    acc = jnp.dot(xn_ref[...], w_ref[...], preferred_element_type=F32)
    if not rope:
        o_ref = rest[0]
        o_ref[...] = acc.astype(o_ref.dtype)
        return
    cos_ref, sin_ref, o_ref = rest[:3]
    cos_t = cos_ref[...]
    sin_t = sin_ref[...]
    for hd in range(acc.shape[1] // MLA_QK_PAD):
        c0 = hd * MLA_QK_PAD
        o_ref[:, c0:c0 + MLA_NOPE] = acc[:, c0:c0 + MLA_NOPE].astype(o_ref.dtype)
        pe = _rope_half_swap(acc[:, c0 + MLA_NOPE:c0 + MLA_QK_PAD], cos_t, sin_t)
        o_ref[:, c0 + MLA_NOPE:c0 + MLA_QK_PAD] = pe.astype(o_ref.dtype)


def _matmul_norm(x, xcol, kdim, g, w, *, rope_tables=None, tm=1024, tn=1024, name):
    m = x.shape[0]
    n = w.shape[1]
    blocks = tm * kdim * 2 + kdim * 4 + kdim * tn * 2 + tm * tn * 2
    in_specs = [pl.BlockSpec((tm, kdim), lambda i, j: (i, xcol)),
                pl.BlockSpec((1, kdim), lambda i, j: (0, 0)),
                pl.BlockSpec((kdim, tn), lambda i, j: (0, j))]
    args = [x, g, w]
    if rope_tables is not None:
        in_specs += [pl.BlockSpec((tm, LANES), lambda i, j: (i, 0))] * 2
        args += list(rope_tables)
        blocks += 2 * tm * LANES * 4
    return pl.pallas_call(
        functools.partial(_mm_norm_body, rope=rope_tables is not None),
        out_shape=jax.ShapeDtypeStruct((m, n), BF16),
        grid=(m // tm, n // tn),
        in_specs=in_specs,
        out_specs=pl.BlockSpec((tm, tn), lambda i, j: (i, j)),
        scratch_shapes=[pltpu.VMEM((tm, kdim), BF16)],
        compiler_params=_params(("arbitrary", "arbitrary"),
                                _vmem_limit(blocks, scratch_bytes=tm * kdim * 2,
                                            temp_bytes=2 * tm * tn * 4 + 3 * tm * kdim * 4)),
        name=name,
    )(*args)


def _mm_acc_body(x_ref, w_ref, o_ref, acc_ref):
    kk = pl.program_id(2)

    @pl.when(kk == 0)
    def _():
        acc_ref[...] = jnp.zeros_like(acc_ref)

    acc_ref[...] += jnp.dot(x_ref[...], w_ref[...], preferred_element_type=F32)

    @pl.when(kk == pl.num_programs(2) - 1)
    def _():
        o_ref[...] = acc_ref[...].astype(o_ref.dtype)


def _matmul_ksplit(x, w, *, tm=1024, tn=1024, tk=2048, name):
    m, k = x.shape
    n = w.shape[1]
    blocks = tm * tk * 2 + tk * tn * 2 + tm * tn * 2
    return pl.pallas_call(
        _mm_acc_body,
        out_shape=jax.ShapeDtypeStruct((m, n), BF16),
        grid=(m // tm, n // tn, k // tk),
        in_specs=[pl.BlockSpec((tm, tk), lambda i, j, kk: (i, kk)),
                  pl.BlockSpec((tk, tn), lambda i, j, kk: (kk, j))],
        out_specs=pl.BlockSpec((tm, tn), lambda i, j, kk: (i, j)),
        scratch_shapes=[pltpu.VMEM((tm, tn), F32)],
        compiler_params=_params(("arbitrary", "arbitrary", "arbitrary"),
                                _vmem_limit(blocks, scratch_bytes=tm * tn * 4, temp_bytes=2 * tm * tn * 4)),
        name=name,
    )(x, w)


def _mix_body(h_ref, a_ref, b_ref, wga_ref, wgb_ref, wa_ref, wb_ref, o_ref):
    h = h_ref[...]
    ga = jax.nn.sigmoid(jnp.dot(h, wga_ref[...], preferred_element_type=F32))
    t = ga * jnp.dot(a_ref[...], wa_ref[...], preferred_element_type=F32)
    gb = jax.nn.sigmoid(jnp.dot(h, wgb_ref[...], preferred_element_type=F32))
    t = t + gb * jnp.dot(b_ref[...], wb_ref[...], preferred_element_type=F32)
    o_ref[...] = t.astype(o_ref.dtype)


def _gated_mix(h, a, b, w_gate, w_pa, w_pb, *, tm=512, tn=512, riders=()):
    m, d = h.shape
    ka = a.shape[1]
    kb = b.shape[1]
    n = w_pa.shape[1]
    gm, nb = m // tm, n // tn
    r_in, r_out, r_shapes, r_bytes, r_plan, r_args = _rider_specs(riders, gm * nb, lambda i, j: i * nb + j)
    blocks = tm * (d + ka + kb) * 2 + (2 * d + ka + kb) * tn * 2 + tm * tn * 2 + r_bytes
    return pl.pallas_call(
        _with_riders(_mix_body, 7, 1, r_plan),
        out_shape=[jax.ShapeDtypeStruct((m, n), BF16)] + r_shapes,
        grid=(gm, nb),
        in_specs=[pl.BlockSpec((tm, d), lambda i, j: (i, 0)),
                  pl.BlockSpec((tm, ka), lambda i, j: (i, 0)),
                  pl.BlockSpec((tm, kb), lambda i, j: (i, 0)),
                  pl.BlockSpec((d, tn), lambda i, j: (0, j)),
                  pl.BlockSpec((d, tn), lambda i, j: (0, j + nb)),
                  pl.BlockSpec((ka, tn), lambda i, j: (0, j)),
                  pl.BlockSpec((kb, tn), lambda i, j: (0, j))] + r_in,
        out_specs=[pl.BlockSpec((tm, tn), lambda i, j: (i, j))] + r_out,
        compiler_params=_params(("arbitrary", "arbitrary"), _vmem_limit(blocks, temp_bytes=4 * tm * tn * 4)),
        name="mix",
    )(h, a, b, w_gate, w_gate, w_pa, w_pb, *r_args)


def _mla_body(q_ref, kn_ref, kp_ref, v_ref, o_ref, *, seq, tq, tk):
    c = (MLA_NOPE + MLA_ROPE) ** -0.5 * math.log2(math.e)
    ones = jnp.ones((tk, LANES), BF16)
    row = lax.broadcasted_iota(jnp.int32, (tq, tk), 0)
    col = lax.broadcasted_iota(jnp.int32, (tq, tk), 1)
    for qi in range(seq // tq):
        q = q_ref[qi * tq:(qi + 1) * tq, :]
        n_kb = -(-((qi + 1) * tq) // tk)
        m = None
        acc = None
        for kj in range(n_kb):
            ks = slice(kj * tk, (kj + 1) * tk)
            kf = jnp.concatenate([kn_ref[ks, :], kp_ref[ks, :]], axis=1)
            s = lax.dot_general(q, kf, (((1,), (1,)), ((), ())), preferred_element_type=F32)
            if (kj + 1) * tk > qi * tq + CHUNK:
                visible = (col + kj * tk) // CHUNK <= (row + qi * tq) // CHUNK
                s = jnp.where(visible, s, NEG)
            bm = jnp.max(s, axis=-1, keepdims=True)
            v_ext = jnp.concatenate([v_ref[ks, :], ones], axis=1)
            if m is None:
                m = bm
                p = jnp.exp2((s - m) * c)
                acc = jnp.dot(p.astype(BF16), v_ext, preferred_element_type=F32)
            else:
                m_new = jnp.maximum(m, bm)
                alpha = jnp.exp2((m - m_new) * c)
                p = jnp.exp2((s - m_new) * c)
                acc = alpha * acc + jnp.dot(p.astype(BF16), v_ext, preferred_element_type=F32)
                m = m_new
        o_ref[qi * tq:(qi + 1) * tq, :] = (acc[:, :MLA_V] / acc[:, MLA_V:]).astype(o_ref.dtype)


def _mla_attention(q, kv, kpe, batch, seq, tq=256, tk=256, riders=()):
    rows = q.shape[0]
    r_in, r_out, r_shapes, r_bytes, r_plan, r_args = _rider_specs(
        riders, batch * MLA_HEADS, lambda b, h: b * MLA_HEADS + h)
    blocks = seq * MLA_QK_PAD * 2 + 3 * seq * LANES * 2 + seq * MLA_V * 2 + r_bytes
    return pl.pallas_call(
        _with_riders(functools.partial(_mla_body, seq=seq, tq=tq, tk=tk), 4, 1, r_plan),
        out_shape=[jax.ShapeDtypeStruct((rows, MLA_HEADS * MLA_V), BF16)] + r_shapes,
        grid=(batch, MLA_HEADS),
        in_specs=[pl.BlockSpec((seq, MLA_QK_PAD), lambda b, h: (b, h)),
                  pl.BlockSpec((seq, MLA_NOPE), lambda b, h: (b, h)),
                  pl.BlockSpec((seq, LANES), lambda b, h: (b, 0)),
                  pl.BlockSpec((seq, MLA_V), lambda b, h: (b, MLA_HEADS + h))] + r_in,
        out_specs=[pl.BlockSpec((seq, MLA_V), lambda b, h: (b, h))] + r_out,
        compiler_params=_params(("arbitrary", "arbitrary"), _vmem_limit(blocks, temp_bytes=8 << 20)),
        name="mla",
    )(q, kv, kpe, kv, *r_args)


def _t5_bucket_table():
    iq = np.arange(SWA_BLOCK)
    ik = np.arange(2 * SWA_BLOCK) - SWA_BLOCK
    rel = ik[None, :] - iq[:, None]
    nb = NUM_BUCKETS // 2
    max_exact = nb // 2
    ret = np.where(rel > 0, nb, 0)
    n = np.abs(rel)
    nf = np.maximum(n, 1).astype(np.float64)
    large = max_exact + (np.log(nf / max_exact) / math.log(MAX_DISTANCE / max_exact)
                         * (nb - max_exact)).astype(np.int32)
    large = np.minimum(large, nb - 1)
    return (ret + np.where(n < max_exact, n, large)).astype(np.int32)


def _bias_body(rb_ref, bucket_ref, o_ref):
    hd = pl.program_id(0)
    bk = bucket_ref[...]
    tab = jnp.zeros(bk.shape, F32)
    for b in range(NUM_BUCKETS):
        tab = jnp.where(bk == b, rb_ref[b, hd] * LOG2E, tab)
    col = lax.broadcasted_iota(jnp.int32, bk.shape, 1)
    qc = lax.broadcasted_iota(jnp.int32, bk.shape, 0) // CHUNK
    kc = col // CHUNK - SWA_BLOCK // CHUNK
    valid = jnp.logical_and(kc <= qc, kc >= qc - WINDOW // CHUNK)
    o_ref[1] = jnp.where(valid, tab, NEG)
    o_ref[0] = jnp.where(jnp.logical_and(valid, col >= SWA_BLOCK), tab, NEG)


def _bias_table(rel_bias):
    bucket = jnp.asarray(_t5_bucket_table())
    return pl.pallas_call(
        _bias_body,
        out_shape=jax.ShapeDtypeStruct((2, SWA_HEADS, SWA_BLOCK, 2 * SWA_BLOCK), F32),
        grid=(SWA_HEADS,),
        in_specs=[pl.BlockSpec(memory_space=pltpu.SMEM),
                  pl.BlockSpec((SWA_BLOCK, 2 * SWA_BLOCK), lambda h: (0, 0))],
        out_specs=pl.BlockSpec((2, None, SWA_BLOCK, 2 * SWA_BLOCK), lambda h: (0, h, 0, 0)),
        compiler_params=_params(("arbitrary",), 16 << 20),
        name="swa_bias",
    )(rel_bias, bucket)


def _swa_body(sink_ref, q_ref, kp_ref, kc_ref, vp_ref, vc_ref, bias_ref, o_ref, *, n_sub):
    c = SWA_HEAD_DIM ** -0.5 * LOG2E
    blk = SWA_BLOCK
    first_table = jnp.minimum(pl.program_id(1), 1)
    low = lax.broadcasted_iota(jnp.int32, ((n_sub + 1) * blk, LANES), 1) < SWA_HEAD_DIM
    low_q = lax.broadcasted_iota(jnp.int32, (blk, LANES), 1) < SWA_HEAD_DIM

    def both_halves(pair, want_low):
        other = pltpu.roll(pair, SWA_HEAD_DIM, 1)
        return jnp.where(low, pair, other) if want_low else jnp.where(low, other, pair)

    k2, v2 = [], []
    for j in range(SWA_KV_HEADS // 2):
        cols = slice(j * LANES, (j + 1) * LANES)
        kpair = jnp.concatenate([kp_ref[:, cols], kc_ref[:, cols]], axis=0)
        vpair = jnp.concatenate([vp_ref[:, cols], vc_ref[:, cols]], axis=0)
        k2 += [both_halves(kpair, True), both_halves(kpair, False)]
        v2 += [both_halves(vpair, True), both_halves(vpair, False)]

    zero = jnp.zeros((blk, LANES), BF16)
    for r in range(n_sub):
        q_rows = slice(r * blk, (r + 1) * blk)
        band = slice(r * blk, (r + 2) * blk)
        table = first_table if r == 0 else 1
        for p in range(SWA_HEADS // 2):
            g = (2 * p) // SWA_GROUP
            qp = q_ref[q_rows, p * LANES:(p + 1) * LANES]
            qs = jnp.concatenate([jnp.where(low_q, qp, zero), jnp.where(low_q, zero, qp)], axis=0)
            s = lax.dot_general(qs, k2[g][band], (((1,), (1,)), ((), ())), preferred_element_type=F32)
            t = s * c + bias_ref[table, 2 * p:2 * p + 2].reshape(2 * blk, 2 * blk)
            es, dens = [], []
            for half in range(2):
                th = t[half * blk:(half + 1) * blk]
                sk = sink_ref[0, 2 * p + half] * LOG2E
                m = jnp.maximum(jnp.max(th, axis=-1, keepdims=True), sk)
                e = jnp.exp2(th - m)
                es.append(e.astype(BF16))
                dens.append(jnp.sum(e, axis=-1, keepdims=True) + jnp.exp2(sk - m))
            o2 = jnp.dot(jnp.concatenate(es, axis=0), v2[g][band], preferred_element_type=F32)
            o_ref[q_rows, p * LANES:(p + 1) * LANES] = jnp.where(
                low_q, o2[:blk] / dens[0], o2[blk:] / dens[1]).astype(o_ref.dtype)


def _swa_attention(proj, sinks, bias_tab, batch, seq, qcol, kcol, vcol, n_sub=4, riders=()):
    rows = proj.shape[0]
    blk = SWA_BLOCK
    steps = seq // (n_sub * blk)
    cur = lambda b, n: b * steps + n
    prev = lambda b, n: (b * steps + n) * n_sub - jnp.minimum(n, 1)
    r_in, r_out, r_shapes, r_bytes, r_plan, r_args = _rider_specs(riders, batch * steps, cur)
    blocks = (2 * n_sub * blk * SWA_Q_WIDTH * 2 + 2 * (n_sub + 1) * blk * SWA_KV_WIDTH * 2
              + 2 * SWA_HEADS * blk * 2 * blk * 4 + r_bytes)
    return pl.pallas_call(
        _with_riders(functools.partial(_swa_body, n_sub=n_sub), 7, 1, r_plan),
        out_shape=[jax.ShapeDtypeStruct((rows, SWA_Q_WIDTH), BF16)] + r_shapes,
        grid=(batch, steps),
        in_specs=[pl.BlockSpec(memory_space=pltpu.SMEM),
                  pl.BlockSpec((n_sub * blk, SWA_Q_WIDTH), lambda b, n: (cur(b, n), qcol)),
                  pl.BlockSpec((blk, SWA_KV_WIDTH), lambda b, n: (prev(b, n), kcol)),
                  pl.BlockSpec((n_sub * blk, SWA_KV_WIDTH), lambda b, n: (cur(b, n), kcol)),
                  pl.BlockSpec((blk, SWA_KV_WIDTH), lambda b, n: (prev(b, n), vcol)),
                  pl.BlockSpec((n_sub * blk, SWA_KV_WIDTH), lambda b, n: (cur(b, n), vcol)),
                  pl.BlockSpec((2, SWA_HEADS, blk, 2 * blk), lambda b, n: (0, 0, 0, 0))] + r_in,
        out_specs=[pl.BlockSpec((n_sub * blk, SWA_Q_WIDTH), lambda b, n: (cur(b, n), 0))] + r_out,
        compiler_params=_params(("arbitrary", "arbitrary"), _vmem_limit(blocks, temp_bytes=8 << 20)),
        name="swa",
    )(sinks, proj, proj, proj, proj, proj, bias_tab, *r_args)


def _rope_tables(positions):
    half = MLA_ROPE // 2
    inv = ROPE_THETA ** (-jnp.arange(half, dtype=F32) * (2.0 / MLA_ROPE))
    one, zero = jnp.ones((half,), F32), jnp.zeros((half,), F32)
    ang = positions.astype(F32).reshape(-1, 1) * jnp.concatenate([inv, zero, inv, zero])
    return (jnp.cos(ang) * jnp.concatenate([one, zero, one, zero]),
            jnp.sin(ang) * jnp.concatenate([-one, zero, one, zero]))


def kernel(x, c, positions, w_ada, b_ada, pre_norm_g, post_norm_g, w_in, q_norm_g, kv_norm_g, w_uq, w_ukv,
           swa_sinks, rel_bias, w_gate, w_proj_a, w_proj_b, w_out, w_ff_up, w_ff_down):
    batch, seq, d = x.shape
    depth = w_ada.shape[0]
    rows = batch * seq
    x2 = x.reshape(rows, d)

    cos_t, sin_t = _rope_tables(positions)
    bias_tab = _bias_table(rel_bias.astype(F32))
    c_pad = jnp.concatenate([c, jnp.zeros((16 - batch, d), c.dtype)], axis=0)

    o_qlat, o_kvlat, o_rope = 0, Q_LORA, Q_LORA + KV_LORA
    o_qs = o_rope + MLA_ROPE
    o_ks = o_qs + SWA_Q_WIDTH
    o_vs = o_ks + SWA_KV_WIDTH

    for l in range(depth):
        wi = w_in[l].astype(BF16)
        w_main = jnp.concatenate([wi[:, o_qs:o_ks], wi[:, o_qlat:o_rope], wi[:, o_ks:o_vs + SWA_KV_WIDTH]],
                                 axis=1)
        w_rope = _spread_rope_columns(wi[:, o_rope:o_qs], 1)
        w_q3 = w_uq[l].astype(BF16).reshape(Q_LORA, MLA_HEADS, MLA_NOPE + MLA_ROPE)
        w_q = jnp.concatenate([w_q3[:, :, :MLA_NOPE], _spread_rope_columns(w_q3[:, :, MLA_NOPE:], 2)],
                              axis=2).reshape(Q_LORA, MLA_HEADS * MLA_QK_PAD)
        w_kv3 = w_ukv[l].reshape(KV_LORA, MLA_HEADS, MLA_NOPE + MLA_V)
        w_kv = jnp.concatenate([w_kv3[:, :, :MLA_NOPE].reshape(KV_LORA, -1),
                                w_kv3[:, :, MLA_NOPE:].reshape(KV_LORA, -1)], axis=1).astype(BF16)
        qcol_s, col_qlat, col_kvlat = 0, SWA_Q_WIDTH // Q_LORA, (SWA_Q_WIDTH + Q_LORA) // KV_LORA
        col_ks = (SWA_Q_WIDTH + Q_LORA + KV_LORA) // SWA_KV_WIDTH
        col_vs = col_ks + 1

        b_ada3 = b_ada.reshape(depth, 1, -1)
        mod_in = _modulation(c_pad, w_ada, b_ada3, l, 2 * d)[:batch].reshape(batch, 2, d)

        proj, h, kpe = _produce_matmul([x2], mod_in, [pre_norm_g[l, 0].reshape(1, d)], w_main, seq,
                                       produce=_token_mixer_input, keep_lhs=True,
                                       rope_side=(w_rope, cos_t, sin_t), tn=512, name="mm_in")
        q = _matmul_norm(proj, col_qlat, Q_LORA, q_norm_g[l].reshape(1, -1), w_q,
                         rope_tables=(cos_t, sin_t), tn=2048, name="mm_uq")
        kv = _matmul_norm(proj, col_kvlat, KV_LORA, kv_norm_g[l].reshape(1, -1), w_kv, tn=2048, name="mm_ukv")
        att_a, w_gate_b, w_pa_b, w_pb_b, mod_rest = _mla_attention(
            q, kv, kpe, batch, seq, riders=[("cast", w_gate, l), ("cast", w_proj_a, l), ("cast", w_proj_b, l),
                                            ("mod", c_pad, w_ada, b_ada3, l, 2 * d)])
        mod3 = jnp.concatenate([mod_in, mod_rest[:batch].reshape(batch, 4, d)], axis=1)
        att_b, w_out_b = _swa_attention(proj, swa_sinks[l].reshape(1, -1).astype(F32), bias_tab, batch, seq,
                                        qcol_s, col_ks, col_vs, riders=[("cast", w_out, l)])
        mix, w_up_b = _gated_mix(h, att_a, att_b, w_gate_b, w_pa_b, w_pb_b, riders=[("cast", w_ff_up, l)])
        (y,) = _matmul(mix, w_out_b, name="mm_out")
        post_g1 = post_norm_g[l, 0].reshape(1, d)

        u, w_down_b = _produce_matmul([x2, y], mod3, [post_g1, pre_norm_g[l, 1].reshape(1, d)], w_up_b, seq,
                                      produce=_channel_mixer_input, act="relu2", riders=[("cast", w_ff_down, l)],
                                      name="mm_up")
        y2 = _matmul_ksplit(u, w_down_b, tk=4096, name="mm_down")
        x2 = _post2(x2, y, y2, mod3, post_g1, post_norm_g[l, 1].reshape(1, d), seq)

    return x2.reshape(batch, seq, d)
```

```python
import functools
import math

import numpy as np
import jax
import jax.numpy as jnp
from jax import lax
from jax.experimental import pallas as pl
from jax.experimental.pallas import tpu as pltpu

F32 = jnp.float32
BF16 = jnp.bfloat16

CHUNK = 64
MLA_HEADS = 16
MLA_NOPE = 128
MLA_ROPE = 64
MLA_V = 128
Q_LORA = 1024
KV_LORA = 512
ROPE_THETA = 10000.0
SWA_HEADS = 32
SWA_KV_HEADS = 4
SWA_HEAD_DIM = 64
SWA_GROUP = SWA_HEADS // SWA_KV_HEADS
WINDOW = 128
SWA_BLOCK = 128
NUM_BUCKETS = 32
MAX_DISTANCE = 128
EPS = 1e-6
NEG = -1e30
LOG2E = math.log2(math.e)

LANES = 128
ROW_CHUNK = 16
ROW_UNROLL = 4
MLA_QK_PAD = 256
SWA_Q_WIDTH = SWA_HEADS * SWA_HEAD_DIM
SWA_KV_WIDTH = SWA_KV_HEADS * SWA_HEAD_DIM
VMEM_CAP_BYTES = 56 * 1024 * 1024


def _vmem_limit(block_bytes, scratch_bytes=0, temp_bytes=0):
    need = 2 * block_bytes + scratch_bytes + temp_bytes + (4 << 20)
    return int(min(max(need, 16 << 20), VMEM_CAP_BYTES))


def _params(sem, vmem):
    return pltpu.CompilerParams(dimension_semantics=sem, vmem_limit_bytes=vmem)


def _rms(x):
    return x * lax.rsqrt(jnp.mean(x * x, axis=-1, keepdims=True) + EPS)


def _rope_half_swap(x, cos_t, sin_t):
    return x * cos_t + pltpu.roll(x, LANES // 2, 1) * sin_t


def _spread_rope_columns(w, axis):
    half = MLA_ROPE // 2
    x1, x2 = jnp.split(w, 2, axis=axis)
    zero = jnp.zeros_like(x1)
    assert x1.shape[axis] == half and 4 * half == LANES
    return jnp.concatenate([x1, zero, x2, zero], axis=axis)


def _regroup_w_in_body(w_ref, main_ref, rope_ref, *, segments, rope_at):
    for dst, src, width in segments:
        main_ref[:, dst:dst + width] = w_ref[:, src:src + width]
    half = MLA_ROPE // 2
    zero = jnp.zeros((w_ref.shape[0], half), rope_ref.dtype)
    rope_ref[...] = jnp.concatenate([w_ref[:, rope_at:rope_at + half], zero,
                                     w_ref[:, rope_at + half:rope_at + MLA_ROPE], zero], axis=1)


def _regroup_w_in(w, segments, rope_at, tr=256):
    k, n_in = w.shape
    n_out = sum(width for _, _, width in segments)
    blocks = tr * (n_in + n_out + LANES) * 2
    return pl.pallas_call(
        functools.partial(_regroup_w_in_body, segments=segments, rope_at=rope_at),
        out_shape=(jax.ShapeDtypeStruct((k, n_out), w.dtype), jax.ShapeDtypeStruct((k, LANES), w.dtype)),
        grid=(k // tr,),
        in_specs=[pl.BlockSpec((tr, n_in), lambda i: (i, 0))],
        out_specs=(pl.BlockSpec((tr, n_out), lambda i: (i, 0)), pl.BlockSpec((tr, LANES), lambda i: (i, 0))),
        compiler_params=_params(("arbitrary",), _vmem_limit(blocks, temp_bytes=2 * tr * n_in * 2)),
        name="regroup_w_in",
    )(w)


def _mod_body(c_ref, w_ref, b_ref, o_ref):
    c = c_ref[...]
    ca = (c * jax.nn.sigmoid(c)).astype(BF16)
    o_ref[...] = jnp.dot(ca, w_ref[...].astype(BF16), preferred_element_type=F32) + b_ref[...]


def _modulation(c_pad, w_ada, b_ada, layer, n_cols, tn=512):
    m, k = c_pad.shape
    blocks = m * k * 4 + k * tn * 4 + tn * 4 + m * tn * 4
    return pl.pallas_call(
        _mod_body,
        out_shape=jax.ShapeDtypeStruct((m, n_cols), F32),
        grid=(n_cols // tn,),
        in_specs=[pl.BlockSpec((m, k), lambda j: (0, 0)),
                  pl.BlockSpec((None, k, tn), lambda j: (layer, 0, j)),
                  pl.BlockSpec((None, 1, tn), lambda j: (layer, 0, j))],
        out_specs=pl.BlockSpec((m, tn), lambda j: (0, j)),
        compiler_params=_params(("arbitrary",), _vmem_limit(blocks, temp_bytes=k * tn * 2)),
        name="mod",
    )(c_pad, w_ada, b_ada)


def _cast_body(src_ref, dst_ref):
    dst_ref[...] = src_ref[...].astype(dst_ref.dtype)


def _with_riders(body, n_in, n_out, plan):
    if not plan:
        return body
    r_in = sum(k_in for _, k_in, _ in plan)
    r_out = sum(k_out for _, _, k_out in plan)

    def wrapped(*refs):
        ins = refs[:n_in]
        rider_ins = refs[n_in:n_in + r_in]
        outs = refs[n_in + r_in:n_in + r_in + n_out]
        rider_outs = refs[n_in + r_in + n_out:n_in + r_in + n_out + r_out]
        scratch = refs[n_in + r_in + n_out + r_out:]
        a = b = 0
        for fn, k_in, k_out in plan:
            fn(*rider_ins[a:a + k_in], *rider_outs[b:b + k_out])
            a, b = a + k_in, b + k_out
        body(*ins, *outs, *scratch)

    return wrapped


def _rider_specs(riders, n_steps, step_of):
    in_specs, out_specs, out_shapes, plan, args, block_bytes = [], [], [], [], [], 0
    for kind, *rest in riders:
        if kind == "cast":
            w3, layer = rest
            _, r, cols = w3.shape
            tr = r // n_steps
            assert tr * n_steps == r and tr % 16 == 0, (w3.shape, n_steps)
            in_specs.append(pl.BlockSpec((None, tr, cols), lambda *g, layer=layer: (layer, step_of(*g), 0)))
            out_specs.append(pl.BlockSpec((tr, cols), lambda *g: (step_of(*g), 0)))
            out_shapes.append(jax.ShapeDtypeStruct((r, cols), BF16))
            plan.append((_cast_body, 1, 1))
            args.append(w3)
            block_bytes += tr * cols * (4 + 2)
        else:
            assert kind == "mod", kind
            c_pad, w_ada, b_ada, layer, c0 = rest
            m, k = c_pad.shape
            n_cols = w_ada.shape[2] - c0
            tc = n_cols // n_steps
            assert tc * n_steps == n_cols and tc % LANES == 0 and c0 % tc == 0, (w_ada.shape, c0, n_steps)
            col = lambda *g, first=c0 // tc: first + step_of(*g)
            in_specs += [pl.BlockSpec((m, k), lambda *g: (0, 0)),
                         pl.BlockSpec((None, k, tc), lambda *g, layer=layer, col=col: (layer, 0, col(*g))),
                         pl.BlockSpec((None, 1, tc), lambda *g, layer=layer, col=col: (layer, 0, col(*g)))]
            out_specs.append(pl.BlockSpec((m, tc), lambda *g: (0, step_of(*g))))
            out_shapes.append(jax.ShapeDtypeStruct((m, n_cols), F32))
            plan.append((_mod_body, 3, 1))
            args += [c_pad, w_ada, b_ada]
            block_bytes += m * k * 4 + k * tc * (4 + 2) + tc * 4 + m * tc * 4
    return in_specs, out_specs, out_shapes, block_bytes, plan, args


def _token_mixer_input(src, mod_ref, vec, rows):
    (x_ref,), (g_ref,) = src, vec
    return _rms(x_ref[rows, :]) * (g_ref[...] * (1.0 + mod_ref[1:2, :])) + mod_ref[0:1, :]


def _channel_mixer_input(src, mod_ref, vec, rows):
    (x_ref, y_ref), (pg_ref, g_ref) = src, vec
    x1 = x_ref[rows, :] + _rms(y_ref[rows, :].astype(F32)) * (mod_ref[2:3, :] * pg_ref[...])
    return _rms(x1) * (g_ref[...] * (1.0 + mod_ref[4:5, :])) + mod_ref[3:4, :]


def _produce_matmul_body(*refs, n_src, n_vec, produce, keep_lhs, rope_side, act):
    src, mod_ref = refs[:n_src], refs[n_src]
    vec, w_ref = refs[n_src + 1:n_src + 1 + n_vec], refs[n_src + 1 + n_vec]
    n_in = n_src + 2 + n_vec + 3 * rope_side
    side_w_ref, cos_ref, sin_ref = refs[n_in - 3:n_in] if rope_side else (None, None, None)
    outs = refs[n_in:]
    o_ref, lhs_out_ref = outs[0], (outs[1] if keep_lhs else None)
    side_out_ref = outs[1 + keep_lhs] if rope_side else None
    buf_a, buf_b = outs[-2:]
    i = pl.program_id(0)
    j = pl.program_id(1)
    slab = src[0].shape[0]

    def step(build_ref, ready_ref):
        for c in range(slab // ROW_CHUNK):
            rows = slice(c * ROW_CHUNK, (c + 1) * ROW_CHUNK)
            h = produce(src, mod_ref, vec, rows).astype(build_ref.dtype)
            dst = pl.ds(pl.multiple_of(j * slab + c * ROW_CHUNK, ROW_CHUNK), ROW_CHUNK)
            build_ref[dst, :] = h
            if keep_lhs:
                lhs_out_ref[rows, :] = h
        if ready_ref is None:
            return
        acc = jnp.dot(ready_ref[...], w_ref[...], preferred_element_type=F32)
        if act == "relu2":
            acc = jnp.maximum(acc, 0.0)
            acc = acc * acc
        o_ref[...] = acc.astype(o_ref.dtype)
        if rope_side:
            @pl.when(j == 0)
            def _():
                side = jnp.dot(ready_ref[...], side_w_ref[...], preferred_element_type=F32)
                side_out_ref[...] = _rope_half_swap(side, cos_ref[...], sin_ref[...]).astype(side_out_ref.dtype)

    @pl.when(i == 0)
    def _():
        step(buf_a, None)

    @pl.when(jnp.logical_and(i > 0, i % 2 == 0))
    def _():
        step(buf_a, buf_b)

    @pl.when(i % 2 == 1)
    def _():
        step(buf_b, buf_a)


def _produce_matmul(src, mod3, vec, w, seq, *, produce, keep_lhs=False, rope_side=None, act=None,
                    tm=1024, tn=1024, riders=(), name):
    m, d = src[0].shape
    n = w.shape[1]
    gm, gn = m // tm, n // tn
    slab = tm // gn
    assert slab * gn == tm and slab % ROW_CHUNK == 0 and seq % tm == 0
    build_tile = lambda i: jnp.minimum(i, gm - 1)
    ready_tile = lambda i: jnp.maximum(i - 1, 0)
    ready_col = lambda i, j: jnp.where(i == 0, 0, j)
    slab_of = lambda i, j: jnp.where(i < gm, i * gn + j, gm * gn - 1)
    slab_spec = pl.BlockSpec((slab, d), lambda i, j: (slab_of(i, j), 0))
    vec_spec = pl.BlockSpec((1, d), lambda i, j: (0, 0))
    r_in, r_out, r_shapes, r_bytes, r_plan, r_args = _rider_specs(
        riders, gm * gn, lambda i, j: ready_tile(i) * gn + ready_col(i, j))
    src_bytes = sum(slab * d * a.dtype.itemsize for a in src)
    blocks = src_bytes + (6 + len(vec)) * d * 4 + d * tn * 2 + tm * tn * 2 + keep_lhs * slab * d * 2 + r_bytes
    side_in, side_out, side_shapes, side_args = [], [], [], []
    if rope_side is not None:
        tile_rows = pl.BlockSpec((tm, LANES), lambda i, j: (ready_tile(i), 0))
        side_in = [pl.BlockSpec((d, LANES), lambda i, j: (0, 0)), tile_rows, tile_rows]
        side_out, side_shapes, side_args = [tile_rows], [jax.ShapeDtypeStruct((m, LANES), BF16)], list(rope_side)
        blocks += d * LANES * 2 + tm * LANES * (4 + 4 + 2)
    body = functools.partial(_produce_matmul_body, n_src=len(src), n_vec=len(vec), produce=produce,
                             keep_lhs=keep_lhs, rope_side=rope_side is not None, act=act)
    return pl.pallas_call(
        _with_riders(body, len(src) + len(vec) + 2 + len(side_in), 1 + keep_lhs + len(side_out), r_plan),
        out_shape=([jax.ShapeDtypeStruct((m, n), BF16)] + [jax.ShapeDtypeStruct((m, d), BF16)] * keep_lhs
                   + side_shapes + r_shapes),
        grid=(gm + 1, gn),
        in_specs=([slab_spec] * len(src)
                  + [pl.BlockSpec((None, mod3.shape[1], d), lambda i, j: (build_tile(i) * tm // seq, 0, 0))]
                  + [vec_spec] * len(vec)
                  + [pl.BlockSpec((d, tn), lambda i, j: (0, ready_col(i, j)))] + side_in + r_in),
        out_specs=([pl.BlockSpec((tm, tn), lambda i, j: (ready_tile(i), ready_col(i, j)))]
                   + [slab_spec] * keep_lhs + side_out + r_out),
        scratch_shapes=[pltpu.VMEM((tm, d), BF16), pltpu.VMEM((tm, d), BF16)],
        compiler_params=_params(("arbitrary", "arbitrary"),
                                _vmem_limit(blocks, scratch_bytes=2 * tm * d * 2, temp_bytes=2 * tm * tn * 4)),
        name=name,
    )(*src, mod3, *vec, w, *side_args, *r_args)


def _post2_body(x_ref, y_ref, y2_ref, mod_ref, pg1_ref, pg2_ref, o_ref):
    def chunk(r, carry):
        rows = pl.ds(pl.multiple_of(r * ROW_CHUNK, ROW_CHUNK), ROW_CHUNK)
        x1 = x_ref[rows, :] + _rms(y_ref[rows, :].astype(F32)) * (mod_ref[2:3, :] * pg1_ref[...])
        o_ref[rows, :] = x1 + _rms(y2_ref[rows, :].astype(F32)) * (mod_ref[5:6, :] * pg2_ref[...])
        return carry

    lax.fori_loop(0, x_ref.shape[0] // ROW_CHUNK, chunk, 0, unroll=ROW_UNROLL)


def _post2(x2, y, y2, mod3, post_g1, post_g2, seq, tr=512):
    rows, d = x2.shape
    per_b = seq // tr
    blocks = tr * d * (4 + 2 + 2 + 4) + 8 * d * 4
    chunk_temps = 8 * ROW_UNROLL * ROW_CHUNK * d * 4
    row = pl.BlockSpec((tr, d), lambda i: (i, 0))
    vec = pl.BlockSpec((1, d), lambda i: (0, 0))
    return pl.pallas_call(
        _post2_body,
        out_shape=jax.ShapeDtypeStruct((rows, d), F32),
        grid=(rows // tr,),
        in_specs=[row, row, row, pl.BlockSpec((None, 6, d), lambda i: (i // per_b, 0, 0)), vec, vec],
        out_specs=row,
        compiler_params=_params(("arbitrary",), _vmem_limit(blocks, temp_bytes=chunk_temps)),
        name="post2",
    )(x2, y, y2, mod3, post_g1, post_g2)


def _mm_body(x_ref, w_ref, o_ref):
    o_ref[...] = jnp.dot(x_ref[...], w_ref[...], preferred_element_type=F32).astype(o_ref.dtype)


def _matmul(x, w, *, tm=1024, tn=1024, riders=(), name):
    m, k = x.shape
    n = w.shape[1]
    tn = min(tn, n)
    gm, gn = m // tm, n // tn
    r_in, r_out, r_shapes, r_bytes, r_plan, r_args = _rider_specs(riders, gm * gn, lambda i, j: i * gn + j)
    blocks = tm * k * 2 + k * tn * 2 + tm * tn * 2 + r_bytes
    return pl.pallas_call(
        _with_riders(_mm_body, 2, 1, r_plan),
        out_shape=[jax.ShapeDtypeStruct((m, n), BF16)] + r_shapes,
        grid=(gm, gn),
        in_specs=[pl.BlockSpec((tm, k), lambda i, j: (i, 0)),
                  pl.BlockSpec((k, tn), lambda i, j: (0, j))] + r_in,
        out_specs=[pl.BlockSpec((tm, tn), lambda i, j: (i, j))] + r_out,
        compiler_params=_params(("arbitrary", "arbitrary"), _vmem_limit(blocks, temp_bytes=2 * tm * tn * 4)),
        name=name,
    )(x, w, *r_args)


def _mm_norm_body(x_ref, g_ref, w_ref, *rest, rope):
    xn_ref = rest[-1]

    @pl.when(pl.program_id(1) == 0)
    def _():
        xn_ref[...] = (_rms(x_ref[...].astype(F32)) * g_ref[...]).astype(BF16)

    acc = jnp.dot(xn_ref[...], w_ref[...], preferred_element_type=F32)
    if not rope:
        o_ref = rest[0]
        o_ref[...] = acc.astype(o_ref.dtype)
        return
    cos_ref, sin_ref, o_ref = rest[:3]
    cos_t = cos_ref[...]
    sin_t = sin_ref[...]
    for hd in range(acc.shape[1] // MLA_QK_PAD):
        c0 = hd * MLA_QK_PAD
        o_ref[:, c0:c0 + MLA_NOPE] = acc[:, c0:c0 + MLA_NOPE].astype(o_ref.dtype)
        pe = _rope_half_swap(acc[:, c0 + MLA_NOPE:c0 + MLA_QK_PAD], cos_t, sin_t)
        o_ref[:, c0 + MLA_NOPE:c0 + MLA_QK_PAD] = pe.astype(o_ref.dtype)


def _matmul_norm(x, xcol, kdim, g, w, *, rope_tables=None, tm=1024, tn=1024, name):
    m = x.shape[0]
    n = w.shape[1]
    blocks = tm * kdim * 2 + kdim * 4 + kdim * tn * 2 + tm * tn * 2
    in_specs = [pl.BlockSpec((tm, kdim), lambda i, j: (i, xcol)),
                pl.BlockSpec((1, kdim), lambda i, j: (0, 0)),
                pl.BlockSpec((kdim, tn), lambda i, j: (0, j))]
    args = [x, g, w]
    if rope_tables is not None:
        in_specs += [pl.BlockSpec((tm, LANES), lambda i, j: (i, 0))] * 2
        args += list(rope_tables)
        blocks += 2 * tm * LANES * 4
    return pl.pallas_call(
        functools.partial(_mm_norm_body, rope=rope_tables is not None),
        out_shape=jax.ShapeDtypeStruct((m, n), BF16),
        grid=(m // tm, n // tn),
        in_specs=in_specs,
        out_specs=pl.BlockSpec((tm, tn), lambda i, j: (i, j)),
        scratch_shapes=[pltpu.VMEM((tm, kdim), BF16)],
        compiler_params=_params(("arbitrary", "arbitrary"),
                                _vmem_limit(blocks, scratch_bytes=tm * kdim * 2,
                                            temp_bytes=2 * tm * tn * 4 + 3 * tm * kdim * 4)),
        name=name,
    )(*args)


def _mm_acc_body(x_ref, w_ref, o_ref, acc_ref):
    kk = pl.program_id(2)

    @pl.when(kk == 0)
    def _():
        acc_ref[...] = jnp.zeros_like(acc_ref)

    acc_ref[...] += jnp.dot(x_ref[...], w_ref[...], preferred_element_type=F32)

    @pl.when(kk == pl.num_programs(2) - 1)
    def _():
        o_ref[...] = acc_ref[...].astype(o_ref.dtype)


def _matmul_ksplit(x, w, *, tm=1024, tn=1024, tk=2048, name):
    m, k = x.shape
    n = w.shape[1]
    blocks = tm * tk * 2 + tk * tn * 2 + tm * tn * 2
    return pl.pallas_call(
        _mm_acc_body,
        out_shape=jax.ShapeDtypeStruct((m, n), BF16),
        grid=(m // tm, n // tn, k // tk),
        in_specs=[pl.BlockSpec((tm, tk), lambda i, j, kk: (i, kk)),
                  pl.BlockSpec((tk, tn), lambda i, j, kk: (kk, j))],
        out_specs=pl.BlockSpec((tm, tn), lambda i, j, kk: (i, j)),
        scratch_shapes=[pltpu.VMEM((tm, tn), F32)],
        compiler_params=_params(("arbitrary", "arbitrary", "arbitrary"),
                                _vmem_limit(blocks, scratch_bytes=tm * tn * 4, temp_bytes=2 * tm * tn * 4)),
        name=name,
    )(x, w)


def _mix_body(h_ref, a_ref, b_ref, wga_ref, wgb_ref, wa_ref, wb_ref, o_ref):
    h = h_ref[...]
    ga = jax.nn.sigmoid(jnp.dot(h, wga_ref[...], preferred_element_type=F32))
    t = ga * jnp.dot(a_ref[...], wa_ref[...], preferred_element_type=F32)
    gb = jax.nn.sigmoid(jnp.dot(h, wgb_ref[...], preferred_element_type=F32))
    t = t + gb * jnp.dot(b_ref[...], wb_ref[...], preferred_element_type=F32)
    o_ref[...] = t.astype(o_ref.dtype)


def _gated_mix(h, a, b, w_gate, w_pa, w_pb, *, tm=512, tn=512, riders=()):
    m, d = h.shape
    ka = a.shape[1]
    kb = b.shape[1]
    n = w_pa.shape[1]
    gm, nb = m // tm, n // tn
    r_in, r_out, r_shapes, r_bytes, r_plan, r_args = _rider_specs(riders, gm * nb, lambda i, j: i * nb + j)
    blocks = tm * (d + ka + kb) * 2 + (2 * d + ka + kb) * tn * 2 + tm * tn * 2 + r_bytes
    return pl.pallas_call(
        _with_riders(_mix_body, 7, 1, r_plan),
        out_shape=[jax.ShapeDtypeStruct((m, n), BF16)] + r_shapes,
        grid=(gm, nb),
        in_specs=[pl.BlockSpec((tm, d), lambda i, j: (i, 0)),
                  pl.BlockSpec((tm, ka), lambda i, j: (i, 0)),
                  pl.BlockSpec((tm, kb), lambda i, j: (i, 0)),
                  pl.BlockSpec((d, tn), lambda i, j: (0, j)),
                  pl.BlockSpec((d, tn), lambda i, j: (0, j + nb)),
                  pl.BlockSpec((ka, tn), lambda i, j: (0, j)),
                  pl.BlockSpec((kb, tn), lambda i, j: (0, j))] + r_in,
        out_specs=[pl.BlockSpec((tm, tn), lambda i, j: (i, j))] + r_out,
        compiler_params=_params(("arbitrary", "arbitrary"), _vmem_limit(blocks, temp_bytes=4 * tm * tn * 4)),
        name="mix",
    )(h, a, b, w_gate, w_gate, w_pa, w_pb, *r_args)


def _mla_body(q_ref, kn_ref, kp_ref, v_ref, o_ref, *, seq, tq, tk):
    c = (MLA_NOPE + MLA_ROPE) ** -0.5 * math.log2(math.e)
    ones = jnp.ones((tk, LANES), BF16)
    row = lax.broadcasted_iota(jnp.int32, (tq, tk), 0)
    col = lax.broadcasted_iota(jnp.int32, (tq, tk), 1)
    for qi in range(seq // tq):
        q = q_ref[qi * tq:(qi + 1) * tq, :]
        n_kb = -(-((qi + 1) * tq) // tk)
        m = None
        acc = None
        for kj in range(n_kb):
            ks = slice(kj * tk, (kj + 1) * tk)
            kf = jnp.concatenate([kn_ref[ks, :], kp_ref[ks, :]], axis=1)
            s = lax.dot_general(q, kf, (((1,), (1,)), ((), ())), preferred_element_type=F32)
            if (kj + 1) * tk > qi * tq + CHUNK:
                visible = (col + kj * tk) // CHUNK <= (row + qi * tq) // CHUNK
                s = jnp.where(visible, s, NEG)
            bm = jnp.max(s, axis=-1, keepdims=True)
            v_ext = jnp.concatenate([v_ref[ks, :], ones], axis=1)
            if m is None:
                m = bm
                p = jnp.exp2((s - m) * c)
                acc = jnp.dot(p.astype(BF16), v_ext, preferred_element_type=F32)
            else:
                m_new = jnp.maximum(m, bm)
                alpha = jnp.exp2((m - m_new) * c)
                p = jnp.exp2((s - m_new) * c)
                acc = alpha * acc + jnp.dot(p.astype(BF16), v_ext, preferred_element_type=F32)
                m = m_new
        o_ref[qi * tq:(qi + 1) * tq, :] = (acc[:, :MLA_V] / acc[:, MLA_V:]).astype(o_ref.dtype)


def _mla_attention(q, kv, kpe, batch, seq, tq=256, tk=256, riders=()):
    rows = q.shape[0]
    r_in, r_out, r_shapes, r_bytes, r_plan, r_args = _rider_specs(
        riders, batch * MLA_HEADS, lambda b, h: b * MLA_HEADS + h)
    blocks = seq * MLA_QK_PAD * 2 + 3 * seq * LANES * 2 + seq * MLA_V * 2 + r_bytes
    return pl.pallas_call(
        _with_riders(functools.partial(_mla_body, seq=seq, tq=tq, tk=tk), 4, 1, r_plan),
        out_shape=[jax.ShapeDtypeStruct((rows, MLA_HEADS * MLA_V), BF16)] + r_shapes,
        grid=(batch, MLA_HEADS),
        in_specs=[pl.BlockSpec((seq, MLA_QK_PAD), lambda b, h: (b, h)),
                  pl.BlockSpec((seq, MLA_NOPE), lambda b, h: (b, h)),
                  pl.BlockSpec((seq, LANES), lambda b, h: (b, 0)),
                  pl.BlockSpec((seq, MLA_V), lambda b, h: (b, MLA_HEADS + h))] + r_in,
        out_specs=[pl.BlockSpec((seq, MLA_V), lambda b, h: (b, h))] + r_out,
        compiler_params=_params(("arbitrary", "arbitrary"), _vmem_limit(blocks, temp_bytes=8 << 20)),
        name="mla",
    )(q, kv, kpe, kv, *r_args)


def _t5_bucket_table():
    iq = np.arange(SWA_BLOCK)
    ik = np.arange(2 * SWA_BLOCK) - SWA_BLOCK
    rel = ik[None, :] - iq[:, None]
    nb = NUM_BUCKETS // 2
    max_exact = nb // 2
    ret = np.where(rel > 0, nb, 0)
    n = np.abs(rel)
    nf = np.maximum(n, 1).astype(np.float64)
    large = max_exact + (np.log(nf / max_exact) / math.log(MAX_DISTANCE / max_exact)
                         * (nb - max_exact)).astype(np.int32)
    large = np.minimum(large, nb - 1)
    return (ret + np.where(n < max_exact, n, large)).astype(np.int32)


def _bias_body(rb_ref, bucket_ref, o_ref):
    hd = pl.program_id(0)
    bk = bucket_ref[...]
    tab = jnp.zeros(bk.shape, F32)
    for b in range(NUM_BUCKETS):
        tab = jnp.where(bk == b, rb_ref[b, hd] * LOG2E, tab)
    col = lax.broadcasted_iota(jnp.int32, bk.shape, 1)
    qc = lax.broadcasted_iota(jnp.int32, bk.shape, 0) // CHUNK
    kc = col // CHUNK - SWA_BLOCK // CHUNK
    valid = jnp.logical_and(kc <= qc, kc >= qc - WINDOW // CHUNK)
    o_ref[1] = jnp.where(valid, tab, NEG)
    o_ref[0] = jnp.where(jnp.logical_and(valid, col >= SWA_BLOCK), tab, NEG)


def _bias_table(rel_bias):
    bucket = jnp.asarray(_t5_bucket_table())
    return pl.pallas_call(
        _bias_body,
        out_shape=jax.ShapeDtypeStruct((2, SWA_HEADS, SWA_BLOCK, 2 * SWA_BLOCK), F32),
        grid=(SWA_HEADS,),
        in_specs=[pl.BlockSpec(memory_space=pltpu.SMEM),
                  pl.BlockSpec((SWA_BLOCK, 2 * SWA_BLOCK), lambda h: (0, 0))],
        out_specs=pl.BlockSpec((2, None, SWA_BLOCK, 2 * SWA_BLOCK), lambda h: (0, h, 0, 0)),
        compiler_params=_params(("arbitrary",), 16 << 20),
        name="swa_bias",
    )(rel_bias, bucket)


def _swa_body(sink_ref, q_ref, kp_ref, kc_ref, vp_ref, vc_ref, bias_ref, o_ref, *, n_sub):
    c = SWA_HEAD_DIM ** -0.5 * LOG2E
    blk = SWA_BLOCK
    first_table = jnp.minimum(pl.program_id(1), 1)
    low = lax.broadcasted_iota(jnp.int32, ((n_sub + 1) * blk, LANES), 1) < SWA_HEAD_DIM
    low_q = lax.broadcasted_iota(jnp.int32, (blk, LANES), 1) < SWA_HEAD_DIM

    def both_halves(pair, want_low):
        other = pltpu.roll(pair, SWA_HEAD_DIM, 1)
        return jnp.where(low, pair, other) if want_low else jnp.where(low, other, pair)

    k2, v2 = [], []
    for j in range(SWA_KV_HEADS // 2):
        cols = slice(j * LANES, (j + 1) * LANES)
        kpair = jnp.concatenate([kp_ref[:, cols], kc_ref[:, cols]], axis=0)
        vpair = jnp.concatenate([vp_ref[:, cols], vc_ref[:, cols]], axis=0)
        k2 += [both_halves(kpair, True), both_halves(kpair, False)]
        v2 += [both_halves(vpair, True), both_halves(vpair, False)]

    zero = jnp.zeros((blk, LANES), BF16)
    for r in range(n_sub):
        q_rows = slice(r * blk, (r + 1) * blk)
        band = slice(r * blk, (r + 2) * blk)
        table = first_table if r == 0 else 1
        for p in range(SWA_HEADS // 2):
            g = (2 * p) // SWA_GROUP
            qp = q_ref[q_rows, p * LANES:(p + 1) * LANES]
            qs = jnp.concatenate([jnp.where(low_q, qp, zero), jnp.where(low_q, zero, qp)], axis=0)
            s = lax.dot_general(qs, k2[g][band], (((1,), (1,)), ((), ())), preferred_element_type=F32)
            t = s * c + bias_ref[table, 2 * p:2 * p + 2].reshape(2 * blk, 2 * blk)
            es, dens = [], []
            for half in range(2):
                th = t[half * blk:(half + 1) * blk]
                sk = sink_ref[0, 2 * p + half] * LOG2E
                m = jnp.maximum(jnp.max(th, axis=-1, keepdims=True), sk)
                e = jnp.exp2(th - m)
                es.append(e.astype(BF16))
                dens.append(jnp.sum(e, axis=-1, keepdims=True) + jnp.exp2(sk - m))
            o2 = jnp.dot(jnp.concatenate(es, axis=0), v2[g][band], preferred_element_type=F32)
            o_ref[q_rows, p * LANES:(p + 1) * LANES] = jnp.where(
                low_q, o2[:blk] / dens[0], o2[blk:] / dens[1]).astype(o_ref.dtype)


def _swa_attention(proj, sinks, bias_tab, batch, seq, qcol, kcol, vcol, n_sub=4, riders=()):
    rows = proj.shape[0]
    blk = SWA_BLOCK
    steps = seq // (n_sub * blk)
    cur = lambda b, n: b * steps + n
    prev = lambda b, n: (b * steps + n) * n_sub - jnp.minimum(n, 1)
    r_in, r_out, r_shapes, r_bytes, r_plan, r_args = _rider_specs(riders, batch * steps, cur)
    blocks = (2 * n_sub * blk * SWA_Q_WIDTH * 2 + 2 * (n_sub + 1) * blk * SWA_KV_WIDTH * 2
              + 2 * SWA_HEADS * blk * 2 * blk * 4 + r_bytes)
    return pl.pallas_call(
        _with_riders(functools.partial(_swa_body, n_sub=n_sub), 7, 1, r_plan),
        out_shape=[jax.ShapeDtypeStruct((rows, SWA_Q_WIDTH), BF16)] + r_shapes,
        grid=(batch, steps),
        in_specs=[pl.BlockSpec(memory_space=pltpu.SMEM),
                  pl.BlockSpec((n_sub * blk, SWA_Q_WIDTH), lambda b, n: (cur(b, n), qcol)),
                  pl.BlockSpec((blk, SWA_KV_WIDTH), lambda b, n: (prev(b, n), kcol)),
                  pl.BlockSpec((n_sub * blk, SWA_KV_WIDTH), lambda b, n: (cur(b, n), kcol)),
                  pl.BlockSpec((blk, SWA_KV_WIDTH), lambda b, n: (prev(b, n), vcol)),
                  pl.BlockSpec((n_sub * blk, SWA_KV_WIDTH), lambda b, n: (cur(b, n), vcol)),
                  pl.BlockSpec((2, SWA_HEADS, blk, 2 * blk), lambda b, n: (0, 0, 0, 0))] + r_in,
        out_specs=[pl.BlockSpec((n_sub * blk, SWA_Q_WIDTH), lambda b, n: (cur(b, n), 0))] + r_out,
        compiler_params=_params(("arbitrary", "arbitrary"), _vmem_limit(blocks, temp_bytes=8 << 20)),
        name="swa",
    )(sinks, proj, proj, proj, proj, proj, bias_tab, *r_args)


def _rope_tables(positions):
    half = MLA_ROPE // 2
    inv = ROPE_THETA ** (-jnp.arange(half, dtype=F32) * (2.0 / MLA_ROPE))
    one, zero = jnp.ones((half,), F32), jnp.zeros((half,), F32)
    ang = positions.astype(F32).reshape(-1, 1) * jnp.concatenate([inv, zero, inv, zero])
    return (jnp.cos(ang) * jnp.concatenate([one, zero, one, zero]),
            jnp.sin(ang) * jnp.concatenate([-one, zero, one, zero]))


def kernel(x, c, positions, w_ada, b_ada, pre_norm_g, post_norm_g, w_in, q_norm_g, kv_norm_g, w_uq, w_ukv,
           swa_sinks, rel_bias, w_gate, w_proj_a, w_proj_b, w_out, w_ff_up, w_ff_down):
    batch, seq, d = x.shape
    depth = w_ada.shape[0]
    rows = batch * seq
    x2 = x.reshape(rows, d)

    cos_t, sin_t = _rope_tables(positions)
    bias_tab = _bias_table(rel_bias.astype(F32))
    c_pad = jnp.concatenate([c, jnp.zeros((16 - batch, d), c.dtype)], axis=0)

    o_qlat, o_kvlat, o_rope = 0, Q_LORA, Q_LORA + KV_LORA
    o_qs = o_rope + MLA_ROPE
    o_ks = o_qs + SWA_Q_WIDTH
    o_vs = o_ks + SWA_KV_WIDTH

    for l in range(depth):
        w_main, w_rope = _regroup_w_in(
            w_in[l].astype(BF16),
            ((0, o_qs, SWA_Q_WIDTH), (SWA_Q_WIDTH, o_qlat, o_rope - o_qlat),
             (SWA_Q_WIDTH + o_rope - o_qlat, o_ks, 2 * SWA_KV_WIDTH)), o_rope)
        w_q3 = w_uq[l].astype(BF16).reshape(Q_LORA, MLA_HEADS, MLA_NOPE + MLA_ROPE)
        w_q = jnp.concatenate([w_q3[:, :, :MLA_NOPE], _spread_rope_columns(w_q3[:, :, MLA_NOPE:], 2)],
                              axis=2).reshape(Q_LORA, MLA_HEADS * MLA_QK_PAD)
        w_kv3 = w_ukv[l].reshape(KV_LORA, MLA_HEADS, MLA_NOPE + MLA_V)
        w_kv = jnp.concatenate([w_kv3[:, :, :MLA_NOPE].reshape(KV_LORA, -1),
                                w_kv3[:, :, MLA_NOPE:].reshape(KV_LORA, -1)], axis=1).astype(BF16)
        qcol_s, col_qlat, col_kvlat = 0, SWA_Q_WIDTH // Q_LORA, (SWA_Q_WIDTH + Q_LORA) // KV_LORA
        col_ks = (SWA_Q_WIDTH + Q_LORA + KV_LORA) // SWA_KV_WIDTH
        col_vs = col_ks + 1

        b_ada3 = b_ada.reshape(depth, 1, -1)
        mod_in = _modulation(c_pad, w_ada, b_ada3, l, 2 * d)[:batch].reshape(batch, 2, d)

        proj, h, kpe = _produce_matmul([x2], mod_in, [pre_norm_g[l, 0].reshape(1, d)], w_main, seq,
                                       produce=_token_mixer_input, keep_lhs=True,
                                       rope_side=(w_rope, cos_t, sin_t), tn=512, name="mm_in")
        q = _matmul_norm(proj, col_qlat, Q_LORA, q_norm_g[l].reshape(1, -1), w_q,
                         rope_tables=(cos_t, sin_t), tn=2048, name="mm_uq")
        kv = _matmul_norm(proj, col_kvlat, KV_LORA, kv_norm_g[l].reshape(1, -1), w_kv, tn=2048, name="mm_ukv")
        att_a, w_gate_b, w_pa_b, w_pb_b, mod_rest = _mla_attention(
            q, kv, kpe, batch, seq, riders=[("cast", w_gate, l), ("cast", w_proj_a, l), ("cast", w_proj_b, l),
                                            ("mod", c_pad, w_ada, b_ada3, l, 2 * d)])
        mod3 = jnp.concatenate([mod_in, mod_rest[:batch].reshape(batch, 4, d)], axis=1)
        att_b, w_out_b = _swa_attention(proj, swa_sinks[l].reshape(1, -1).astype(F32), bias_tab, batch, seq,
                                        qcol_s, col_ks, col_vs, riders=[("cast", w_out, l)])
        mix, w_up_b = _gated_mix(h, att_a, att_b, w_gate_b, w_pa_b, w_pb_b, riders=[("cast", w_ff_up, l)])
        (y,) = _matmul(mix, w_out_b, name="mm_out")
        post_g1 = post_norm_g[l, 0].reshape(1, d)

        u, w_down_b = _produce_matmul([x2, y], mod3, [post_g1, pre_norm_g[l, 1].reshape(1, d)], w_up_b, seq,
                                      produce=_channel_mixer_input, act="relu2", riders=[("cast", w_ff_down, l)],
                                      name="mm_up")
        y2 = _matmul_ksplit(u, w_down_b, tk=4096, name="mm_down")
        x2 = _post2(x2, y, y2, mod3, post_g1, post_norm_g[l, 1].reshape(1, d), seq)

    return x2.reshape(batch, seq, d)
```

```python
import functools
import math

import numpy as np
import jax
import jax.numpy as jnp
from jax import lax
from jax.experimental import pallas as pl
from jax.experimental.pallas import tpu as pltpu

F32 = jnp.float32
BF16 = jnp.bfloat16

CHUNK = 64
MLA_HEADS = 16
MLA_NOPE = 128
MLA_ROPE = 64
MLA_V = 128
Q_LORA = 1024
KV_LORA = 512
ROPE_THETA = 10000.0
SWA_HEADS = 32
SWA_KV_HEADS = 4
SWA_HEAD_DIM = 64
SWA_GROUP = SWA_HEADS // SWA_KV_HEADS
WINDOW = 128
SWA_BLOCK = 128
NUM_BUCKETS = 32
MAX_DISTANCE = 128
EPS = 1e-6
NEG = -1e30
LOG2E = math.log2(math.e)

LANES = 128
ROW_CHUNK = 16
ROW_UNROLL = 4
MLA_QK_PAD = 256
SWA_Q_WIDTH = SWA_HEADS * SWA_HEAD_DIM
SWA_KV_WIDTH = SWA_KV_HEADS * SWA_HEAD_DIM
VMEM_CAP_BYTES = 56 * 1024 * 1024


def _vmem_limit(block_bytes, scratch_bytes=0, temp_bytes=0):
    need = 2 * block_bytes + scratch_bytes + temp_bytes + (4 << 20)
    return int(min(max(need, 16 << 20), VMEM_CAP_BYTES))


def _params(sem, vmem):
    return pltpu.CompilerParams(dimension_semantics=sem, vmem_limit_bytes=vmem)


def _rms(x):
    return x * lax.rsqrt(jnp.mean(x * x, axis=-1, keepdims=True) + EPS)


def _rope_half_swap(x, cos_t, sin_t):
    return x * cos_t + pltpu.roll(x, LANES // 2, 1) * sin_t


def _spread_rope_columns(w, axis):
    half = MLA_ROPE // 2
    x1, x2 = jnp.split(w, 2, axis=axis)
    zero = jnp.zeros_like(x1)
    assert x1.shape[axis] == half and 4 * half == LANES
    return jnp.concatenate([x1, zero, x2, zero], axis=axis)


def _regroup_w_in_body(w_ref, main_ref, rope_ref, *, segments, rope_at):
    for dst, src, width in segments:
        main_ref[:, dst:dst + width] = w_ref[:, src:src + width]
    half = MLA_ROPE // 2
    zero = jnp.zeros((w_ref.shape[0], half), rope_ref.dtype)
    rope_ref[...] = jnp.concatenate([w_ref[:, rope_at:rope_at + half], zero,
                                     w_ref[:, rope_at + half:rope_at + MLA_ROPE], zero], axis=1)


def _regroup_w_in(w, segments, rope_at, tr=256):
    k, n_in = w.shape
    n_out = sum(width for _, _, width in segments)
    blocks = tr * (n_in + n_out + LANES) * 2
    return pl.pallas_call(
        functools.partial(_regroup_w_in_body, segments=segments, rope_at=rope_at),
        out_shape=(jax.ShapeDtypeStruct((k, n_out), w.dtype), jax.ShapeDtypeStruct((k, LANES), w.dtype)),
        grid=(k // tr,),
        in_specs=[pl.BlockSpec((tr, n_in), lambda i: (i, 0))],
        out_specs=(pl.BlockSpec((tr, n_out), lambda i: (i, 0)), pl.BlockSpec((tr, LANES), lambda i: (i, 0))),
        compiler_params=_params(("arbitrary",), _vmem_limit(blocks, temp_bytes=2 * tr * n_in * 2)),
        name="regroup_w_in",
    )(w)


def _mod_body(c_ref, w_ref, b_ref, o_ref):
    c = c_ref[...]
    ca = (c * jax.nn.sigmoid(c)).astype(BF16)
    o_ref[...] = jnp.dot(ca, w_ref[...].astype(BF16), preferred_element_type=F32) + b_ref[...]


def _modulation(c_pad, w_ada, b_ada, layer, n_cols, tn=512):
    m, k = c_pad.shape
    blocks = m * k * 4 + k * tn * 4 + tn * 4 + m * tn * 4
    return pl.pallas_call(
        _mod_body,
        out_shape=jax.ShapeDtypeStruct((m, n_cols), F32),
        grid=(n_cols // tn,),
        in_specs=[pl.BlockSpec((m, k), lambda j: (0, 0)),
                  pl.BlockSpec((None, k, tn), lambda j: (layer, 0, j)),
                  pl.BlockSpec((None, 1, tn), lambda j: (layer, 0, j))],
        out_specs=pl.BlockSpec((m, tn), lambda j: (0, j)),
        compiler_params=_params(("arbitrary",), _vmem_limit(blocks, temp_bytes=k * tn * 2)),
        name="mod",
    )(c_pad, w_ada, b_ada)


def _cast_body(src_ref, dst_ref):
    dst_ref[...] = src_ref[...].astype(dst_ref.dtype)


def _with_riders(body, n_in, n_out, plan):
    if not plan:
        return body
    r_in = sum(k_in for _, k_in, _ in plan)
    r_out = sum(k_out for _, _, k_out in plan)

    def wrapped(*refs):
        ins = refs[:n_in]
        rider_ins = refs[n_in:n_in + r_in]
        outs = refs[n_in + r_in:n_in + r_in + n_out]
        rider_outs = refs[n_in + r_in + n_out:n_in + r_in + n_out + r_out]
        scratch = refs[n_in + r_in + n_out + r_out:]
        a = b = 0
        for fn, k_in, k_out in plan:
            fn(*rider_ins[a:a + k_in], *rider_outs[b:b + k_out])
            a, b = a + k_in, b + k_out
        body(*ins, *outs, *scratch)

    return wrapped


def _rider_specs(riders, n_steps, step_of):
    in_specs, out_specs, out_shapes, plan, args, block_bytes = [], [], [], [], [], 0
    for kind, *rest in riders:
        if kind == "cast":
            w3, layer = rest
            _, r, cols = w3.shape
            tr = r // n_steps
            assert tr * n_steps == r and tr % 16 == 0, (w3.shape, n_steps)
            in_specs.append(pl.BlockSpec((None, tr, cols), lambda *g, layer=layer: (layer, step_of(*g), 0)))
            out_specs.append(pl.BlockSpec((tr, cols), lambda *g: (step_of(*g), 0)))
            out_shapes.append(jax.ShapeDtypeStruct((r, cols), BF16))
            plan.append((_cast_body, 1, 1))
            args.append(w3)
            block_bytes += tr * cols * (4 + 2)
        else:
            assert kind == "mod", kind
            c_pad, w_ada, b_ada, layer, c0 = rest
            m, k = c_pad.shape
            n_cols = w_ada.shape[2] - c0
            tc = n_cols // n_steps
            assert tc * n_steps == n_cols and tc % LANES == 0 and c0 % tc == 0, (w_ada.shape, c0, n_steps)
            col = lambda *g, first=c0 // tc: first + step_of(*g)
            in_specs += [pl.BlockSpec((m, k), lambda *g: (0, 0)),
                         pl.BlockSpec((None, k, tc), lambda *g, layer=layer, col=col: (layer, 0, col(*g))),
                         pl.BlockSpec((None, 1, tc), lambda *g, layer=layer, col=col: (layer, 0, col(*g)))]
            out_specs.append(pl.BlockSpec((m, tc), lambda *g: (0, step_of(*g))))
            out_shapes.append(jax.ShapeDtypeStruct((m, n_cols), F32))
            plan.append((_mod_body, 3, 1))
            args += [c_pad, w_ada, b_ada]
            block_bytes += m * k * 4 + k * tc * (4 + 2) + tc * 4 + m * tc * 4
    return in_specs, out_specs, out_shapes, block_bytes, plan, args


def _token_mixer_input(src, mod_ref, vec, rows):
    (x_ref,), (g_ref,) = src, vec
    return _rms(x_ref[rows, :]) * (g_ref[...] * (1.0 + mod_ref[1:2, :])) + mod_ref[0:1, :]


def _channel_mixer_input(src, mod_ref, vec, rows):
    (x_ref, y_ref), (pg_ref, g_ref) = src, vec
    x1 = x_ref[rows, :] + _rms(y_ref[rows, :].astype(F32)) * (mod_ref[2:3, :] * pg_ref[...])
    return _rms(x1) * (g_ref[...] * (1.0 + mod_ref[4:5, :])) + mod_ref[3:4, :]


def _produce_matmul_body(*refs, n_src, n_vec, produce, keep_lhs, rope_side, act):
    src, mod_ref = refs[:n_src], refs[n_src]
    vec, w_ref = refs[n_src + 1:n_src + 1 + n_vec], refs[n_src + 1 + n_vec]
    n_in = n_src + 2 + n_vec + 3 * rope_side
    side_w_ref, cos_ref, sin_ref = refs[n_in - 3:n_in] if rope_side else (None, None, None)
    outs = refs[n_in:]
    o_ref, lhs_out_ref = outs[0], (outs[1] if keep_lhs else None)
    side_out_ref = outs[1 + keep_lhs] if rope_side else None
    buf_a, buf_b = outs[-2:]
    i = pl.program_id(0)
    j = pl.program_id(1)
    slab = src[0].shape[0]

    def step(build_ref, ready_ref):
        for c in range(slab // ROW_CHUNK):
            rows = slice(c * ROW_CHUNK, (c + 1) * ROW_CHUNK)
            h = produce(src, mod_ref, vec, rows).astype(build_ref.dtype)
            dst = pl.ds(pl.multiple_of(j * slab + c * ROW_CHUNK, ROW_CHUNK), ROW_CHUNK)
            build_ref[dst, :] = h
            if keep_lhs:
                lhs_out_ref[rows, :] = h
        if ready_ref is None:
            return
        acc = jnp.dot(ready_ref[...], w_ref[...], preferred_element_type=F32)
        if act == "relu2":
            acc = jnp.maximum(acc, 0.0)
            acc = acc * acc
        o_ref[...] = acc.astype(o_ref.dtype)
        if rope_side:
            @pl.when(j == 0)
            def _():
                side = jnp.dot(ready_ref[...], side_w_ref[...], preferred_element_type=F32)
                side_out_ref[...] = _rope_half_swap(side, cos_ref[...], sin_ref[...]).astype(side_out_ref.dtype)

    @pl.when(i == 0)
    def _():
        step(buf_a, None)

    @pl.when(jnp.logical_and(i > 0, i % 2 == 0))
    def _():
        step(buf_a, buf_b)

    @pl.when(i % 2 == 1)
    def _():
        step(buf_b, buf_a)


def _produce_matmul(src, mod3, vec, w, seq, *, produce, keep_lhs=False, rope_side=None, act=None,
                    tm=1024, tn=1024, riders=(), name):
    m, d = src[0].shape
    n = w.shape[1]
    gm, gn = m // tm, n // tn
    slab = tm // gn
    assert slab * gn == tm and slab % ROW_CHUNK == 0 and seq % tm == 0
    build_tile = lambda i: jnp.minimum(i, gm - 1)
    ready_tile = lambda i: jnp.maximum(i - 1, 0)
    ready_col = lambda i, j: jnp.where(i == 0, 0, j)
    slab_of = lambda i, j: jnp.where(i < gm, i * gn + j, gm * gn - 1)
    slab_spec = pl.BlockSpec((slab, d), lambda i, j: (slab_of(i, j), 0))
    vec_spec = pl.BlockSpec((1, d), lambda i, j: (0, 0))
    r_in, r_out, r_shapes, r_bytes, r_plan, r_args = _rider_specs(
        riders, gm * gn, lambda i, j: ready_tile(i) * gn + ready_col(i, j))
    src_bytes = sum(slab * d * a.dtype.itemsize for a in src)
    blocks = src_bytes + (6 + len(vec)) * d * 4 + d * tn * 2 + tm * tn * 2 + keep_lhs * slab * d * 2 + r_bytes
    side_in, side_out, side_shapes, side_args = [], [], [], []
    if rope_side is not None:
        tile_rows = pl.BlockSpec((tm, LANES), lambda i, j: (ready_tile(i), 0))
        side_in = [pl.BlockSpec((d, LANES), lambda i, j: (0, 0)), tile_rows, tile_rows]
        side_out, side_shapes, side_args = [tile_rows], [jax.ShapeDtypeStruct((m, LANES), BF16)], list(rope_side)
        blocks += d * LANES * 2 + tm * LANES * (4 + 4 + 2)
    body = functools.partial(_produce_matmul_body, n_src=len(src), n_vec=len(vec), produce=produce,
                             keep_lhs=keep_lhs, rope_side=rope_side is not None, act=act)
    return pl.pallas_call(
        _with_riders(body, len(src) + len(vec) + 2 + len(side_in), 1 + keep_lhs + len(side_out), r_plan),
        out_shape=([jax.ShapeDtypeStruct((m, n), BF16)] + [jax.ShapeDtypeStruct((m, d), BF16)] * keep_lhs
                   + side_shapes + r_shapes),
        grid=(gm + 1, gn),
        in_specs=([slab_spec] * len(src)
                  + [pl.BlockSpec((None, mod3.shape[1], d), lambda i, j: (build_tile(i) * tm // seq, 0, 0))]
                  + [vec_spec] * len(vec)
                  + [pl.BlockSpec((d, tn), lambda i, j: (0, ready_col(i, j)))] + side_in + r_in),
        out_specs=([pl.BlockSpec((tm, tn), lambda i, j: (ready_tile(i), ready_col(i, j)))]
                   + [slab_spec] * keep_lhs + side_out + r_out),
        scratch_shapes=[pltpu.VMEM((tm, d), BF16), pltpu.VMEM((tm, d), BF16)],
        compiler_params=_params(("arbitrary", "arbitrary"),
                                _vmem_limit(blocks, scratch_bytes=2 * tm * d * 2, temp_bytes=2 * tm * tn * 4)),
        name=name,
    )(*src, mod3, *vec, w, *side_args, *r_args)


def _post2_body(x_ref, y_ref, y2_ref, mod_ref, pg1_ref, pg2_ref, o_ref):
    def chunk(r, carry):
        rows = pl.ds(pl.multiple_of(r * ROW_CHUNK, ROW_CHUNK), ROW_CHUNK)
        x1 = x_ref[rows, :] + _rms(y_ref[rows, :].astype(F32)) * (mod_ref[2:3, :] * pg1_ref[...])
        o_ref[rows, :] = x1 + _rms(y2_ref[rows, :].astype(F32)) * (mod_ref[5:6, :] * pg2_ref[...])
        return carry

    lax.fori_loop(0, x_ref.shape[0] // ROW_CHUNK, chunk, 0, unroll=ROW_UNROLL)


def _post2(x2, y, y2, mod3, post_g1, post_g2, seq, tr=512):
    rows, d = x2.shape
    per_b = seq // tr
    blocks = tr * d * (4 + 2 + 2 + 4) + 8 * d * 4
    chunk_temps = 8 * ROW_UNROLL * ROW_CHUNK * d * 4
    row = pl.BlockSpec((tr, d), lambda i: (i, 0))
    vec = pl.BlockSpec((1, d), lambda i: (0, 0))
    return pl.pallas_call(
        _post2_body,
        out_shape=jax.ShapeDtypeStruct((rows, d), F32),
        grid=(rows // tr,),
        in_specs=[row, row, row, pl.BlockSpec((None, 6, d), lambda i: (i // per_b, 0, 0)), vec, vec],
        out_specs=row,
        compiler_params=_params(("arbitrary",), _vmem_limit(blocks, temp_bytes=chunk_temps)),
        name="post2",
    )(x2, y, y2, mod3, post_g1, post_g2)


def _mm_body(x_ref, w_ref, o_ref):
    o_ref[...] = jnp.dot(x_ref[...], w_ref[...], preferred_element_type=F32).astype(o_ref.dtype)


def _matmul(x, w, *, tm=1024, tn=1024, riders=(), name):
    m, k = x.shape
    n = w.shape[1]
    tn = min(tn, n)
    gm, gn = m // tm, n // tn
    r_in, r_out, r_shapes, r_bytes, r_plan, r_args = _rider_specs(riders, gm * gn, lambda i, j: i * gn + j)
    blocks = tm * k * 2 + k * tn * 2 + tm * tn * 2 + r_bytes
    return pl.pallas_call(
        _with_riders(_mm_body, 2, 1, r_plan),
        out_shape=[jax.ShapeDtypeStruct((m, n), BF16)] + r_shapes,
        grid=(gm, gn),
        in_specs=[pl.BlockSpec((tm, k), lambda i, j: (i, 0)),
                  pl.BlockSpec((k, tn), lambda i, j: (0, j))] + r_in,
        out_specs=[pl.BlockSpec((tm, tn), lambda i, j: (i, j))] + r_out,
        compiler_params=_params(("arbitrary", "arbitrary"), _vmem_limit(blocks, temp_bytes=2 * tm * tn * 4)),
        name=name,
    )(x, w, *r_args)


def _mm_norm_body(x_ref, g_ref, w_ref, *rest, rope):
    xn_ref = rest[-1]

    @pl.when(pl.program_id(1) == 0)
    def _():
        xn_ref[...] = (_rms(x_ref[...].astype(F32)) * g_ref[...]).astype(BF16)

    acc = jnp.dot(xn_ref[...], w_ref[...], preferred_element_type=F32)
    if not rope:
        o_ref = rest[0]
        o_ref[...] = acc.astype(o_ref.dtype)
        return
    cos_ref, sin_ref, o_ref = rest[:3]
    cos_t = cos_ref[...]
    sin_t = sin_ref[...]
    for hd in range(acc.shape[1] // MLA_QK_PAD):
        c0 = hd * MLA_QK_PAD
        o_ref[:, c0:c0 + MLA_NOPE] = acc[:, c0:c0 + MLA_NOPE].astype(o_ref.dtype)
        pe = _rope_half_swap(acc[:, c0 + MLA_NOPE:c0 + MLA_QK_PAD], cos_t, sin_t)
        o_ref[:, c0 + MLA_NOPE:c0 + MLA_QK_PAD] = pe.astype(o_ref.dtype)


def _matmul_norm(x, xcol, kdim, g, w, *, rope_tables=None, tm=1024, tn=1024, name):
    m = x.shape[0]
    n = w.shape[1]
    blocks = tm * kdim * 2 + kdim * 4 + kdim * tn * 2 + tm * tn * 2
    in_specs = [pl.BlockSpec((tm, kdim), lambda i, j: (i, xcol)),
                pl.BlockSpec((1, kdim), lambda i, j: (0, 0)),
                pl.BlockSpec((kdim, tn), lambda i, j: (0, j))]
    args = [x, g, w]
    if rope_tables is not None:
        in_specs += [pl.BlockSpec((tm, LANES), lambda i, j: (i, 0))] * 2
        args += list(rope_tables)
        blocks += 2 * tm * LANES * 4
    return pl.pallas_call(
        functools.partial(_mm_norm_body, rope=rope_tables is not None),
        out_shape=jax.ShapeDtypeStruct((m, n), BF16),
        grid=(m // tm, n // tn),
        in_specs=in_specs,
        out_specs=pl.BlockSpec((tm, tn), lambda i, j: (i, j)),
        scratch_shapes=[pltpu.VMEM((tm, kdim), BF16)],
        compiler_params=_params(("arbitrary", "arbitrary"),
                                _vmem_limit(blocks, scratch_bytes=tm * kdim * 2,
                                            temp_bytes=2 * tm * tn * 4 + 3 * tm * kdim * 4)),
        name=name,
    )(*args)


def _mm_acc_body(x_ref, w_ref, o_ref, acc_ref):
    kk = pl.program_id(2)

    @pl.when(kk == 0)
    def _():
        acc_ref[...] = jnp.zeros_like(acc_ref)

    acc_ref[...] += jnp.dot(x_ref[...], w_ref[...], preferred_element_type=F32)

    @pl.when(kk == pl.num_programs(2) - 1)
    def _():
        o_ref[...] = acc_ref[...].astype(o_ref.dtype)


def _matmul_ksplit(x, w, *, tm=1024, tn=1024, tk=2048, name):
    m, k = x.shape
    n = w.shape[1]
    blocks = tm * tk * 2 + tk * tn * 2 + tm * tn * 2
    return pl.pallas_call(
        _mm_acc_body,
        out_shape=jax.ShapeDtypeStruct((m, n), BF16),
        grid=(m // tm, n // tn, k // tk),
        in_specs=[pl.BlockSpec((tm, tk), lambda i, j, kk: (i, kk)),
                  pl.BlockSpec((tk, tn), lambda i, j, kk: (kk, j))],
        out_specs=pl.BlockSpec((tm, tn), lambda i, j, kk: (i, j)),
        scratch_shapes=[pltpu.VMEM((tm, tn), F32)],
        compiler_params=_params(("arbitrary", "arbitrary", "arbitrary"),
                                _vmem_limit(blocks, scratch_bytes=tm * tn * 4, temp_bytes=2 * tm * tn * 4)),
        name=name,
    )(x, w)


def _mix_body(h_ref, a_ref, b_ref, wga_ref, wgb_ref, wa_ref, wb_ref, o_ref):
    h = h_ref[...]
    ga = jax.nn.sigmoid(jnp.dot(h, wga_ref[...], preferred_element_type=F32))
    t = ga * jnp.dot(a_ref[...], wa_ref[...], preferred_element_type=F32)
    gb = jax.nn.sigmoid(jnp.dot(h, wgb_ref[...], preferred_element_type=F32))
    t = t + gb * jnp.dot(b_ref[...], wb_ref[...], preferred_element_type=F32)
    o_ref[...] = t.astype(o_ref.dtype)


def _gated_mix(h, a, b, w_gate, w_pa, w_pb, *, tm=512, tn=512, riders=()):
    m, d = h.shape
    ka = a.shape[1]
    kb = b.shape[1]
    n = w_pa.shape[1]
    gm, nb = m // tm, n // tn
    r_in, r_out, r_shapes, r_bytes, r_plan, r_args = _rider_specs(riders, gm * nb, lambda i, j: i * nb + j)
    blocks = tm * (d + ka + kb) * 2 + (2 * d + ka + kb) * tn * 2 + tm * tn * 2 + r_bytes
    return pl.pallas_call(
        _with_riders(_mix_body, 7, 1, r_plan),
        out_shape=[jax.ShapeDtypeStruct((m, n), BF16)] + r_shapes,
        grid=(gm, nb),
        in_specs=[pl.BlockSpec((tm, d), lambda i, j: (i, 0)),
                  pl.BlockSpec((tm, ka), lambda i, j: (i, 0)),
                  pl.BlockSpec((tm, kb), lambda i, j: (i, 0)),
                  pl.BlockSpec((d, tn), lambda i, j: (0, j)),
                  pl.BlockSpec((d, tn), lambda i, j: (0, j + nb)),
                  pl.BlockSpec((ka, tn), lambda i, j: (0, j)),
                  pl.BlockSpec((kb, tn), lambda i, j: (0, j))] + r_in,
        out_specs=[pl.BlockSpec((tm, tn), lambda i, j: (i, j))] + r_out,
        compiler_params=_params(("arbitrary", "arbitrary"), _vmem_limit(blocks, temp_bytes=4 * tm * tn * 4)),
        name="mix",
    )(h, a, b, w_gate, w_gate, w_pa, w_pb, *r_args)


def _mla_body(q_ref, kn_ref, kp_ref, v_ref, o_ref, *, seq, tq, tk):
    c = (MLA_NOPE + MLA_ROPE) ** -0.5 * math.log2(math.e)
    ones = jnp.ones((tk, LANES), BF16)
    row = lax.broadcasted_iota(jnp.int32, (tq, tk), 0)
    col = lax.broadcasted_iota(jnp.int32, (tq, tk), 1)
    for qi in range(seq // tq):
        q = q_ref[qi * tq:(qi + 1) * tq, :]
        n_kb = -(-((qi + 1) * tq) // tk)
        m = None
        acc = None
        for kj in range(n_kb):
            ks = slice(kj * tk, (kj + 1) * tk)
            kf = jnp.concatenate([kn_ref[ks, :], kp_ref[ks, :]], axis=1)
            s = lax.dot_general(q, kf, (((1,), (1,)), ((), ())), preferred_element_type=F32)
            if (kj + 1) * tk > qi * tq + CHUNK:
                visible = (col + kj * tk) // CHUNK <= (row + qi * tq) // CHUNK
                s = jnp.where(visible, s, NEG)
            bm = jnp.max(s, axis=-1, keepdims=True)
            v_ext = jnp.concatenate([v_ref[ks, :], ones], axis=1)
            if m is None:
                m = bm
                p = jnp.exp2((s - m) * c)
                acc = jnp.dot(p.astype(BF16), v_ext, preferred_element_type=F32)
            else:
                m_new = jnp.maximum(m, bm)
                alpha = jnp.exp2((m - m_new) * c)
                p = jnp.exp2((s - m_new) * c)
                acc = alpha * acc + jnp.dot(p.astype(BF16), v_ext, preferred_element_type=F32)
                m = m_new
        o_ref[qi * tq:(qi + 1) * tq, :] = (acc[:, :MLA_V] / acc[:, MLA_V:]).astype(o_ref.dtype)


def _mla_attention(q, kv, kpe, batch, seq, tq=256, tk=256, riders=()):
    rows = q.shape[0]
    r_in, r_out, r_shapes, r_bytes, r_plan, r_args = _rider_specs(
        riders, batch * MLA_HEADS, lambda b, h: b * MLA_HEADS + h)
    blocks = seq * MLA_QK_PAD * 2 + 3 * seq * LANES * 2 + seq * MLA_V * 2 + r_bytes
    return pl.pallas_call(
        _with_riders(functools.partial(_mla_body, seq=seq, tq=tq, tk=tk), 4, 1, r_plan),
        out_shape=[jax.ShapeDtypeStruct((rows, MLA_HEADS * MLA_V), BF16)] + r_shapes,
        grid=(batch, MLA_HEADS),
        in_specs=[pl.BlockSpec((seq, MLA_QK_PAD), lambda b, h: (b, h)),
                  pl.BlockSpec((seq, MLA_NOPE), lambda b, h: (b, h)),
                  pl.BlockSpec((seq, LANES), lambda b, h: (b, 0)),
                  pl.BlockSpec((seq, MLA_V), lambda b, h: (b, MLA_HEADS + h))] + r_in,
        out_specs=[pl.BlockSpec((seq, MLA_V), lambda b, h: (b, h))] + r_out,
        compiler_params=_params(("arbitrary", "arbitrary"), _vmem_limit(blocks, temp_bytes=8 << 20)),
        name="mla",
    )(q, kv, kpe, kv, *r_args)


def _t5_bucket_table():
    iq = np.arange(SWA_BLOCK)
    ik = np.arange(2 * SWA_BLOCK) - SWA_BLOCK
    rel = ik[None, :] - iq[:, None]
    nb = NUM_BUCKETS // 2
    max_exact = nb // 2
    ret = np.where(rel > 0, nb, 0)
    n = np.abs(rel)
    nf = np.maximum(n, 1).astype(np.float64)
    large = max_exact + (np.log(nf / max_exact) / math.log(MAX_DISTANCE / max_exact)
                         * (nb - max_exact)).astype(np.int32)
    large = np.minimum(large, nb - 1)
    return (ret + np.where(n < max_exact, n, large)).astype(np.int32)


def _bias_body(rb_ref, bucket_ref, o_ref):
    hd = pl.program_id(0)
    bk = bucket_ref[...]
    tab = jnp.zeros(bk.shape, F32)
    for b in range(NUM_BUCKETS):
        tab = jnp.where(bk == b, rb_ref[b, hd] * LOG2E, tab)
    col = lax.broadcasted_iota(jnp.int32, bk.shape, 1)
    qc = lax.broadcasted_iota(jnp.int32, bk.shape, 0) // CHUNK
    kc = col // CHUNK - SWA_BLOCK // CHUNK
    valid = jnp.logical_and(kc <= qc, kc >= qc - WINDOW // CHUNK)
    o_ref[1] = jnp.where(valid, tab, NEG)
    o_ref[0] = jnp.where(jnp.logical_and(valid, col >= SWA_BLOCK), tab, NEG)


def _bias_table(rel_bias):
    bucket = jnp.asarray(_t5_bucket_table())
    return pl.pallas_call(
        _bias_body,
        out_shape=jax.ShapeDtypeStruct((2, SWA_HEADS, SWA_BLOCK, 2 * SWA_BLOCK), F32),
        grid=(SWA_HEADS,),
        in_specs=[pl.BlockSpec(memory_space=pltpu.SMEM),
                  pl.BlockSpec((SWA_BLOCK, 2 * SWA_BLOCK), lambda h: (0, 0))],
        out_specs=pl.BlockSpec((2, None, SWA_BLOCK, 2 * SWA_BLOCK), lambda h: (0, h, 0, 0)),
        compiler_params=_params(("arbitrary",), 16 << 20),
        name="swa_bias",
    )(rel_bias, bucket)


def _swa_body(sink_ref, q_ref, kp_ref, kc_ref, vp_ref, vc_ref, bias_ref, o_ref, *, n_sub):
    c = SWA_HEAD_DIM ** -0.5 * LOG2E
    blk = SWA_BLOCK
    first_table = jnp.minimum(pl.program_id(1), 1)
    low = lax.broadcasted_iota(jnp.int32, ((n_sub + 1) * blk, LANES), 1) < SWA_HEAD_DIM
    low_q = lax.broadcasted_iota(jnp.int32, (blk, LANES), 1) < SWA_HEAD_DIM

    def both_halves(pair, want_low):
        other = pltpu.roll(pair, SWA_HEAD_DIM, 1)
        return jnp.where(low, pair, other) if want_low else jnp.where(low, other, pair)

    k2, v2 = [], []
    for j in range(SWA_KV_HEADS // 2):
        cols = slice(j * LANES, (j + 1) * LANES)
        kpair = jnp.concatenate([kp_ref[:, cols], kc_ref[:, cols]], axis=0)
        vpair = jnp.concatenate([vp_ref[:, cols], vc_ref[:, cols]], axis=0)
        k2 += [both_halves(kpair, True), both_halves(kpair, False)]
        v2 += [both_halves(vpair, True), both_halves(vpair, False)]

    zero = jnp.zeros((blk, LANES), BF16)
    for r in range(n_sub):
        q_rows = slice(r * blk, (r + 1) * blk)
        band = slice(r * blk, (r + 2) * blk)
        table = first_table if r == 0 else 1
        for p in range(SWA_HEADS // 2):
            g = (2 * p) // SWA_GROUP
            qp = q_ref[q_rows, p * LANES:(p + 1) * LANES]
            qs = jnp.concatenate([jnp.where(low_q, qp, zero), jnp.where(low_q, zero, qp)], axis=0)
            s = lax.dot_general(qs, k2[g][band], (((1,), (1,)), ((), ())), preferred_element_type=F32)
            t = s * c + bias_ref[table, 2 * p:2 * p + 2].reshape(2 * blk, 2 * blk)
            es, dens = [], []
            for half in range(2):
                th = t[half * blk:(half + 1) * blk]
                sk = sink_ref[0, 2 * p + half] * LOG2E
                m = jnp.maximum(jnp.max(th, axis=-1, keepdims=True), sk)
                e = jnp.exp2(th - m)
                es.append(e.astype(BF16))
                dens.append(jnp.sum(e, axis=-1, keepdims=True) + jnp.exp2(sk - m))
            o2 = jnp.dot(jnp.concatenate(es, axis=0), v2[g][band], preferred_element_type=F32)
            o_ref[q_rows, p * LANES:(p + 1) * LANES] = jnp.where(
                low_q, o2[:blk] / dens[0], o2[blk:] / dens[1]).astype(o_ref.dtype)


def _swa_attention(proj, sinks, bias_tab, batch, seq, qcol, kcol, vcol, n_sub=4, riders=()):
    rows = proj.shape[0]
    blk = SWA_BLOCK
    steps = seq // (n_sub * blk)
    cur = lambda b, n: b * steps + n
    prev = lambda b, n: (b * steps + n) * n_sub - jnp.minimum(n, 1)
    r_in, r_out, r_shapes, r_bytes, r_plan, r_args = _rider_specs(riders, batch * steps, cur)
    blocks = (2 * n_sub * blk * SWA_Q_WIDTH * 2 + 2 * (n_sub + 1) * blk * SWA_KV_WIDTH * 2
              + 2 * SWA_HEADS * blk * 2 * blk * 4 + r_bytes)
    return pl.pallas_call(
        _with_riders(functools.partial(_swa_body, n_sub=n_sub), 7, 1, r_plan),
        out_shape=[jax.ShapeDtypeStruct((rows, SWA_Q_WIDTH), BF16)] + r_shapes,
        grid=(batch, steps),
        in_specs=[pl.BlockSpec(memory_space=pltpu.SMEM),
                  pl.BlockSpec((n_sub * blk, SWA_Q_WIDTH), lambda b, n: (cur(b, n), qcol)),
                  pl.BlockSpec((blk, SWA_KV_WIDTH), lambda b, n: (prev(b, n), kcol)),
                  pl.BlockSpec((n_sub * blk, SWA_KV_WIDTH), lambda b, n: (cur(b, n), kcol)),
                  pl.BlockSpec((blk, SWA_KV_WIDTH), lambda b, n: (prev(b, n), vcol)),
                  pl.BlockSpec((n_sub * blk, SWA_KV_WIDTH), lambda b, n: (cur(b, n), vcol)),
                  pl.BlockSpec((2, SWA_HEADS, blk, 2 * blk), lambda b, n: (0, 0, 0, 0))] + r_in,
        out_specs=[pl.BlockSpec((n_sub * blk, SWA_Q_WIDTH), lambda b, n: (cur(b, n), 0))] + r_out,
        compiler_params=_params(("arbitrary", "arbitrary"), _vmem_limit(blocks, temp_bytes=8 << 20)),
        name="swa",
    )(sinks, proj, proj, proj, proj, proj, bias_tab, *r_args)


def _rope_tables(positions):
    half = MLA_ROPE // 2
    inv = ROPE_THETA ** (-jnp.arange(half, dtype=F32) * (2.0 / MLA_ROPE))
    one, zero = jnp.ones((half,), F32), jnp.zeros((half,), F32)
    ang = positions.astype(F32).reshape(-1, 1) * jnp.concatenate([inv, zero, inv, zero])
    return (jnp.cos(ang) * jnp.concatenate([one, zero, one, zero]),
            jnp.sin(ang) * jnp.concatenate([-one, zero, one, zero]))


def kernel(x, c, positions, w_ada, b_ada, pre_norm_g, post_norm_g, w_in, q_norm_g, kv_norm_g, w_uq, w_ukv,
           swa_sinks, rel_bias, w_gate, w_proj_a, w_proj_b, w_out, w_ff_up, w_ff_down):
    batch, seq, d = x.shape
    depth = w_ada.shape[0]
    rows = batch * seq
    x2 = x.reshape(rows, d)

    cos_t, sin_t = _rope_tables(positions)
    bias_tab = _bias_table(rel_bias.astype(F32))
    c_pad = jnp.concatenate([c, jnp.zeros((16 - batch, d), c.dtype)], axis=0)

    o_qlat, o_kvlat, o_rope = 0, Q_LORA, Q_LORA + KV_LORA
    o_qs = o_rope + MLA_ROPE
    o_ks = o_qs + SWA_Q_WIDTH
    o_vs = o_ks + SWA_KV_WIDTH

    for l in range(depth):
        w_main, w_rope = _regroup_w_in(
            w_in[l].astype(BF16),
            ((0, o_qs, SWA_Q_WIDTH), (SWA_Q_WIDTH, o_qlat, o_rope - o_qlat),
             (SWA_Q_WIDTH + o_rope - o_qlat, o_ks, 2 * SWA_KV_WIDTH)), o_rope)
        w_q3 = w_uq[l].astype(BF16).reshape(Q_LORA, MLA_HEADS, MLA_NOPE + MLA_ROPE)
        w_q = jnp.concatenate([w_q3[:, :, :MLA_NOPE], _spread_rope_columns(w_q3[:, :, MLA_NOPE:], 2)],
                              axis=2).reshape(Q_LORA, MLA_HEADS * MLA_QK_PAD)
        w_kv3 = w_ukv[l].reshape(KV_LORA, MLA_HEADS, MLA_NOPE + MLA_V)
        w_kv = jnp.concatenate([w_kv3[:, :, :MLA_NOPE].reshape(KV_LORA, -1),
                                w_kv3[:, :, MLA_NOPE:].reshape(KV_LORA, -1)], axis=1).astype(BF16)
        qcol_s, col_qlat, col_kvlat = 0, SWA_Q_WIDTH // Q_LORA, (SWA_Q_WIDTH + Q_LORA) // KV_LORA
        col_ks = (SWA_Q_WIDTH + Q_LORA + KV_LORA) // SWA_KV_WIDTH
        col_vs = col_ks + 1

        b_ada3 = b_ada.reshape(depth, 1, -1)
        mod_in = _modulation(c_pad, w_ada, b_ada3, l, 2 * d)[:batch].reshape(batch, 2, d)

        proj, h, kpe = _produce_matmul([x2], mod_in, [pre_norm_g[l, 0].reshape(1, d)], w_main, seq,
                                       produce=_token_mixer_input, keep_lhs=True,
                                       rope_side=(w_rope, cos_t, sin_t), tn=512, name="mm_in")
        q = _matmul_norm(proj, col_qlat, Q_LORA, q_norm_g[l].reshape(1, -1), w_q,
                         rope_tables=(cos_t, sin_t), tn=4096, name="mm_uq")
        kv = _matmul_norm(proj, col_kvlat, KV_LORA, kv_norm_g[l].reshape(1, -1), w_kv, tn=4096, name="mm_ukv")
        att_a, w_gate_b, w_pa_b, w_pb_b, mod_rest = _mla_attention(
            q, kv, kpe, batch, seq, riders=[("cast", w_gate, l), ("cast", w_proj_a, l), ("cast", w_proj_b, l),
                                            ("mod", c_pad, w_ada, b_ada3, l, 2 * d)])
        mod3 = jnp.concatenate([mod_in, mod_rest[:batch].reshape(batch, 4, d)], axis=1)
        att_b, w_out_b = _swa_attention(proj, swa_sinks[l].reshape(1, -1).astype(F32), bias_tab, batch, seq,
                                        qcol_s, col_ks, col_vs, riders=[("cast", w_out, l)])
        mix, w_up_b = _gated_mix(h, att_a, att_b, w_gate_b, w_pa_b, w_pb_b, riders=[("cast", w_ff_up, l)])
        (y,) = _matmul(mix, w_out_b, name="mm_out")
        post_g1 = post_norm_g[l, 0].reshape(1, d)

        u, w_down_b = _produce_matmul([x2, y], mod3, [post_g1, pre_norm_g[l, 1].reshape(1, d)], w_up_b, seq,
                                      produce=_channel_mixer_input, act="relu2", riders=[("cast", w_ff_down, l)],
                                      name="mm_up")
        y2 = _matmul_ksplit(u, w_down_b, tk=4096, name="mm_down")
        x2 = _post2(x2, y, y2, mod3, post_g1, post_norm_g[l, 1].reshape(1, d), seq)

    return x2.reshape(batch, seq, d)
```
